```python
import jax, jax.numpy as jnp
from jax import lax
import numpy as np

D_MODEL = 1024
BATCH = 4
SEQ = 4096
DEPTH = 2

GRID_W = 64
CTX_LEN = 256
CONV_CH = 256
CONV_WIDTH = 31
NAT_HEADS = 4
NAT_HEAD_DIM = 64
NAT_W = NAT_HEADS * NAT_HEAD_DIM
NAT_KH = 8
NAT_KW = 16
FNET_GROUPS = 4
FNET_GROUP_DIM = 64
FNET_W = FNET_GROUPS * FNET_GROUP_DIM
GQA_Q_HEADS = 8
GQA_KV_HEADS = 2
GQA_HEAD_DIM = 64
GQA_Q_W = GQA_Q_HEADS * GQA_HEAD_DIM
GQA_KV_W = GQA_KV_HEADS * GQA_HEAD_DIM
GQA_WINDOW = 128
GQA_BLOCK = 128
ROPE_BASE = 10000.0
PEER_HEADS = 8
PEER_N_KEYS = 128
PEER_N_EXPERTS = PEER_N_KEYS * PEER_N_KEYS
PEER_TOPK = 16
PEER_D_KEY = 256
PEER_CHUNK = 128
N_BRANCH = 4
LN_EPS = 1e-5
NEG_INF = -1e30
DEEPNORM_ALPHA = (2 * DEPTH) ** 0.25
DEEPNORM_BETA = (8 * DEPTH) ** -0.25
IN_SIZES = (2 * CONV_CH, NAT_W, NAT_W, NAT_W, FNET_W, GQA_Q_W, GQA_KV_W, GQA_KV_W, N_BRANCH * D_MODEL)
IN_TOTAL = sum(IN_SIZES)

kernel_name = "hybrid_parallel_branch_peer_diffusion_block"


def layer_norm(x, g=None, b=None):
    xf = x.astype(jnp.float32)
    mu = jnp.mean(xf, -1, keepdims=True)
    var = jnp.mean(jnp.square(xf - mu), -1, keepdims=True)
    y = (xf - mu) * lax.rsqrt(var + LN_EPS)
    if g is not None:
        y = y * g.astype(jnp.float32) + b.astype(jnp.float32)
    return y.astype(x.dtype)


def modulate(x, shift, scale):
    return layer_norm(x) * (1.0 + scale) + shift


def split_in(p):
    offs = np.cumsum(IN_SIZES)[:-1].tolist()
    return jnp.split(p, offs, axis=-1)


def to_heads(t, h):
    return t.reshape(*t.shape[:-1], h, t.shape[-1] // h)


def conformer_conv(p_conv, conv_w, conv_b, ln_g, ln_b):
    a, g = jnp.split(p_conv, 2, axis=-1)
    h = a * jax.nn.sigmoid(g)
    h = lax.conv_general_dilated(h, conv_w[:, None, :], window_strides=(1,),
                                 padding=[(CONV_WIDTH // 2, CONV_WIDTH // 2)],
                                 dimension_numbers=('NWC', 'WIO', 'NWC'),
                                 feature_group_count=CONV_CH) + conv_b
    return jax.nn.silu(layer_norm(h, ln_g, ln_b))


def fourier_mix(p_f):
    B, T, _ = p_f.shape
    h = p_f.astype(jnp.float32).reshape(B, T, FNET_GROUPS, FNET_GROUP_DIM)
    h = jnp.fft.fftn(h, axes=(1, 3), norm='ortho').real
    return h.reshape(B, T, FNET_W).astype(p_f.dtype)


def axial_rope(x, rows, cols):
    half = x.shape[-1] // 2
    quarter = half // 2
    freqs = ROPE_BASE ** (-jnp.arange(quarter, dtype=jnp.float32) / quarter)

    def rot(xa, pos):
        ang = pos.astype(jnp.float32)[:, None] * freqs
        cos = jnp.cos(ang)[None, :, None, :]
        sin = jnp.sin(ang)[None, :, None, :]
        x1, x2 = xa[..., :quarter], xa[..., quarter:]
        return jnp.concatenate([x1 * cos - x2 * sin, x1 * sin + x2 * cos], -1)

    xf = x.astype(jnp.float32)
    out = jnp.concatenate([rot(xf[..., :half], rows), rot(xf[..., half:], cols)], -1)
    return out.astype(x.dtype)


def context_attention(q, k, v):
    B, L, H, dh = q.shape
    s = jnp.einsum('blhd,bmhd->bhlm', q, k).astype(jnp.float32) * dh ** -0.5
    p = jax.nn.softmax(s, -1).astype(q.dtype)
    return jnp.einsum('bhlm,bmhd->blhd', p, v).reshape(B, L, H * dh)


def neighbourhood_attention(q, k, v, kc, vc, rpb):
    B, S, H, dh = q.shape
    rows = S // GRID_W
    kh = min(NAT_KH, rows)
    scale = dh ** -0.5
    qg = q.reshape(B, rows, GRID_W, H, dh)
    kg = k.reshape(B, rows, GRID_W, H, dh)
    vg = v.reshape(B, rows, GRID_W, H, dh)
    col = jnp.arange(GRID_W)
    cstart = jnp.clip(col - NAT_KW // 2, 0, GRID_W - NAT_KW)
    cidx = cstart[:, None] + jnp.arange(NAT_KW)[None, :]
    dc = cidx - col[:, None] + (NAT_KW - 1)

    def one_row(r):
        rstart = jnp.clip(r - kh // 2, 0, rows - kh)
        q_r = lax.dynamic_index_in_dim(qg, r, axis=1, keepdims=False)
        k_band = lax.dynamic_slice_in_dim(kg, rstart, kh, axis=1)
        v_band = lax.dynamic_slice_in_dim(vg, rstart, kh, axis=1)
        k_nb = k_band[:, :, cidx]
        v_nb = v_band[:, :, cidx]
        dr = rstart + jnp.arange(kh) - r + (NAT_KH - 1)
        bias = rpb[:, dr][:, :, dc].transpose(0, 2, 1, 3)
        s_nb = jnp.einsum('bchd,bwckhd->bhcwk', q_r, k_nb).astype(jnp.float32) * scale
        s_nb = (s_nb + bias[None].astype(jnp.float32)).reshape(B, H, GRID_W, kh * NAT_KW)
        s_ctx = jnp.einsum('bchd,blhd->bhcl', q_r, kc).astype(jnp.float32) * scale
        probs = jax.nn.softmax(jnp.concatenate([s_nb, s_ctx], -1), -1).astype(q.dtype)
        p_nb = probs[..., :kh * NAT_KW].reshape(B, H, GRID_W, kh, NAT_KW)
        p_ctx = probs[..., kh * NAT_KW:]
        return (jnp.einsum('bhcwk,bwckhd->bchd', p_nb, v_nb)
                + jnp.einsum('bhcl,blhd->bchd', p_ctx, vc))

    out = lax.map(one_row, jnp.arange(rows))
    return out.transpose(1, 0, 2, 3, 4).reshape(B, S, H * dh)


def window_gqa_latent(q, k, v, kc, vc, sink):
    B, S, _, dh = q.shape
    G = GQA_KV_HEADS
    R = GQA_Q_HEADS // G
    L = kc.shape[1]
    nb = S // GQA_BLOCK
    scale = dh ** -0.5
    qb = q.reshape(B, nb, GQA_BLOCK, G, R, dh)
    pad = ((0, 0), (GQA_BLOCK, GQA_BLOCK), (0, 0), (0, 0))
    kp = jnp.pad(k, pad).reshape(B, nb + 2, GQA_BLOCK, G, dh)
    vp = jnp.pad(v, pad).reshape(B, nb + 2, GQA_BLOCK, G, dh)
    kband = jnp.concatenate([kp[:, :-2], kp[:, 1:-1], kp[:, 2:]], axis=2)
    vband = jnp.concatenate([vp[:, :-2], vp[:, 1:-1], vp[:, 2:]], axis=2)
    qpos = jnp.arange(S).reshape(nb, GQA_BLOCK)
    kpos = jnp.arange(nb)[:, None] * GQA_BLOCK - GQA_BLOCK + jnp.arange(3 * GQA_BLOCK)[None, :]
    kp3 = kpos[:, None, :]
    valid = (jnp.abs(kp3 - qpos[:, :, None]) <= GQA_WINDOW) & (kp3 >= 0) & (kp3 < S)
    s_band = jnp.einsum('bnqgrd,bnkgd->bgrnqk', qb, kband).astype(jnp.float32) * scale
    s_band = jnp.where(valid, s_band, NEG_INF)
    s_ctx = jnp.einsum('bnqgrd,blgd->bgrnql', qb, kc).astype(jnp.float32) * scale
    s_sink = jnp.broadcast_to(sink.astype(jnp.float32).reshape(1, G, R, 1, 1, 1),
                              (B, G, R, nb, GQA_BLOCK, 1))
    probs = jax.nn.softmax(jnp.concatenate([s_band, s_ctx, s_sink], -1), -1).astype(q.dtype)
    nk = 3 * GQA_BLOCK
    out = (jnp.einsum('bgrnqk,bnkgd->bnqgrd', probs[..., :nk], vband)
           + jnp.einsum('bgrnql,blgd->bnqgrd', probs[..., nk:nk + L], vc))
    return out.reshape(B, S, GQA_Q_HEADS * dh)


def gqa_context(q, k, v, sink):
    B, L, _, dh = q.shape
    G = GQA_KV_HEADS
    R = GQA_Q_HEADS // G
    qg = q.reshape(B, L, G, R, dh)
    s = jnp.einsum('blgrd,bmgd->bgrlm', qg, k).astype(jnp.float32) * dh ** -0.5
    s_sink = jnp.broadcast_to(sink.astype(jnp.float32).reshape(1, G, R, 1, 1), (B, G, R, L, 1))
    probs = jax.nn.softmax(jnp.concatenate([s, s_sink], -1), -1)[..., :L].astype(q.dtype)
    return jnp.einsum('bgrlm,bmgd->blgrd', probs, v).reshape(B, L, GQA_Q_HEADS * dh)


def merge_branches(y_a, y_b, y_c, y_d, gate_logits, w_a, w_b, w_c, w_d, w_o):
    g = jax.nn.sigmoid(gate_logits).reshape(*gate_logits.shape[:-1], N_BRANCH, D_MODEL)
    m = (g[..., 0, :] * (y_a @ w_a) + g[..., 1, :] * (y_b @ w_b)
         + g[..., 2, :] * (y_c @ w_c) + g[..., 3, :] * (y_d @ w_d))
    return m @ w_o


def peer_ffn(u, wq, k1, k2, eu, ev):
    B, T, D = u.shape
    xs = u.reshape(-1, PEER_CHUNK, D)
    half = PEER_D_KEY // 2

    def chunk(xc):
        q = (xc @ wq).reshape(PEER_CHUNK, PEER_HEADS, 2, half)
        s1 = jnp.einsum('thd,hnd->thn', q[:, :, 0], k1).astype(jnp.float32)
        s2 = jnp.einsum('thd,hnd->thn', q[:, :, 1], k2).astype(jnp.float32)
        v1, i1 = lax.top_k(s1, PEER_TOPK)
        v2, i2 = lax.top_k(s2, PEER_TOPK)
        cand = (v1[..., :, None] + v2[..., None, :]).reshape(PEER_CHUNK, PEER_HEADS, -1)
        cidx = (i1[..., :, None] * PEER_N_KEYS + i2[..., None, :]).reshape(PEER_CHUNK, PEER_HEADS, -1)
        top_s, pos = lax.top_k(cand, PEER_TOPK)
        eidx = jnp.take_along_axis(cidx, pos, -1)
        g = jax.nn.softmax(top_s, -1).astype(xc.dtype)
        h = jax.nn.gelu(jnp.einsum('thkd,td->thk', eu[eidx], xc), approximate=False)
        return jnp.einsum('thk,thkd->td', g * h, ev[eidx])

    return lax.map(chunk, xs).reshape(B, T, D)


def setup_inputs(seed: int = 0) -> dict:
    key = jax.random.key(seed)
    ks = jax.random.split(key, 28)
    f32 = jnp.float32
    D = D_MODEL
    L = DEPTH

    def nrm(k, shape, s):
        return jax.random.normal(k, shape, f32) * s

    return {
        "x": nrm(ks[0], (BATCH, SEQ, D), 1.0),
        "c": nrm(ks[1], (BATCH, D), 1.0),
        "ctx": nrm(ks[2], (BATCH, CTX_LEN, D), 1.0),
        "c_ctx": nrm(ks[3], (D,), 1.0),
        "w_ada": nrm(ks[4], (L, D, 6 * D), D ** -0.5),
        "b_ada": nrm(ks[5], (L, 6 * D), 0.01),
        "w_in": nrm(ks[6], (L, D, IN_TOTAL), D ** -0.5),
        "b_in": nrm(ks[7], (L, IN_TOTAL), 0.01),
        "conv_w": nrm(ks[8], (L, CONV_WIDTH, CONV_CH), CONV_WIDTH ** -0.5),
        "conv_b": nrm(ks[9], (L, CONV_CH), 0.01),
        "conv_ln_g": 1.0 + nrm(ks[10], (L, CONV_CH), 0.1),
        "conv_ln_b": nrm(ks[11], (L, CONV_CH), 0.01),
        "nat_rpb": nrm(ks[12], (L, NAT_HEADS, 2 * NAT_KH - 1, 2 * NAT_KW - 1), 0.1),
        "gqa_sink": nrm(ks[13], (L, GQA_Q_HEADS), 1.0),
        "w_branch_a": nrm(ks[14], (L, CONV_CH, D), CONV_CH ** -0.5),
        "w_branch_b": nrm(ks[15], (L, NAT_W, D), NAT_W ** -0.5),
        "w_branch_c": nrm(ks[16], (L, FNET_W, D), FNET_W ** -0.5),
        "w_branch_d": nrm(ks[17], (L, GQA_Q_W, D), GQA_Q_W ** -0.5),
        "w_out": nrm(ks[18], (L, D, D), D ** -0.5 * DEEPNORM_BETA),
        "ln1_g": 1.0 + nrm(ks[19], (L, D), 0.1),
        "ln1_b": nrm(ks[20], (L, D), 0.01),
        "peer_wq": nrm(ks[21], (L, D, PEER_HEADS * PEER_D_KEY), D ** -0.5),
        "peer_k1": nrm(ks[22], (L, PEER_HEADS, PEER_N_KEYS, PEER_D_KEY // 2), (PEER_D_KEY // 2) ** -0.5),
        "peer_k2": nrm(ks[23], (L, PEER_HEADS, PEER_N_KEYS, PEER_D_KEY // 2), (PEER_D_KEY // 2) ** -0.5),
        "peer_u": nrm(ks[24], (L, PEER_N_EXPERTS, D), D ** -0.5),
        "peer_v": nrm(ks[25], (L, PEER_N_EXPERTS, D), DEEPNORM_BETA),
        "ln2_g": 1.0 + nrm(ks[26], (L, D), 0.1),
        "ln2_b": nrm(ks[27], (L, D), 0.01),
    }


def reference(x, c, ctx, c_ctx, w_ada, b_ada, w_in, b_in, conv_w, conv_b, conv_ln_g, conv_ln_b,
              nat_rpb, gqa_sink, w_branch_a, w_branch_b, w_branch_c, w_branch_d, w_out,
              ln1_g, ln1_b, peer_wq, peer_k1, peer_k2, peer_u, peer_v, ln2_g, ln2_b):
    S = x.shape[1]
    t = jnp.arange(S)
    pos_row = t // GRID_W
    pos_col = t % GRID_W
    xc = ctx
    for i in range(DEPTH):
        mod_l = (jax.nn.silu(c) @ w_ada[i] + b_ada[i])[:, None, :]
        mod_c = jax.nn.silu(c_ctx) @ w_ada[i] + b_ada[i]
        sh1_l, sc1_l, g1_l, sh2_l, sc2_l, g2_l = jnp.split(mod_l, 6, axis=-1)
        sh1_c, sc1_c, g1_c, sh2_c, sc2_c, g2_c = jnp.split(mod_c, 6, axis=-1)

        pc = modulate(xc, sh1_c, sc1_c) @ w_in[i] + b_in[i]
        c_conv, c_nq, c_nk, c_nv, c_f, c_gq, c_gk, c_gv, c_gate = split_in(pc)
        nk_c = to_heads(c_nk, NAT_HEADS)
        nv_c = to_heads(c_nv, NAT_HEADS)
        gk_c = to_heads(c_gk, GQA_KV_HEADS)
        gv_c = to_heads(c_gv, GQA_KV_HEADS)

        pl = modulate(x, sh1_l, sc1_l) @ w_in[i] + b_in[i]
        l_conv, l_nq, l_nk, l_nv, l_f, l_gq, l_gk, l_gv, l_gate = split_in(pl)
        y_a = conformer_conv(l_conv, conv_w[i], conv_b[i], conv_ln_g[i], conv_ln_b[i])
        y_b = neighbourhood_attention(to_heads(l_nq, NAT_HEADS), to_heads(l_nk, NAT_HEADS),
                                      to_heads(l_nv, NAT_HEADS), nk_c, nv_c, nat_rpb[i])
        y_c = fourier_mix(l_f)
        y_d = window_gqa_latent(axial_rope(to_heads(l_gq, GQA_Q_HEADS), pos_row, pos_col),
                                axial_rope(to_heads(l_gk, GQA_KV_HEADS), pos_row, pos_col),
                                to_heads(l_gv, GQA_KV_HEADS), gk_c, gv_c, gqa_sink[i])
        mix = merge_branches(y_a, y_b, y_c, y_d, l_gate, w_branch_a[i], w_branch_b[i],
                             w_branch_c[i], w_branch_d[i], w_out[i])
        x = layer_norm(DEEPNORM_ALPHA * x + g1_l * mix, ln1_g[i], ln1_b[i])
        ffn = peer_ffn(modulate(x, sh2_l, sc2_l), peer_wq[i], peer_k1[i], peer_k2[i], peer_u[i], peer_v[i])
        x = layer_norm(DEEPNORM_ALPHA * x + g2_l * ffn, ln2_g[i], ln2_b[i])

        if i < DEPTH - 1:
            yc_a = conformer_conv(c_conv, conv_w[i], conv_b[i], conv_ln_g[i], conv_ln_b[i])
            yc_b = context_attention(to_heads(c_nq, NAT_HEADS), nk_c, nv_c)
            yc_c = fourier_mix(c_f)
            yc_d = gqa_context(to_heads(c_gq, GQA_Q_HEADS), gk_c, gv_c, gqa_sink[i])
            mix_c = merge_branches(yc_a, yc_b, yc_c, yc_d, c_gate, w_branch_a[i], w_branch_b[i],
                                   w_branch_c[i], w_branch_d[i], w_out[i])
            xc = layer_norm(DEEPNORM_ALPHA * xc + g1_c * mix_c, ln1_g[i], ln1_b[i])
            ffn_c = peer_ffn(modulate(xc, sh2_c, sc2_c), peer_wq[i], peer_k1[i], peer_k2[i],
                             peer_u[i], peer_v[i])
            xc = layer_norm(DEEPNORM_ALPHA * xc + g2_c * ffn_c, ln2_g[i], ln2_b[i])
    return x
```

```python
import functools
import math

import numpy as np
import jax
import jax.numpy as jnp
from jax import lax
from jax.experimental import pallas as pl
from jax.experimental.pallas import tpu as pltpu

f32 = jnp.float32
bf16 = jnp.bfloat16

D_MODEL = 1024
DEPTH = 2
GRID_W = 64
CONV_CH = 256
CONV_WIDTH = 31
NAT_HEADS = 4
NAT_KH = 8
NAT_KW = 16
GQA_Q_HEADS = 8
GQA_KV_HEADS = 2
GQA_BLOCK = 128
HEAD_DIM = 64
ROPE_BASE = 10000.0
PEER_HEADS = 8
PEER_N_KEYS = 128
PEER_TOPK = 16
LN_EPS = 1e-5
NEG_INF = -1e30
DEEPNORM_ALPHA = (2 * DEPTH) ** 0.25

LANES = 128
N_SMALL = 2304
VMEM_LIMIT = 56 * 1024 * 1024


def _cparams(sem, vmem=VMEM_LIMIT):
    return pltpu.CompilerParams(dimension_semantics=sem, vmem_limit_bytes=vmem)


def _ln(x):
    mu = jnp.mean(x, axis=-1, keepdims=True)
    xc = x - mu
    var = jnp.mean(xc * xc, axis=-1, keepdims=True)
    return xc * lax.rsqrt(var + LN_EPS)


def _dot(a, b):
    return jnp.dot(a, b, preferred_element_type=f32)


def _dot_nt(a, b):
    return lax.dot_general(a, b, (((1,), (1,)), ((), ())), preferred_element_type=f32)


def _ada_kernel(c_ref, w_ref, b_ref, o_ref):
    c = c_ref[...]
    h = c * jax.nn.sigmoid(c)
    o_ref[0] = jnp.dot(h, w_ref[0], preferred_element_type=f32,
                       precision=lax.Precision.HIGHEST) + b_ref[0]


def _ada_call(cvec, w_ada, b_ada):
    L, D, N = w_ada.shape
    tn = 1536
    return pl.pallas_call(
        _ada_kernel,
        grid=(L, N // tn),
        in_specs=[pl.BlockSpec((8, D), lambda l, j: (0, 0)),
                  pl.BlockSpec((1, D, tn), lambda l, j: (l, 0, j)),
                  pl.BlockSpec((1, 1, tn), lambda l, j: (l, 0, j))],
        out_specs=pl.BlockSpec((1, 8, tn), lambda l, j: (l, 0, j)),
        out_shape=jax.ShapeDtypeStruct((L, 8, N), f32),
        compiler_params=_cparams(("parallel", "parallel")),
        name="ada_mod",
    )(cvec, w_ada, b_ada.reshape(L, 1, N))


def _inproj_kernel(x_ref, sh_ref, sc_ref, w_ref, b_ref, o_ref):
    xm = _ln(x_ref[0]) * (1.0 + sc_ref[0]) + sh_ref[0]
    o_ref[0] = (_dot(xm.astype(bf16), w_ref[...]) + b_ref[...]).astype(o_ref.dtype)


def _inproj_call(x, sh, sc, w, b, tm):
    B, T, D = x.shape
    N = w.shape[1]
    return pl.pallas_call(
        _inproj_kernel,
        grid=(B, T // tm),
        in_specs=[pl.BlockSpec((1, tm, D), lambda b, i: (b, i, 0)),
                  pl.BlockSpec((1, 1, D), lambda b, i: (b, 0, 0)),
                  pl.BlockSpec((1, 1, D), lambda b, i: (b, 0, 0)),
                  pl.BlockSpec((D, N), lambda b, i: (0, 0)),
                  pl.BlockSpec((1, N), lambda b, i: (0, 0))],
        out_specs=pl.BlockSpec((1, tm, N), lambda b, i: (b, i, 0)),
        out_shape=jax.ShapeDtypeStruct((B, T, N), bf16),
        compiler_params=_cparams(("parallel", "parallel")),
        name="in_proj",
    )(x, sh, sc, w, b)


CONV_HALO = 16
CONV_SUB = 128


def _conv_kernel(prev_ref, cur_ref, next_ref, w_ref, cb_ref, g_ref, b_ref, o_ref, hs_ref, *, tc, nchunks):
    i = pl.program_id(1)

    def glu(v):
        v = v.astype(f32)
        return v[:, :CONV_CH] * jax.nn.sigmoid(v[:, CONV_CH:])

    hs_ref[0:CONV_HALO, :] = jnp.where(i > 0, glu(prev_ref[0]), 0.0)
    hs_ref[CONV_HALO:CONV_HALO + tc, :] = glu(cur_ref[0])
    hs_ref[CONV_HALO + tc:2 * CONV_HALO + tc, :] = jnp.where(i < nchunks - 1, glu(next_ref[0]), 0.0)
    base = CONV_HALO - CONV_WIDTH // 2
    for s in range(tc // CONV_SUB):
        acc = jnp.zeros((CONV_SUB, CONV_CH), f32)
        for j in range(CONV_WIDTH):
            acc = acc + hs_ref[pl.ds(s * CONV_SUB + base + j, CONV_SUB), :] * w_ref[j:j + 1, :]
        y = _ln(acc + cb_ref[...]) * g_ref[...] + b_ref[...]
        y = y * jax.nn.sigmoid(y)
        o_ref[0, s * CONV_SUB:(s + 1) * CONV_SUB, :] = y.astype(o_ref.dtype)


def _conv_call(p, conv_w, conv_b, ln_g, ln_b):
    B, T, _ = p.shape
    tc = min(512, T)
    nchunks = T // tc
    hb = tc // CONV_HALO
    nhb = T // CONV_HALO
    width = 2 * CONV_CH
    kern = functools.partial(_conv_kernel, tc=tc, nchunks=nchunks)
    vec = lambda v: v.reshape(1, CONV_CH)
    return pl.pallas_call(
        kern,
        grid=(B, nchunks),
        in_specs=[pl.BlockSpec((1, CONV_HALO, width), lambda b, i: (b, jnp.maximum(i * hb - 1, 0), 0)),
                  pl.BlockSpec((1, tc, width), lambda b, i: (b, i, 0)),
                  pl.BlockSpec((1, CONV_HALO, width), lambda b, i: (b, jnp.minimum((i + 1) * hb, nhb - 1), 0)),
                  pl.BlockSpec((CONV_WIDTH, CONV_CH), lambda b, i: (0, 0)),
                  pl.BlockSpec((1, CONV_CH), lambda b, i: (0, 0)),
                  pl.BlockSpec((1, CONV_CH), lambda b, i: (0, 0)),
                  pl.BlockSpec((1, CONV_CH), lambda b, i: (0, 0))],
        out_specs=pl.BlockSpec((1, tc, CONV_CH), lambda b, i: (b, i, 0)),
        out_shape=jax.ShapeDtypeStruct((B, T, CONV_CH), bf16),
        scratch_shapes=[pltpu.VMEM((tc + 2 * CONV_HALO, CONV_CH), f32)],
        compiler_params=_cparams(("parallel", "parallel")),
        name="conv_branch",
    )(p, p, p, conv_w, vec(conv_b), vec(ln_g), vec(ln_b))


FNET_W = 256
FNET_GROUP_DIM = 64
F_COL = 1280 // FNET_W


def _fft_kernel(f_ref, cs_ref, bd_ref, o_ref, rhs_ref, *, T, scale):
    i = pl.program_id(0)
    b = pl.program_id(1)

    @pl.when(i == 0)
    def _():
        rows = min(512, T)
        for r in range(T // rows):
            z = _dot(f_ref[0, r * rows:(r + 1) * rows, :], bd_ref[...])
            rhs_ref[b, r * rows:(r + 1) * rows, :] = z[:, :FNET_W].astype(bf16)
            rhs_ref[b, T + r * rows:T + (r + 1) * rows, :] = (-z[:, FNET_W:]).astype(bf16)

    o_ref[0] = (_dot(cs_ref[...], rhs_ref[b]) * scale).astype(o_ref.dtype)


def _dft_time_mats(T):
    t = jnp.arange(T, dtype=jnp.int32)
    ang = ((t[:, None] * t[None, :]) % T).astype(f32) * f32(2.0 * math.pi / T)
    return jnp.concatenate([jnp.cos(ang), jnp.sin(ang)], axis=1).astype(bf16)


def _dft_chan_mats():
    n = FNET_GROUP_DIM
    k = np.arange(n)
    ang = 2.0 * np.pi * ((k[:, None] * k[None, :]) % n) / n
    eye = np.eye(FNET_W // n)
    return jnp.asarray(np.concatenate([np.kron(eye, np.cos(ang)), np.kron(eye, np.sin(ang))], axis=1), f32).astype(bf16)


def _fft_call(p, cs, bd):
    B, T, _ = p.shape
    tm = min(512, T)
    kern = functools.partial(_fft_kernel, T=T, scale=1.0 / math.sqrt(T * FNET_GROUP_DIM))
    return pl.pallas_call(
        kern,
        grid=(T // tm, B),
        in_specs=[pl.BlockSpec((1, T, FNET_W), lambda i, b: (b, 0, F_COL)),
                  pl.BlockSpec((tm, 2 * T), lambda i, b: (i, 0)),
                  pl.BlockSpec((FNET_W, 2 * FNET_W), lambda i, b: (0, 0))],
        out_specs=pl.BlockSpec((1, tm, FNET_W), lambda i, b: (b, i, 0)),
        out_shape=jax.ShapeDtypeStruct((B, T, FNET_W), bf16),
        scratch_shapes=[pltpu.VMEM((B, 2 * T, FNET_W), bf16)],
        compiler_params=_cparams(("arbitrary", "arbitrary")),
        name="fourier_branch",
    )(p, cs, bd)


def _lane_is_low(shape):
    return lax.broadcasted_iota(jnp.int32, shape, len(shape) - 1) < HEAD_DIM


def _softmax_pv(logits, values, extra=None):
    m = logits[0].max(axis=-1, keepdims=True)
    for s in logits[1:]:
        m = jnp.maximum(m, s.max(axis=-1, keepdims=True))
    if extra is not None:
        m = jnp.maximum(m, extra)
    den = None
    out = None
    for s, v in zip(logits, values):
        p = jnp.exp(s - m)
        d = p.sum(axis=-1, keepdims=True)
        o = _dot(p.astype(bf16), v)
        den = d if den is None else den + d
        out = o if out is None else out + o
    if extra is not None:
        den = den + jnp.exp(extra - m)
    return out / den


NAT_QROWS = 8
NAT_BAND = 16
NAT_TQ = NAT_QROWS * GRID_W
NAT_TK = NAT_BAND * GRID_W
NQ_COL, NK_COL, NV_COL = 2, 3, 4


def _nat_kernel(q_ref, k_ref, v_ref, kc_ref, vc_ref, bias_ref, o_ref, *, nsteps):
    j = pl.program_id(1)
    start = jnp.clip(j * NAT_QROWS - NAT_KH // 2, 0, nsteps * NAT_QROWS - NAT_BAND) * GRID_W
    start = pl.multiple_of(start, GRID_W * 4)
    low = _lane_is_low((NAT_TQ, LANES))
    for c in range(NAT_HEADS // 2):
        cols = slice(c * LANES, (c + 1) * LANES)
        q2 = q_ref[0, :, cols] * 0.125
        k2 = k_ref[0, pl.ds(start, NAT_TK), cols]
        v2 = v_ref[0, pl.ds(start, NAT_TK), cols]
        kc2 = kc_ref[0, :, cols]
        vc2 = vc_ref[0, :, cols]
        halves = []
        for e in range(2):
            qm = jnp.where(low if e == 0 else jnp.logical_not(low), q2, jnp.zeros_like(q2))
            s = _dot_nt(qm, k2) + bias_ref[0, 2 * c + e]
            sc = _dot_nt(qm, kc2)
            halves.append(_softmax_pv([s, sc], [v2, vc2]))
        o_ref[0, :, cols] = jnp.where(low, halves[0], halves[1]).astype(o_ref.dtype)


def _nat_bias_tables(rpb):
    rows = 64
    nsteps = rows // NAT_QROWS
    a = np.arange(NAT_QROWS)[:, None]
    m = np.arange(NAT_BAND)[None, :]
    sel_r = np.zeros((3, NAT_QROWS, NAT_BAND, 2 * NAT_KH - 1), np.float32)
    ok_r = np.zeros((3, NAT_QROWS, NAT_BAND), bool)
    for v, jj in enumerate((1, 0, nsteps - 1)):
        band0 = int(np.clip(jj * NAT_QROWS - NAT_KH // 2, 0, rows - NAT_BAND))
        qr = jj * NAT_QROWS + a
        kr = band0 + m
        rstart = np.clip(qr - NAT_KH // 2, 0, rows - NAT_KH)
        ok = (kr >= rstart) & (kr < rstart + NAT_KH)
        dr = np.clip(kr - qr + NAT_KH - 1, 0, 2 * NAT_KH - 2)
        ok_r[v] = ok
        sel_r[v] = np.eye(2 * NAT_KH - 1, dtype=np.float32)[dr] * ok[..., None]
    qc = np.arange(GRID_W)[:, None]
    kc = np.arange(GRID_W)[None, :]
    cstart = np.clip(qc - NAT_KW // 2, 0, GRID_W - NAT_KW)
    ok_c = (kc >= cstart) & (kc < cstart + NAT_KW)
    dc = np.clip(kc - qc + NAT_KW - 1, 0, 2 * NAT_KW - 2)
    sel_c = np.eye(2 * NAT_KW - 1, dtype=np.float32)[dc] * ok_c[..., None]
    valid = ok_r[:, :, None, :, None] & ok_c[None, None, :, None, :]
    valid = valid.reshape(3, 1, NAT_TQ, NAT_TK)
    t = jnp.einsum('vamr,hrc,qkc->vhaqmk', jnp.asarray(sel_r), rpb.astype(f32), jnp.asarray(sel_c),
                   precision=lax.Precision.HIGHEST)
    t = t.reshape(3, NAT_HEADS, NAT_TQ, NAT_TK)
    return jnp.where(jnp.asarray(valid), t, NEG_INF)


def _nat_call(p, pc, bias):
    B, T, _ = p.shape
    L = pc.shape[1]
    W = NAT_HEADS * HEAD_DIM
    nsteps = T // NAT_TQ
    kern = functools.partial(_nat_kernel, nsteps=nsteps)
    var = lambda j: jnp.where(j == 0, 1, jnp.where(j == nsteps - 1, 2, 0))
    return pl.pallas_call(
        kern,
        grid=(B, nsteps),
        in_specs=[pl.BlockSpec((1, NAT_TQ, W), lambda b, j: (b, j, NQ_COL)),
                  pl.BlockSpec((1, T, W), lambda b, j: (b, 0, NK_COL)),
                  pl.BlockSpec((1, T, W), lambda b, j: (b, 0, NV_COL)),
                  pl.BlockSpec((1, L, W), lambda b, j: (b, 0, NK_COL)),
                  pl.BlockSpec((1, L, W), lambda b, j: (b, 0, NV_COL)),
                  pl.BlockSpec((1, NAT_HEADS, NAT_TQ, NAT_TK), lambda b, j: (var(j), 0, 0, 0))],
        out_specs=pl.BlockSpec((1, NAT_TQ, W), lambda b, j: (b, j, 0)),
        out_shape=jax.ShapeDtypeStruct((B, T, W), bf16),
        compiler_params=_cparams(("parallel", "arbitrary")),
        name="nat_attention",
    )(p, p, p, pc, pc, bias)


GQ_COL = 1536 // 512
GK_COL = 2048 // LANES
GV_COL = 2176 // LANES
GQA_R = GQA_Q_HEADS // GQA_KV_HEADS


def _rope(x, cos, sin):
    lane = lax.broadcasted_iota(jnp.int32, x.shape, 1)
    first = (lane % (HEAD_DIM // 2)) < (HEAD_DIM // 4)
    swapped = jnp.where(first, pltpu.roll(x, LANES - HEAD_DIM // 4, 1), pltpu.roll(x, HEAD_DIM // 4, 1))
    return x * cos + swapped * sin


def _rope_tables(S):
    t = np.arange(S)
    pos = np.stack([t // GRID_W, t % GRID_W], axis=1).astype(np.float64)
    quarter = HEAD_DIM // 4
    freqs = ROPE_BASE ** (-np.arange(quarter, dtype=np.float64) / quarter)
    lane = np.arange(LANES)
    which = (lane % HEAD_DIM) // (HEAD_DIM // 2)
    ang = pos[:, which] * freqs[lane % quarter][None, :]
    sign = np.where((lane % (HEAD_DIM // 2)) < quarter, -1.0, 1.0)[None, :]
    return jnp.asarray(np.cos(ang), f32), jnp.asarray(np.sin(ang) * sign, f32)


def _gqa_kernel(q_ref, kp_ref, kc_ref, kn_ref, vp_ref, vc_ref, vn_ref, kx_ref, vx_ref,
                cp_ref, cc_ref, cn_ref, sp_ref, sc_ref, sn_ref, sink_ref, o_ref, *, nb):
    n = pl.program_id(1)
    blk = GQA_BLOCK
    rows = GQA_R * blk
    cos_q, sin_q = cc_ref[...], sc_ref[...]
    qs = [(_rope(q_ref[0, :, r * LANES:(r + 1) * LANES].astype(f32), cos_q, sin_q) * 0.125) for r in range(GQA_R)]
    q_all = jnp.concatenate(qs, axis=0)
    low = _lane_is_low((rows, LANES))
    kp = _rope(kp_ref[0].astype(f32), cp_ref[...], sp_ref[...]).astype(bf16)
    kc = _rope(kc_ref[0].astype(f32), cos_q, sin_q).astype(bf16)
    kn = _rope(kn_ref[0].astype(f32), cn_ref[...], sn_ref[...]).astype(bf16)
    qi = lax.broadcasted_iota(jnp.int32, (rows, blk), 0) % blk
    kj = lax.broadcasted_iota(jnp.int32, (rows, blk), 1)
    bias_p = jnp.where((kj >= qi) & (n > 0), 0.0, NEG_INF)
    bias_n = jnp.where((kj <= qi) & (n < nb - 1), 0.0, NEG_INF)
    halves = []
    for g in range(GQA_KV_HEADS):
        qm = jnp.where(low if g == 0 else jnp.logical_not(low), q_all, 0.0).astype(bf16)
        logits = [_dot_nt(qm, kp) + bias_p, _dot_nt(qm, kc), _dot_nt(qm, kn) + bias_n, _dot_nt(qm, kx_ref[0])]
        vals = [vp_ref[0], vc_ref[0], vn_ref[0], vx_ref[0]]
        halves.append(_softmax_pv(logits, vals, extra=sink_ref[g][:, 0:1]))
    out = jnp.where(low, halves[0], halves[1])
    for r in range(GQA_R):
        o_ref[0, :, r * LANES:(r + 1) * LANES] = out[r * blk:(r + 1) * blk].astype(o_ref.dtype)


def _sink_cols(sink, blk):
    s = sink.astype(f32).reshape(GQA_KV_HEADS, GQA_R, 1, 1)
    return jnp.broadcast_to(s, (GQA_KV_HEADS, GQA_R, blk, LANES)).reshape(GQA_KV_HEADS, GQA_R * blk, LANES)


def _gqa_call(p, pc, cos, sin, sink):
    B, T, _ = p.shape
    L = pc.shape[1]
    blk = GQA_BLOCK
    nb = T // blk
    QW = GQA_Q_HEADS * HEAD_DIM
    kern = functools.partial(_gqa_kernel, nb=nb)
    prev = lambda n: jnp.maximum(n - 1, 0)
    nxt = lambda n: jnp.minimum(n + 1, nb - 1)
    kv = lambda col, f: pl.BlockSpec((1, blk, LANES), lambda b, n: (b, f(n), col))
    tab = lambda f: pl.BlockSpec((blk, LANES), lambda b, n: (f(n), 0))
    same = lambda n: n
    return pl.pallas_call(
        kern,
        grid=(B, nb),
        in_specs=[pl.BlockSpec((1, blk, QW), lambda b, n: (b, n, GQ_COL)),
                  kv(GK_COL, prev), kv(GK_COL, same), kv(GK_COL, nxt),
                  kv(GV_COL, prev), kv(GV_COL, same), kv(GV_COL, nxt),
                  pl.BlockSpec((1, L, LANES), lambda b, n: (b, 0, GK_COL)),
                  pl.BlockSpec((1, L, LANES), lambda b, n: (b, 0, GV_COL)),
                  tab(prev), tab(same), tab(nxt), tab(prev), tab(same), tab(nxt),
                  pl.BlockSpec((GQA_KV_HEADS, GQA_R * blk, LANES), lambda b, n: (0, 0, 0))],
        out_specs=pl.BlockSpec((1, blk, QW), lambda b, n: (b, n, 0)),
        out_shape=jax.ShapeDtypeStruct((B, T, QW), bf16),
        compiler_params=_cparams(("parallel", "parallel")),
        name="window_gqa",
    )(p, p, p, p, p, p, p, pc, pc, cos, cos, cos, sin, sin, sin, _sink_cols(sink, blk))


def _ctx_attn_kernel(nq_ref, nk_ref, nv_ref, gq_ref, gk_ref, gv_ref, sink_ref, ob_ref, od_ref, *, L):
    low = _lane_is_low((L, LANES))
    for c in range(NAT_HEADS // 2):
        cols = slice(c * LANES, (c + 1) * LANES)
        q2 = nq_ref[0, :, cols] * 0.125
        halves = []
        for e in range(2):
            qm = jnp.where(low if e == 0 else jnp.logical_not(low), q2, jnp.zeros_like(q2))
            halves.append(_softmax_pv([_dot_nt(qm, nk_ref[0, :, cols])], [nv_ref[0, :, cols]]))
        ob_ref[0, :, cols] = jnp.where(low, halves[0], halves[1]).astype(ob_ref.dtype)
    q_all = jnp.concatenate([gq_ref[0, :, r * LANES:(r + 1) * LANES] for r in range(GQA_R)], axis=0) * 0.125
    low4 = _lane_is_low((GQA_R * L, LANES))
    halves = []
    for g in range(GQA_KV_HEADS):
        qm = jnp.where(low4 if g == 0 else jnp.logical_not(low4), q_all, jnp.zeros_like(q_all))
        halves.append(_softmax_pv([_dot_nt(qm, gk_ref[0])], [gv_ref[0]], extra=sink_ref[g][:, 0:1]))
    out = jnp.where(low4, halves[0], halves[1])
    for r in range(GQA_R):
        od_ref[0, :, r * LANES:(r + 1) * LANES] = out[r * L:(r + 1) * L].astype(od_ref.dtype)


def _ctx_attn_call(pc, sink):
    B, L, _ = pc.shape
    W = NAT_HEADS * HEAD_DIM
    QW = GQA_Q_HEADS * HEAD_DIM
    kern = functools.partial(_ctx_attn_kernel, L=L)
    return pl.pallas_call(
        kern,
        grid=(B,),
        in_specs=[pl.BlockSpec((1, L, W), lambda b: (b, 0, NQ_COL)),
                  pl.BlockSpec((1, L, W), lambda b: (b, 0, NK_COL)),
                  pl.BlockSpec((1, L, W), lambda b: (b, 0, NV_COL)),
                  pl.BlockSpec((1, L, QW), lambda b: (b, 0, GQ_COL)),
                  pl.BlockSpec((1, L, LANES), lambda b: (b, 0, GK_COL)),
                  pl.BlockSpec((1, L, LANES), lambda b: (b, 0, GV_COL)),
                  pl.BlockSpec((GQA_KV_HEADS, GQA_R * L, LANES), lambda b: (0, 0, 0))],
        out_specs=[pl.BlockSpec((1, L, W), lambda b: (b, 0, 0)),
                   pl.BlockSpec((1, L, QW), lambda b: (b, 0, 0))],
        out_shape=[jax.ShapeDtypeStruct((B, L, W), bf16), jax.ShapeDtypeStruct((B, L, QW), bf16)],
        compiler_params=_cparams(("parallel",)),
        name="context_attention",
    )(pc, pc, pc, pc, pc, pc, _sink_cols(sink, L))


def _merge_kernel(x_ref, sh_ref, sc_ref, g1_ref, ya_ref, yb_ref, yc_ref, yd_ref, wg_ref, bg_ref,
                  wa_ref, wb_ref, wc_ref, wd_ref, wo_ref, lg_ref, lb_ref, o_ref):
    D = D_MODEL
    x = x_ref[0]
    xm = (_ln(x) * (1.0 + sc_ref[0]) + sh_ref[0]).astype(bf16)
    m = None
    for j, (y_ref, w_ref) in enumerate(((ya_ref, wa_ref), (yb_ref, wb_ref), (yc_ref, wc_ref), (yd_ref, wd_ref))):
        gate = jax.nn.sigmoid(_dot(xm, wg_ref[:, j * D:(j + 1) * D]) + bg_ref[:, j * D:(j + 1) * D])
        t = gate * _dot(y_ref[0], w_ref[...])
        m = t if m is None else m + t
    mix = _dot(m.astype(bf16), wo_ref[...])
    z = DEEPNORM_ALPHA * x + g1_ref[0] * mix
    o_ref[0] = _ln(z) * lg_ref[...] + lb_ref[...]


def _merge_call(x, sh, sc, g1, ya, yb, yc, yd, wg, bg, wa, wb, wc, wd, wo, lg, lb, tm):
    B, T, D = x.shape
    tok = lambda w: pl.BlockSpec((1, tm, w), lambda b, i: (b, i, 0))
    mod = pl.BlockSpec((1, 1, D), lambda b, i: (b, 0, 0))
    full = lambda a: pl.BlockSpec(a.shape, lambda b, i: (0,) * a.ndim)
    return pl.pallas_call(
        _merge_kernel,
        grid=(B, T // tm),
        in_specs=[tok(D), mod, mod, mod, tok(ya.shape[-1]), tok(yb.shape[-1]), tok(yc.shape[-1]), tok(yd.shape[-1]),
                  full(wg), full(bg), full(wa), full(wb), full(wc), full(wd), full(wo), full(lg), full(lb)],
        out_specs=tok(D),
        out_shape=jax.ShapeDtypeStruct((B, T, D), f32),
        compiler_params=_cparams(("parallel", "parallel")),
        name="merge_ln",
    )(x, sh, sc, g1, ya, yb, yc, yd, wg, bg, wa, wb, wc, wd, wo, lg, lb)


PEER_TM = 512
PEER_EBLK = 4
PEER_NTOP = PEER_TOPK + 1
BIG_NEG = -3.0e38
SQRT_HALF = 0.7071067811865476


def _top_values(s, count):
    vals = []
    cur = s
    for r in range(count):
        m = jnp.max(cur, axis=0, keepdims=True)
        vals.append(m)
        if r + 1 < count:
            cur = jnp.where(cur == m, BIG_NEG, cur)
    return vals


def _peer_kernel(x_ref, sh_ref, sc_ref, g2_ref, wq_ref, k1_ref, k2_ref, eu_ref, ev_ref, lg_ref, lb_ref, o_ref,
                 ut_ref, s1_ref, e1_ref, s2_ref, e2_ref, tau_ref, acc_ref, *, tm, nsteps):
    step = pl.program_id(2)
    nk = PEER_N_KEYS

    @pl.when(step == 0)
    def _select():
        u = _ln(x_ref[0]) * (1.0 + sc_ref[0]) + sh_ref[0]
        ut_ref[...] = u.T.astype(bf16)
        acc_ref[...] = jnp.zeros_like(acc_ref)
        pairs = [(i, j) for i in range(PEER_NTOP) for j in range(PEER_NTOP) if (i + 1) * (j + 1) <= PEER_NTOP]
        pad = (-len(pairs)) % 8

        def head(h, carry):
            r0 = pl.multiple_of(h * 2 * nk, 2 * nk)
            q1 = _dot(wq_ref[pl.ds(r0, nk), :], ut_ref[...]).astype(bf16)
            q2 = _dot(wq_ref[pl.ds(r0 + nk, nk), :], ut_ref[...]).astype(bf16)
            s1 = _dot(k1_ref[h], q1)
            s2 = _dot(k2_ref[h], q2)
            for t in range(tm // LANES):
                ln = slice(t * LANES, (t + 1) * LANES)
                s1t, s2t = s1[:, ln], s2[:, ln]
                a = _top_values(s1t, PEER_NTOP)
                b = _top_values(s2t, PEER_NTOP)
                cand = jnp.concatenate([a[i] + b[j] for i, j in pairs]
                                       + [jnp.full((pad, LANES), BIG_NEG, f32)], axis=0)
                top = _top_values(cand, PEER_NTOP)
                tau = 0.5 * (top[PEER_TOPK - 1] + top[PEER_TOPK])
                z = jnp.sum(jnp.where(cand > tau, jnp.exp(cand - (a[0] + b[0])), 0.0), axis=0, keepdims=True)
                s1_ref[h, :, ln] = s1t
                s2_ref[h, :, ln] = s2t
                e1_ref[h, :, ln] = jnp.exp(s1t - a[0]) / z
                e2_ref[h, :, ln] = jnp.exp(s2t - b[0])
                tau_ref[h, :, ln] = tau
            return carry

        lax.fori_loop(0, PEER_HEADS, head, 0)

    gs = []
    for blk in range(PEER_EBLK):
        i1 = step * PEER_EBLK + blk
        ht = _dot(eu_ref[blk * nk:(blk + 1) * nk, :], ut_ref[...])
        w = jnp.zeros((nk, tm), f32)
        for h in range(PEER_HEADS):
            thr = tau_ref[h] -s1_ref[h, pl.ds(i1, 1), :]
            e1 = e1_ref[h, pl.ds(i1, 1), :]
            w = w + jnp.where(s2_ref[h] > thr, e2_ref[h] * e1, 0.0)
        gelu = 0.5 * ht * (1.0 + lax.erf(ht * SQRT_HALF))
        gs.append((w * gelu).astype(bf16))
    acc_ref[...] += _dot(ev_ref[...], jnp.concatenate(gs, axis=0))

    @pl.when(step == nsteps - 1)
    def _finish():
        z = DEEPNORM_ALPHA * x_ref[0] + g2_ref[0] * acc_ref[...].T
        o_ref[0] = _ln(z) * lg_ref[...] + lb_ref[...]


def _peer_call(x, sh, sc, g2, wq_t, k1, k2, eu, ev_t, lg, lb):
    B, T, D = x.shape
    tm = PEER_TM
    nk = PEER_N_KEYS
    ne = eu.shape[0]
    eb = PEER_EBLK * nk
    nsteps = ne // eb
    kern = functools.partial(_peer_kernel, tm=tm, nsteps=nsteps)
    mod = pl.BlockSpec((1, 1, D), lambda b, i, s: (b, 0, 0))
    full = lambda a: pl.BlockSpec(a.shape, lambda b, i, s: (0,) * a.ndim)
    sel = pltpu.VMEM((PEER_HEADS, nk, tm), f32)
    return pl.pallas_call(
        kern,
        grid=(B, T // tm, nsteps),
        in_specs=[pl.BlockSpec((1, tm, D), lambda b, i, s: (b, i, 0)), mod, mod, mod,
                  full(wq_t), full(k1), full(k2),
                  pl.BlockSpec((eb, D), lambda b, i, s: (s, 0)),
                  pl.BlockSpec((D, eb), lambda b, i, s: (0, s)),
                  full(lg), full(lb)],
        out_specs=pl.BlockSpec((1, tm, D), lambda b, i, s: (b, i, 0)),
        out_shape=jax.ShapeDtypeStruct((B, T, D), f32),
        scratch_shapes=[pltpu.VMEM((D, tm), bf16), sel, sel, sel, sel,
                        pltpu.VMEM((PEER_HEADS, 1, tm), f32), pltpu.VMEM((D, tm), f32)],
        compiler_params=_cparams(("parallel", "parallel", "arbitrary")),
        name="peer_ffn",
    )(x, sh, sc, g2, wq_t, k1, k2, eu, ev_t, lg, lb)


def _gqa_head_perm():
    cols = []
    for c in range(GQA_R):
        cols += list(range(c * HEAD_DIM, (c + 1) * HEAD_DIM))
        cols += list(range((GQA_R + c) * HEAD_DIM, (GQA_R + c + 1) * HEAD_DIM))
    return np.asarray(cols)


def kernel(x, c, ctx, c_ctx, w_ada, b_ada, w_in, b_in, conv_w, conv_b, conv_ln_g, conv_ln_b, nat_rpb, gqa_sink,
           w_branch_a, w_branch_b, w_branch_c, w_branch_d, w_out, ln1_g, ln1_b, peer_wq, peer_k1, peer_k2,
           peer_u, peer_v, ln2_g, ln2_b):
    B, S, D = x.shape
    L = ctx.shape[1]
    gq0, gq1 = 1536, 2048
    perm = _gqa_head_perm()

    cvec = jnp.concatenate([c, c_ctx[None, :], jnp.zeros((8 - B - 1, D), f32)], axis=0)
    mod = _ada_call(cvec, w_ada, b_ada)
    cos, sin = _rope_tables(S)
    cs_lat = _dft_time_mats(S)
    cs_ctx = _dft_time_mats(L)
    bd = _dft_chan_mats()
    row = lambda v: v.reshape(1, -1)

    xc = ctx.reshape(1, B * L, D)
    for i in range(DEPTH):
        lat = [mod[i, :B, k * D:(k + 1) * D].reshape(B, 1, D) for k in range(6)]
        con = [mod[i, B:B + 1, k * D:(k + 1) * D].reshape(1, 1, D) for k in range(6)]
        wi, bi = w_in[i], b_in[i]
        w_small = jnp.concatenate([wi[:, :gq0], wi[:, gq0:gq1][:, perm], wi[:, gq1:N_SMALL]], axis=1).astype(bf16)
        b_small = row(jnp.concatenate([bi[:gq0], bi[gq0:gq1][perm], bi[gq1:N_SMALL]]))
        w_gate = wi[:, N_SMALL:].astype(bf16)
        b_gate = row(bi[N_SMALL:])
        wa, wb, wc = w_branch_a[i].astype(bf16), w_branch_b[i].astype(bf16), w_branch_c[i].astype(bf16)
        wd = w_branch_d[i][perm].astype(bf16)
        wo = w_out[i].astype(bf16)
        merge_w = (w_gate, b_gate, wa, wb, wc, wd, wo, row(ln1_g[i]), row(ln1_b[i]))
        peer_w = (peer_wq[i].T.astype(bf16), peer_k1[i].astype(bf16), peer_k2[i].astype(bf16),
                  peer_u[i].astype(bf16), peer_v[i].T.astype(bf16), row(ln2_g[i]), row(ln2_b[i]))
        conv_p = (conv_w[i], conv_b[i], conv_ln_g[i], conv_ln_b[i])

        pc = _inproj_call(xc, con[0], con[1], w_small, b_small, 512).reshape(B, L, N_SMALL)
        p = _inproj_call(x, lat[0], lat[1], w_small, b_small, 512)
        y_a = _conv_call(p, *conv_p)
        y_b = _nat_call(p, pc, _nat_bias_tables(nat_rpb[i]))
        y_c = _fft_call(p, cs_lat, bd)
        y_d = _gqa_call(p, pc, cos, sin, gqa_sink[i])
        x = _merge_call(x, lat[0], lat[1], lat[2], y_a, y_b, y_c, y_d, *merge_w, 256)
        x = _peer_call(x, lat[3], lat[4], lat[5], *peer_w)

        if i < DEPTH - 1:
            yc_a = _conv_call(pc, *conv_p)
            yc_b, yc_d = _ctx_attn_call(pc, gqa_sink[i])
            yc_c = _fft_call(pc, cs_ctx, bd)
            flat = lambda a: a.reshape(1, B * L, a.shape[-1])
            xc = _merge_call(xc, con[0], con[1], con[2], flat(yc_a), flat(yc_b), flat(yc_c), flat(yc_d),
                             *merge_w, 256)
            xc = _peer_call(xc, con[3], con[4], con[5], *peer_w)
    return x
```

```python
import functools
import math

import numpy as np
import jax
import jax.numpy as jnp
from jax import lax
from jax.experimental import pallas as pl
from jax.experimental.pallas import tpu as pltpu

f32 = jnp.float32
bf16 = jnp.bfloat16

D_MODEL = 1024
DEPTH = 2
GRID_W = 64
CONV_CH = 256
CONV_WIDTH = 31
NAT_HEADS = 4
NAT_KH = 8
NAT_KW = 16
GQA_Q_HEADS = 8
GQA_KV_HEADS = 2
GQA_BLOCK = 128
HEAD_DIM = 64
ROPE_BASE = 10000.0
PEER_HEADS = 8
PEER_N_KEYS = 128
PEER_TOPK = 16
LN_EPS = 1e-5
NEG_INF = -1e30
DEEPNORM_ALPHA = (2 * DEPTH) ** 0.25

LANES = 128
N_SMALL = 2304
VMEM_LIMIT = 56 * 1024 * 1024


def _cparams(sem, vmem=VMEM_LIMIT):
    return pltpu.CompilerParams(dimension_semantics=sem, vmem_limit_bytes=vmem)


def _ln(x):
    mu = jnp.mean(x, axis=-1, keepdims=True)
    xc = x - mu
    var = jnp.mean(xc * xc, axis=-1, keepdims=True)
    return xc * lax.rsqrt(var + LN_EPS)


def _dot(a, b):
    return jnp.dot(a, b, preferred_element_type=f32)


def _dot_nt(a, b):
    return lax.dot_general(a, b, (((1,), (1,)), ((), ())), preferred_element_type=f32)


def _ada_kernel(c_ref, w_ref, b_ref, o_ref):
    c = c_ref[...]
    h = c * jax.nn.sigmoid(c)
    o_ref[0] = jnp.dot(h, w_ref[0], preferred_element_type=f32,
                       precision=lax.Precision.HIGHEST) + b_ref[0]


def _ada_call(cvec, w_ada, b_ada):
    L, D, N = w_ada.shape
    tn = 1536
    return pl.pallas_call(
        _ada_kernel,
        grid=(L, N // tn),
        in_specs=[pl.BlockSpec((8, D), lambda l, j: (0, 0)),
                  pl.BlockSpec((1, D, tn), lambda l, j: (l, 0, j)),
                  pl.BlockSpec((1, 1, tn), lambda l, j: (l, 0, j))],
        out_specs=pl.BlockSpec((1, 8, tn), lambda l, j: (l, 0, j)),
        out_shape=jax.ShapeDtypeStruct((L, 8, N), f32),
        compiler_params=_cparams(("parallel", "parallel")),
        name="ada_mod",
    )(cvec, w_ada, b_ada.reshape(L, 1, N))


def _inproj_kernel(x_ref, sh_ref, sc_ref, w_ref, b_ref, o_ref):
    xm = _ln(x_ref[0]) * (1.0 + sc_ref[0]) + sh_ref[0]
    o_ref[0] = (_dot(xm.astype(bf16), w_ref[...]) + b_ref[...]).astype(o_ref.dtype)


def _inproj_call(x, sh, sc, w, b, tm):
    B, T, D = x.shape
    N = w.shape[1]
    return pl.pallas_call(
        _inproj_kernel,
        grid=(B, T // tm),
        in_specs=[pl.BlockSpec((1, tm, D), lambda b, i: (b, i, 0)),
                  pl.BlockSpec((1, 1, D), lambda b, i: (b, 0, 0)),
                  pl.BlockSpec((1, 1, D), lambda b, i: (b, 0, 0)),
                  pl.BlockSpec((D, N), lambda b, i: (0, 0)),
                  pl.BlockSpec((1, N), lambda b, i: (0, 0))],
        out_specs=pl.BlockSpec((1, tm, N), lambda b, i: (b, i, 0)),
        out_shape=jax.ShapeDtypeStruct((B, T, N), bf16),
        compiler_params=_cparams(("parallel", "parallel")),
        name="in_proj",
    )(x, sh, sc, w, b)


CONV_HALO = 16
CONV_SUB = 128


def _conv_kernel(prev_ref, cur_ref, next_ref, w_ref, cb_ref, g_ref, b_ref, o_ref, hs_ref, *, tc, nchunks):
    i = pl.program_id(1)

    def glu(v):
        v = v.astype(f32)
        return v[:, :CONV_CH] * jax.nn.sigmoid(v[:, CONV_CH:])

    hs_ref[0:CONV_HALO, :] = jnp.where(i > 0, glu(prev_ref[0]), 0.0)
    hs_ref[CONV_HALO:CONV_HALO + tc, :] = glu(cur_ref[0])
    hs_ref[CONV_HALO + tc:2 * CONV_HALO + tc, :] = jnp.where(i < nchunks - 1, glu(next_ref[0]), 0.0)
    base = CONV_HALO - CONV_WIDTH // 2
    for s in range(tc // CONV_SUB):
        acc = jnp.zeros((CONV_SUB, CONV_CH), f32)
        for j in range(CONV_WIDTH):
            acc = acc + hs_ref[pl.ds(s * CONV_SUB + base + j, CONV_SUB), :] * w_ref[j:j + 1, :]
        y = _ln(acc + cb_ref[...]) * g_ref[...] + b_ref[...]
        y = y * jax.nn.sigmoid(y)
        o_ref[0, s * CONV_SUB:(s + 1) * CONV_SUB, :] = y.astype(o_ref.dtype)


def _conv_call(p, conv_w, conv_b, ln_g, ln_b):
    B, T, _ = p.shape
    tc = min(512, T)
    nchunks = T // tc
    hb = tc // CONV_HALO
    nhb = T // CONV_HALO
    width = 2 * CONV_CH
    kern = functools.partial(_conv_kernel, tc=tc, nchunks=nchunks)
    vec = lambda v: v.reshape(1, CONV_CH)
    return pl.pallas_call(
        kern,
        grid=(B, nchunks),
        in_specs=[pl.BlockSpec((1, CONV_HALO, width), lambda b, i: (b, jnp.maximum(i * hb - 1, 0), 0)),
                  pl.BlockSpec((1, tc, width), lambda b, i: (b, i, 0)),
                  pl.BlockSpec((1, CONV_HALO, width), lambda b, i: (b, jnp.minimum((i + 1) * hb, nhb - 1), 0)),
                  pl.BlockSpec((CONV_WIDTH, CONV_CH), lambda b, i: (0, 0)),
                  pl.BlockSpec((1, CONV_CH), lambda b, i: (0, 0)),
                  pl.BlockSpec((1, CONV_CH), lambda b, i: (0, 0)),
                  pl.BlockSpec((1, CONV_CH), lambda b, i: (0, 0))],
        out_specs=pl.BlockSpec((1, tc, CONV_CH), lambda b, i: (b, i, 0)),
        out_shape=jax.ShapeDtypeStruct((B, T, CONV_CH), bf16),
        scratch_shapes=[pltpu.VMEM((tc + 2 * CONV_HALO, CONV_CH), f32)],
        compiler_params=_cparams(("parallel", "parallel")),
        name="conv_branch",
    )(p, p, p, conv_w, vec(conv_b), vec(ln_g), vec(ln_b))


FNET_W = 256
FNET_GROUP_DIM = 64
F_COL = 1280 // FNET_W


def _fft_kernel(f_ref, cs_ref, bd_ref, o_ref, rhs_ref, *, T, scale):
    i = pl.program_id(0)
    b = pl.program_id(1)

    @pl.when(i == 0)
    def _():
        rows = min(512, T)
        for r in range(T // rows):
            z = _dot(f_ref[0, r * rows:(r + 1) * rows, :], bd_ref[...])
            rhs_ref[b, r * rows:(r + 1) * rows, :] = z[:, :FNET_W].astype(bf16)
            rhs_ref[b, T + r * rows:T + (r + 1) * rows, :] = (-z[:, FNET_W:]).astype(bf16)

    o_ref[0] = (_dot(cs_ref[...], rhs_ref[b]) * scale).astype(o_ref.dtype)


def _dft_time_mats(T):
    t = jnp.arange(T, dtype=jnp.int32)
    ang = ((t[:, None] * t[None, :]) % T).astype(f32) * f32(2.0 * math.pi / T)
    return jnp.concatenate([jnp.cos(ang), jnp.sin(ang)], axis=1).astype(bf16)


def _dft_chan_mats():
    n = FNET_GROUP_DIM
    k = np.arange(n)
    ang = 2.0 * np.pi * ((k[:, None] * k[None, :]) % n) / n
    eye = np.eye(FNET_W // n)
    return jnp.asarray(np.concatenate([np.kron(eye, np.cos(ang)), np.kron(eye, np.sin(ang))], axis=1), f32).astype(bf16)


def _fft_call(p, cs, bd):
    B, T, _ = p.shape
    tm = min(512, T)
    kern = functools.partial(_fft_kernel, T=T, scale=1.0 / math.sqrt(T * FNET_GROUP_DIM))
    return pl.pallas_call(
        kern,
        grid=(T // tm, B),
        in_specs=[pl.BlockSpec((1, T, FNET_W), lambda i, b: (b, 0, F_COL)),
                  pl.BlockSpec((tm, 2 * T), lambda i, b: (i, 0)),
                  pl.BlockSpec((FNET_W, 2 * FNET_W), lambda i, b: (0, 0))],
        out_specs=pl.BlockSpec((1, tm, FNET_W), lambda i, b: (b, i, 0)),
        out_shape=jax.ShapeDtypeStruct((B, T, FNET_W), bf16),
        scratch_shapes=[pltpu.VMEM((B, 2 * T, FNET_W), bf16)],
        compiler_params=_cparams(("arbitrary", "arbitrary")),
        name="fourier_branch",
    )(p, cs, bd)


def _lane_is_low(shape):
    return lax.broadcasted_iota(jnp.int32, shape, len(shape) - 1) < HEAD_DIM


def _softmax_pv(logits, values, extra=None):
    m = logits[0].max(axis=-1, keepdims=True)
    for s in logits[1:]:
        m = jnp.maximum(m, s.max(axis=-1, keepdims=True))
    if extra is not None:
        m = jnp.maximum(m, extra)
    den = None
    out = None
    for s, v in zip(logits, values):
        p = jnp.exp(s - m)
        d = p.sum(axis=-1, keepdims=True)
        o = _dot(p.astype(bf16), v)
        den = d if den is None else den + d
        out = o if out is None else out + o
    if extra is not None:
        den = den + jnp.exp(extra - m)
    return out / den


NAT_QROWS = 8
NAT_BAND = 16
NAT_TQ = NAT_QROWS * GRID_W
NAT_TK = NAT_BAND * GRID_W
NQ_COL, NK_COL, NV_COL = 2, 3, 4


def _nat_kernel(q_ref, k_ref, v_ref, kc_ref, vc_ref, bias_ref, o_ref, *, nsteps):
    j = pl.program_id(1)
    start = jnp.clip(j * NAT_QROWS - NAT_KH // 2, 0, nsteps * NAT_QROWS - NAT_BAND) * GRID_W
    start = pl.multiple_of(start, GRID_W * 4)
    low = _lane_is_low((NAT_TQ, LANES))
    for c in range(NAT_HEADS // 2):
        cols = slice(c * LANES, (c + 1) * LANES)
        q2 = q_ref[0, :, cols] * 0.125
        k2 = k_ref[0, pl.ds(start, NAT_TK), cols]
        v2 = v_ref[0, pl.ds(start, NAT_TK), cols]
        kc2 = kc_ref[0, :, cols]
        vc2 = vc_ref[0, :, cols]
        halves = []
        for e in range(2):
            qm = jnp.where(low if e == 0 else jnp.logical_not(low), q2, jnp.zeros_like(q2))
            s = _dot_nt(qm, k2) + bias_ref[0, 2 * c + e]
            sc = _dot_nt(qm, kc2)
            halves.append(_softmax_pv([s, sc], [v2, vc2]))
        o_ref[0, :, cols] = jnp.where(low, halves[0], halves[1]).astype(o_ref.dtype)


def _nat_bias_tables(rpb):
    rows = 64
    nsteps = rows // NAT_QROWS
    a = np.arange(NAT_QROWS)[:, None]
    m = np.arange(NAT_BAND)[None, :]
    sel_r = np.zeros((3, NAT_QROWS, NAT_BAND, 2 * NAT_KH - 1), np.float32)
    ok_r = np.zeros((3, NAT_QROWS, NAT_BAND), bool)
    for v, jj in enumerate((1, 0, nsteps - 1)):
        band0 = int(np.clip(jj * NAT_QROWS - NAT_KH // 2, 0, rows - NAT_BAND))
        qr = jj * NAT_QROWS + a
        kr = band0 + m
        rstart = np.clip(qr - NAT_KH // 2, 0, rows - NAT_KH)
        ok = (kr >= rstart) & (kr < rstart + NAT_KH)
        dr = np.clip(kr - qr + NAT_KH - 1, 0, 2 * NAT_KH - 2)
        ok_r[v] = ok
        sel_r[v] = np.eye(2 * NAT_KH - 1, dtype=np.float32)[dr] * ok[..., None]
    qc = np.arange(GRID_W)[:, None]
    kc = np.arange(GRID_W)[None, :]
    cstart = np.clip(qc - NAT_KW // 2, 0, GRID_W - NAT_KW)
    ok_c = (kc >= cstart) & (kc < cstart + NAT_KW)
    dc = np.clip(kc - qc + NAT_KW - 1, 0, 2 * NAT_KW - 2)
    sel_c = np.eye(2 * NAT_KW - 1, dtype=np.float32)[dc] * ok_c[..., None]
    valid = ok_r[:, :, None, :, None] & ok_c[None, None, :, None, :]
    valid = valid.reshape(3, 1, NAT_TQ, NAT_TK)
    t = jnp.einsum('vamr,hrc,qkc->vhaqmk', jnp.asarray(sel_r), rpb.astype(f32), jnp.asarray(sel_c),
                   precision=lax.Precision.HIGHEST)
    t = t.reshape(3, NAT_HEADS, NAT_TQ, NAT_TK)
    return jnp.where(jnp.asarray(valid), t, NEG_INF)


def _nat_call(p, pc, bias):
    B, T, _ = p.shape
    L = pc.shape[1]
    W = NAT_HEADS * HEAD_DIM
    nsteps = T // NAT_TQ
    kern = functools.partial(_nat_kernel, nsteps=nsteps)
    var = lambda j: jnp.where(j == 0, 1, jnp.where(j == nsteps - 1, 2, 0))
    return pl.pallas_call(
        kern,
        grid=(B, nsteps),
        in_specs=[pl.BlockSpec((1, NAT_TQ, W), lambda b, j: (b, j, NQ_COL)),
                  pl.BlockSpec((1, T, W), lambda b, j: (b, 0, NK_COL)),
                  pl.BlockSpec((1, T, W), lambda b, j: (b, 0, NV_COL)),
                  pl.BlockSpec((1, L, W), lambda b, j: (b, 0, NK_COL)),
                  pl.BlockSpec((1, L, W), lambda b, j: (b, 0, NV_COL)),
                  pl.BlockSpec((1, NAT_HEADS, NAT_TQ, NAT_TK), lambda b, j: (var(j), 0, 0, 0))],
        out_specs=pl.BlockSpec((1, NAT_TQ, W), lambda b, j: (b, j, 0)),
        out_shape=jax.ShapeDtypeStruct((B, T, W), bf16),
        compiler_params=_cparams(("parallel", "arbitrary")),
        name="nat_attention",
    )(p, p, p, pc, pc, bias)


GQ_COL = 1536 // 512
GK_COL = 2048 // LANES
GV_COL = 2176 // LANES
GQA_R = GQA_Q_HEADS // GQA_KV_HEADS


def _rope(x, cos, sin):
    lane = lax.broadcasted_iota(jnp.int32, x.shape, 1)
    first = (lane % (HEAD_DIM // 2)) < (HEAD_DIM // 4)
    swapped = jnp.where(first, pltpu.roll(x, LANES - HEAD_DIM // 4, 1), pltpu.roll(x, HEAD_DIM // 4, 1))
    return x * cos + swapped * sin


def _rope_tables(S):
    t = np.arange(S)
    pos = np.stack([t // GRID_W, t % GRID_W], axis=1).astype(np.float64)
    quarter = HEAD_DIM // 4
    freqs = ROPE_BASE ** (-np.arange(quarter, dtype=np.float64) / quarter)
    lane = np.arange(LANES)
    which = (lane % HEAD_DIM) // (HEAD_DIM // 2)
    ang = pos[:, which] * freqs[lane % quarter][None, :]
    sign = np.where((lane % (HEAD_DIM // 2)) < quarter, -1.0, 1.0)[None, :]
    return jnp.asarray(np.cos(ang), f32), jnp.asarray(np.sin(ang) * sign, f32)


def _gqa_kernel(q_ref, kp_ref, kc_ref, kn_ref, vp_ref, vc_ref, vn_ref, kx_ref, vx_ref,
                cp_ref, cc_ref, cn_ref, sp_ref, sc_ref, sn_ref, sink_ref, o_ref, *, nb):
    n = pl.program_id(1)
    blk = GQA_BLOCK
    rows = GQA_R * blk
    cos_q, sin_q = cc_ref[...], sc_ref[...]
    qs = [(_rope(q_ref[0, :, r * LANES:(r + 1) * LANES].astype(f32), cos_q, sin_q) * 0.125) for r in range(GQA_R)]
    q_all = jnp.concatenate(qs, axis=0)
    low = _lane_is_low((rows, LANES))
    kp = _rope(kp_ref[0].astype(f32), cp_ref[...], sp_ref[...]).astype(bf16)
    kc = _rope(kc_ref[0].astype(f32), cos_q, sin_q).astype(bf16)
    kn = _rope(kn_ref[0].astype(f32), cn_ref[...], sn_ref[...]).astype(bf16)
    qi = lax.broadcasted_iota(jnp.int32, (rows, blk), 0) % blk
    kj = lax.broadcasted_iota(jnp.int32, (rows, blk), 1)
    bias_p = jnp.where((kj >= qi) & (n > 0), 0.0, NEG_INF)
    bias_n = jnp.where((kj <= qi) & (n < nb - 1), 0.0, NEG_INF)
    halves = []
    for g in range(GQA_KV_HEADS):
        qm = jnp.where(low if g == 0 else jnp.logical_not(low), q_all, 0.0).astype(bf16)
        logits = [_dot_nt(qm, kp) + bias_p, _dot_nt(qm, kc), _dot_nt(qm, kn) + bias_n, _dot_nt(qm, kx_ref[0])]
        vals = [vp_ref[0], vc_ref[0], vn_ref[0], vx_ref[0]]
        halves.append(_softmax_pv(logits, vals, extra=sink_ref[g][:, 0:1]))
    out = jnp.where(low, halves[0], halves[1])
    for r in range(GQA_R):
        o_ref[0, :, r * LANES:(r + 1) * LANES] = out[r * blk:(r + 1) * blk].astype(o_ref.dtype)


def _sink_cols(sink, blk):
    s = sink.astype(f32).reshape(GQA_KV_HEADS, GQA_R, 1, 1)
    return jnp.broadcast_to(s, (GQA_KV_HEADS, GQA_R, blk, LANES)).reshape(GQA_KV_HEADS, GQA_R * blk, LANES)


def _gqa_call(p, pc, cos, sin, sink):
    B, T, _ = p.shape
    L = pc.shape[1]
    blk = GQA_BLOCK
    nb = T // blk
    QW = GQA_Q_HEADS * HEAD_DIM
    kern = functools.partial(_gqa_kernel, nb=nb)
    prev = lambda n: jnp.maximum(n - 1, 0)
    nxt = lambda n: jnp.minimum(n + 1, nb - 1)
    kv = lambda col, f: pl.BlockSpec((1, blk, LANES), lambda b, n: (b, f(n), col))
    tab = lambda f: pl.BlockSpec((blk, LANES), lambda b, n: (f(n), 0))
    same = lambda n: n
    return pl.pallas_call(
        kern,
        grid=(B, nb),
        in_specs=[pl.BlockSpec((1, blk, QW), lambda b, n: (b, n, GQ_COL)),
                  kv(GK_COL, prev), kv(GK_COL, same), kv(GK_COL, nxt),
                  kv(GV_COL, prev), kv(GV_COL, same), kv(GV_COL, nxt),
                  pl.BlockSpec((1, L, LANES), lambda b, n: (b, 0, GK_COL)),
                  pl.BlockSpec((1, L, LANES), lambda b, n: (b, 0, GV_COL)),
                  tab(prev), tab(same), tab(nxt), tab(prev), tab(same), tab(nxt),
                  pl.BlockSpec((GQA_KV_HEADS, GQA_R * blk, LANES), lambda b, n: (0, 0, 0))],
        out_specs=pl.BlockSpec((1, blk, QW), lambda b, n: (b, n, 0)),
        out_shape=jax.ShapeDtypeStruct((B, T, QW), bf16),
        compiler_params=_cparams(("parallel", "parallel")),
        name="window_gqa",
    )(p, p, p, p, p, p, p, pc, pc, cos, cos, cos, sin, sin, sin, _sink_cols(sink, blk))


def _ctx_attn_kernel(nq_ref, nk_ref, nv_ref, gq_ref, gk_ref, gv_ref, sink_ref, ob_ref, od_ref, *, L):
    low = _lane_is_low((L, LANES))
    for c in range(NAT_HEADS // 2):
        cols = slice(c * LANES, (c + 1) * LANES)
        q2 = nq_ref[0, :, cols] * 0.125
        halves = []
        for e in range(2):
            qm = jnp.where(low if e == 0 else jnp.logical_not(low), q2, jnp.zeros_like(q2))
            halves.append(_softmax_pv([_dot_nt(qm, nk_ref[0, :, cols])], [nv_ref[0, :, cols]]))
        ob_ref[0, :, cols] = jnp.where(low, halves[0], halves[1]).astype(ob_ref.dtype)
    q_all = jnp.concatenate([gq_ref[0, :, r * LANES:(r + 1) * LANES] for r in range(GQA_R)], axis=0) * 0.125
    low4 = _lane_is_low((GQA_R * L, LANES))
    halves = []
    for g in range(GQA_KV_HEADS):
        qm = jnp.where(low4 if g == 0 else jnp.logical_not(low4), q_all, jnp.zeros_like(q_all))
        halves.append(_softmax_pv([_dot_nt(qm, gk_ref[0])], [gv_ref[0]], extra=sink_ref[g][:, 0:1]))
    out = jnp.where(low4, halves[0], halves[1])
    for r in range(GQA_R):
        od_ref[0, :, r * LANES:(r + 1) * LANES] = out[r * L:(r + 1) * L].astype(od_ref.dtype)


def _ctx_attn_call(pc, sink):
    B, L, _ = pc.shape
    W = NAT_HEADS * HEAD_DIM
    QW = GQA_Q_HEADS * HEAD_DIM
    kern = functools.partial(_ctx_attn_kernel, L=L)
    return pl.pallas_call(
        kern,
        grid=(B,),
        in_specs=[pl.BlockSpec((1, L, W), lambda b: (b, 0, NQ_COL)),
                  pl.BlockSpec((1, L, W), lambda b: (b, 0, NK_COL)),
                  pl.BlockSpec((1, L, W), lambda b: (b, 0, NV_COL)),
                  pl.BlockSpec((1, L, QW), lambda b: (b, 0, GQ_COL)),
                  pl.BlockSpec((1, L, LANES), lambda b: (b, 0, GK_COL)),
                  pl.BlockSpec((1, L, LANES), lambda b: (b, 0, GV_COL)),
                  pl.BlockSpec((GQA_KV_HEADS, GQA_R * L, LANES), lambda b: (0, 0, 0))],
        out_specs=[pl.BlockSpec((1, L, W), lambda b: (b, 0, 0)),
                   pl.BlockSpec((1, L, QW), lambda b: (b, 0, 0))],
        out_shape=[jax.ShapeDtypeStruct((B, L, W), bf16), jax.ShapeDtypeStruct((B, L, QW), bf16)],
        compiler_params=_cparams(("parallel",)),
        name="context_attention",
    )(pc, pc, pc, pc, pc, pc, _sink_cols(sink, L))


def _merge_kernel(x_ref, sh_ref, sc_ref, g1_ref, ya_ref, yb_ref, yc_ref, yd_ref, wg_ref, bg_ref,
                  wa_ref, wb_ref, wc_ref, wd_ref, wo_ref, lg_ref, lb_ref, o_ref):
    D = D_MODEL
    x = x_ref[0]
    xm = (_ln(x) * (1.0 + sc_ref[0]) + sh_ref[0]).astype(bf16)
    m = None
    for j, (y_ref, w_ref) in enumerate(((ya_ref, wa_ref), (yb_ref, wb_ref), (yc_ref, wc_ref), (yd_ref, wd_ref))):
        gate = jax.nn.sigmoid(_dot(xm, wg_ref[:, j * D:(j + 1) * D]) + bg_ref[:, j * D:(j + 1) * D])
        t = gate * _dot(y_ref[0], w_ref[...])
        m = t if m is None else m + t
    mix = _dot(m.astype(bf16), wo_ref[...])
    z = DEEPNORM_ALPHA * x + g1_ref[0] * mix
    o_ref[0] = _ln(z) * lg_ref[...] + lb_ref[...]


def _merge_call(x, sh, sc, g1, ya, yb, yc, yd, wg, bg, wa, wb, wc, wd, wo, lg, lb, tm):
    B, T, D = x.shape
    tok = lambda w: pl.BlockSpec((1, tm, w), lambda b, i: (b, i, 0))
    mod = pl.BlockSpec((1, 1, D), lambda b, i: (b, 0, 0))
    full = lambda a: pl.BlockSpec(a.shape, lambda b, i: (0,) * a.ndim)
    return pl.pallas_call(
        _merge_kernel,
        grid=(B, T // tm),
        in_specs=[tok(D), mod, mod, mod, tok(ya.shape[-1]), tok(yb.shape[-1]), tok(yc.shape[-1]), tok(yd.shape[-1]),
                  full(wg), full(bg), full(wa), full(wb), full(wc), full(wd), full(wo), full(lg), full(lb)],
        out_specs=tok(D),
        out_shape=jax.ShapeDtypeStruct((B, T, D), f32),
        compiler_params=_cparams(("parallel", "parallel")),
        name="merge_ln",
    )(x, sh, sc, g1, ya, yb, yc, yd, wg, bg, wa, wb, wc, wd, wo, lg, lb)


PEER_TM = 512
PEER_EBLK = 8
PEER_NTOP = PEER_TOPK + 1
BIG_NEG = -3.0e38
SQRT_HALF = 0.7071067811865476


def _top_values(s, count, rank_below=None):
    vals = []
    cur = s
    rank = None if rank_below is None else jnp.full(s.shape, float(rank_below), f32)
    for r in range(count):
        m = jnp.max(cur, axis=0, keepdims=True)
        vals.append(m)
        hit = cur == m
        if rank is not None and r < rank_below:
            rank = jnp.where(hit, float(r), rank)
        if r + 1 < count:
            cur = jnp.where(hit, BIG_NEG, cur)
    return vals, rank


def _peer_kernel(x_ref, sh_ref, sc_ref, g2_ref, wq_ref, k1_ref, k2_ref, eu_ref, ev_ref, lg_ref, lb_ref, o_ref,
                 ut_ref, n1_ref, e1_ref, r2_ref, e2_ref, acc_ref, *, tm, nsteps):
    step = pl.program_id(2)
    nk = PEER_N_KEYS

    @pl.when(step == 0)
    def _select():
        u = _ln(x_ref[0]) * (1.0 + sc_ref[0]) + sh_ref[0]
        ut_ref[...] = u.T.astype(bf16)
        acc_ref[...] = jnp.zeros_like(acc_ref)
        pairs = [(i, j) for i in range(PEER_NTOP) for j in range(PEER_NTOP) if (i + 1) * (j + 1) <= PEER_NTOP]
        pad = (-len(pairs)) % 8

        def head(h, carry):
            r0 = pl.multiple_of(h * 2 * nk, 2 * nk)
            q1 = _dot(wq_ref[pl.ds(r0, nk), :], ut_ref[...]).astype(bf16)
            q2 = _dot(wq_ref[pl.ds(r0 + nk, nk), :], ut_ref[...]).astype(bf16)
            s1 = _dot(k1_ref[h], q1)
            s2 = _dot(k2_ref[h], q2)
            for t in range(tm // LANES):
                ln = slice(t * LANES, (t + 1) * LANES)
                s1t, s2t = s1[:, ln], s2[:, ln]
                a, _ = _top_values(s1t, PEER_NTOP)
                b, rank2 = _top_values(s2t, PEER_NTOP, rank_below=PEER_TOPK)
                cand = jnp.concatenate([a[i] + b[j] for i, j in pairs]
                                       + [jnp.full((pad, LANES), BIG_NEG, f32)], axis=0)
                top, _ = _top_values(cand, PEER_NTOP)
                tau = 0.5 * (top[PEER_TOPK - 1] + top[PEER_TOPK])
                z = jnp.sum(jnp.where(cand > tau, jnp.exp(cand - (a[0] + b[0])), 0.0), axis=0, keepdims=True)
                n1 = jnp.zeros_like(s1t)
                for j in range(PEER_TOPK):
                    n1 = n1 + jnp.where(s1t + b[j] > tau, 1.0, 0.0)
                n1_ref[h, :, ln] = n1
                e1_ref[h, :, ln] = jnp.exp(s1t - a[0]) / z
                r2_ref[h, :, ln] = rank2.astype(bf16)
                e2_ref[h, :, ln] = jnp.exp(s2t - b[0]).astype(bf16)
            return carry

        lax.fori_loop(0, PEER_HEADS, head, 0)

    gs = []
    for blk in range(PEER_EBLK):
        i1 = step * PEER_EBLK + blk
        ht = _dot(eu_ref[blk * nk:(blk + 1) * nk, :], ut_ref[...])
        w = jnp.zeros((nk, tm), bf16)
        for h in range(PEER_HEADS):
            n1 = n1_ref[h, pl.ds(i1, 1), :].astype(bf16)
            e1 = e1_ref[h, pl.ds(i1, 1), :].astype(bf16)
            w = w + jnp.where(r2_ref[h] < n1, e2_ref[h] * e1, jnp.zeros((), bf16))
        gelu = 0.5 * ht * (1.0 + lax.erf(ht * SQRT_HALF))
        gs.append(w * gelu.astype(bf16))
    acc_ref[...] += _dot(ev_ref[...], jnp.concatenate(gs, axis=0))

    @pl.when(step == nsteps - 1)
    def _finish():
        z = DEEPNORM_ALPHA * x_ref[0] + g2_ref[0] * acc_ref[...].T
        o_ref[0] = _ln(z) * lg_ref[...] + lb_ref[...]


def _peer_call(x, sh, sc, g2, wq_t, k1, k2, eu, ev_t, lg, lb):
    B, T, D = x.shape
    tm = PEER_TM
    nk = PEER_N_KEYS
    ne = eu.shape[0]
    eb = PEER_EBLK * nk
    nsteps = ne // eb
    kern = functools.partial(_peer_kernel, tm=tm, nsteps=nsteps)
    mod = pl.BlockSpec((1, 1, D), lambda b, i, s: (b, 0, 0))
    full = lambda a: pl.BlockSpec(a.shape, lambda b, i, s: (0,) * a.ndim)
    sel = lambda dt: pltpu.VMEM((PEER_HEADS, nk, tm), dt)
    return pl.pallas_call(
        kern,
        grid=(B, T // tm, nsteps),
        in_specs=[pl.BlockSpec((1, tm, D), lambda b, i, s: (b, i, 0)), mod, mod, mod,
                  full(wq_t), full(k1), full(k2),
                  pl.BlockSpec((eb, D), lambda b, i, s: (s, 0)),
                  pl.BlockSpec((D, eb), lambda b, i, s: (0, s)),
                  full(lg), full(lb)],
        out_specs=pl.BlockSpec((1, tm, D), lambda b, i, s: (b, i, 0)),
        out_shape=jax.ShapeDtypeStruct((B, T, D), f32),
        scratch_shapes=[pltpu.VMEM((D, tm), bf16), sel(f32), sel(f32), sel(bf16), sel(bf16),
                        pltpu.VMEM((D, tm), f32)],
        compiler_params=_cparams(("parallel", "parallel", "arbitrary")),
        name="peer_ffn",
    )(x, sh, sc, g2, wq_t, k1, k2, eu, ev_t, lg, lb)


def _gqa_head_perm():
    cols = []
    for c in range(GQA_R):
        cols += list(range(c * HEAD_DIM, (c + 1) * HEAD_DIM))
        cols += list(range((GQA_R + c) * HEAD_DIM, (GQA_R + c + 1) * HEAD_DIM))
    return np.asarray(cols)


def kernel(x, c, ctx, c_ctx, w_ada, b_ada, w_in, b_in, conv_w, conv_b, conv_ln_g, conv_ln_b, nat_rpb, gqa_sink,
           w_branch_a, w_branch_b, w_branch_c, w_branch_d, w_out, ln1_g, ln1_b, peer_wq, peer_k1, peer_k2,
           peer_u, peer_v, ln2_g, ln2_b):
    B, S, D = x.shape
    L = ctx.shape[1]
    gq0, gq1 = 1536, 2048
    perm = _gqa_head_perm()

    cvec = jnp.concatenate([c, c_ctx[None, :], jnp.zeros((8 - B - 1, D), f32)], axis=0)
    mod = _ada_call(cvec, w_ada, b_ada)
    cos, sin = _rope_tables(S)
    cs_lat = _dft_time_mats(S)
    cs_ctx = _dft_time_mats(L)
    bd = _dft_chan_mats()
    row = lambda v: v.reshape(1, -1)

    xc = ctx.reshape(1, B * L, D)
    for i in range(DEPTH):
        lat = [mod[i, :B, k * D:(k + 1) * D].reshape(B, 1, D) for k in range(6)]
        con = [mod[i, B:B + 1, k * D:(k + 1) * D].reshape(1, 1, D) for k in range(6)]
        wi, bi = w_in[i], b_in[i]
        w_small = jnp.concatenate([wi[:, :gq0], wi[:, gq0:gq1][:, perm], wi[:, gq1:N_SMALL]], axis=1).astype(bf16)
        b_small = row(jnp.concatenate([bi[:gq0], bi[gq0:gq1][perm], bi[gq1:N_SMALL]]))
        w_gate = wi[:, N_SMALL:].astype(bf16)
        b_gate = row(bi[N_SMALL:])
        wa, wb, wc = w_branch_a[i].astype(bf16), w_branch_b[i].astype(bf16), w_branch_c[i].astype(bf16)
        wd = w_branch_d[i][perm].astype(bf16)
        wo = w_out[i].astype(bf16)
        merge_w = (w_gate, b_gate, wa, wb, wc, wd, wo, row(ln1_g[i]), row(ln1_b[i]))
        peer_w = (peer_wq[i].T.astype(bf16), peer_k1[i].astype(bf16), peer_k2[i].astype(bf16),
                  peer_u[i].astype(bf16), peer_v[i].T.astype(bf16), row(ln2_g[i]), row(ln2_b[i]))
        conv_p = (conv_w[i], conv_b[i], conv_ln_g[i], conv_ln_b[i])

        pc = _inproj_call(xc, con[0], con[1], w_small, b_small, 512).reshape(B, L, N_SMALL)
        p = _inproj_call(x, lat[0], lat[1], w_small, b_small, 512)
        y_a = _conv_call(p, *conv_p)
        y_b = _nat_call(p, pc, _nat_bias_tables(nat_rpb[i]))
        y_c = _fft_call(p, cs_lat, bd)
        y_d = _gqa_call(p, pc, cos, sin, gqa_sink[i])
        x = _merge_call(x, lat[0], lat[1], lat[2], y_a, y_b, y_c, y_d, *merge_w, 256)
        x = _peer_call(x, lat[3], lat[4], lat[5], *peer_w)

        if i < DEPTH - 1:
            yc_a = _conv_call(pc, *conv_p)
            yc_b, yc_d = _ctx_attn_call(pc, gqa_sink[i])
            yc_c = _fft_call(pc, cs_ctx, bd)
            flat = lambda a: a.reshape(1, B * L, a.shape[-1])
            xc = _merge_call(xc, con[0], con[1], con[2], flat(yc_a), flat(yc_b), flat(yc_c), flat(yc_d),
                             *merge_w, 256)
            xc = _peer_call(xc, con[3], con[4], con[5], *peer_w)
    return x
```

```python
import functools
import math

import numpy as np
import jax
import jax.numpy as jnp
from jax import lax
from jax.experimental import pallas as pl
from jax.experimental.pallas import tpu as pltpu

f32 = jnp.float32
bf16 = jnp.bfloat16

D_MODEL = 1024
DEPTH = 2
GRID_W = 64
CONV_CH = 256
CONV_WIDTH = 31
NAT_HEADS = 4
NAT_KH = 8
NAT_KW = 16
GQA_Q_HEADS = 8
GQA_KV_HEADS = 2
GQA_BLOCK = 128
HEAD_DIM = 64
ROPE_BASE = 10000.0
PEER_HEADS = 8
PEER_N_KEYS = 128
PEER_TOPK = 16
LN_EPS = 1e-5
NEG_INF = -1e30
DEEPNORM_ALPHA = (2 * DEPTH) ** 0.25

LANES = 128
N_SMALL = 2304
VMEM_LIMIT = 56 * 1024 * 1024


def _cparams(sem, vmem=VMEM_LIMIT):
    return pltpu.CompilerParams(dimension_semantics=sem, vmem_limit_bytes=vmem)


def _ln(x):
    mu = jnp.mean(x, axis=-1, keepdims=True)
    xc = x - mu
    var = jnp.mean(xc * xc, axis=-1, keepdims=True)
    return xc * lax.rsqrt(var + LN_EPS)


def _dot(a, b):
    return jnp.dot(a, b, preferred_element_type=f32)


def _dot_nt(a, b):
    return lax.dot_general(a, b, (((1,), (1,)), ((), ())), preferred_element_type=f32)


def _ada_kernel(c_ref, w_ref, b_ref, o_ref):
    c = c_ref[...]
    h = c * jax.nn.sigmoid(c)
    o_ref[0] = jnp.dot(h, w_ref[0], preferred_element_type=f32,
                       precision=lax.Precision.HIGHEST) + b_ref[0]


def _ada_call(cvec, w_ada, b_ada):
    L, D, N = w_ada.shape
    tn = 1536
    return pl.pallas_call(
        _ada_kernel,
        grid=(L, N // tn),
        in_specs=[pl.BlockSpec((8, D), lambda l, j: (0, 0)),
                  pl.BlockSpec((1, D, tn), lambda l, j: (l, 0, j)),
                  pl.BlockSpec((1, 1, tn), lambda l, j: (l, 0, j))],
        out_specs=pl.BlockSpec((1, 8, tn), lambda l, j: (l, 0, j)),
        out_shape=jax.ShapeDtypeStruct((L, 8, N), f32),
        compiler_params=_cparams(("parallel", "parallel")),
        name="ada_mod",
    )(cvec, w_ada, b_ada.reshape(L, 1, N))


F_OFF = 1280
FNET_W = 256


def _inproj_kernel(x_ref, sh_ref, sc_ref, w_ref, b_ref, o_ref, of_ref):
    xm = _ln(x_ref[0]) * (1.0 + sc_ref[0]) + sh_ref[0]
    y = (_dot(xm.astype(bf16), w_ref[...]) + b_ref[...]).astype(o_ref.dtype)
    o_ref[0] = y
    of_ref[0] = y[:, F_OFF:F_OFF + FNET_W]


def _inproj_call(x, sh, sc, w, b, tm):
    B, T, D = x.shape
    N = w.shape[1]
    return pl.pallas_call(
        _inproj_kernel,
        grid=(B, T // tm),
        in_specs=[pl.BlockSpec((1, tm, D), lambda b, i: (b, i, 0)),
                  pl.BlockSpec((1, 1, D), lambda b, i: (b, 0, 0)),
                  pl.BlockSpec((1, 1, D), lambda b, i: (b, 0, 0)),
                  pl.BlockSpec((D, N), lambda b, i: (0, 0)),
                  pl.BlockSpec((1, N), lambda b, i: (0, 0))],
        out_specs=[pl.BlockSpec((1, tm, N), lambda b, i: (b, i, 0)),
                   pl.BlockSpec((1, tm, FNET_W), lambda b, i: (b, i, 0))],
        out_shape=[jax.ShapeDtypeStruct((B, T, N), bf16), jax.ShapeDtypeStruct((B, T, FNET_W), bf16)],
        compiler_params=_cparams(("parallel", "parallel")),
        name="in_proj",
    )(x, sh, sc, w, b)


CONV_HALO = 16
CONV_SUB = 128


def _conv_kernel(prev_ref, cur_ref, next_ref, w_ref, cb_ref, g_ref, b_ref, o_ref, hs_ref, *, tc, nchunks):
    i = pl.program_id(1)

    def glu(v):
        v = v.astype(f32)
        return v[:, :CONV_CH] * jax.nn.sigmoid(v[:, CONV_CH:])

    hs_ref[0:CONV_HALO, :] = jnp.where(i > 0, glu(prev_ref[0]), 0.0)
    hs_ref[CONV_HALO:CONV_HALO + tc, :] = glu(cur_ref[0])
    hs_ref[CONV_HALO + tc:2 * CONV_HALO + tc, :] = jnp.where(i < nchunks - 1, glu(next_ref[0]), 0.0)
    base = CONV_HALO - CONV_WIDTH // 2
    for s in range(tc // CONV_SUB):
        acc = jnp.zeros((CONV_SUB, CONV_CH), f32)
        for j in range(CONV_WIDTH):
            acc = acc + hs_ref[pl.ds(s * CONV_SUB + base + j, CONV_SUB), :] * w_ref[j:j + 1, :]
        y = _ln(acc + cb_ref[...]) * g_ref[...] + b_ref[...]
        y = y * jax.nn.sigmoid(y)
        o_ref[0, s * CONV_SUB:(s + 1) * CONV_SUB, :] = y.astype(o_ref.dtype)


def _conv_call(p, conv_w, conv_b, ln_g, ln_b):
    B, T, _ = p.shape
    tc = min(512, T)
    nchunks = T // tc
    hb = tc // CONV_HALO
    nhb = T // CONV_HALO
    width = 2 * CONV_CH
    kern = functools.partial(_conv_kernel, tc=tc, nchunks=nchunks)
    vec = lambda v: v.reshape(1, CONV_CH)
    return pl.pallas_call(
        kern,
        grid=(B, nchunks),
        in_specs=[pl.BlockSpec((1, CONV_HALO, width), lambda b, i: (b, jnp.maximum(i * hb - 1, 0), 0)),
                  pl.BlockSpec((1, tc, width), lambda b, i: (b, i, 0)),
                  pl.BlockSpec((1, CONV_HALO, width), lambda b, i: (b, jnp.minimum((i + 1) * hb, nhb - 1), 0)),
                  pl.BlockSpec((CONV_WIDTH, CONV_CH), lambda b, i: (0, 0)),
                  pl.BlockSpec((1, CONV_CH), lambda b, i: (0, 0)),
                  pl.BlockSpec((1, CONV_CH), lambda b, i: (0, 0)),
                  pl.BlockSpec((1, CONV_CH), lambda b, i: (0, 0))],
        out_specs=pl.BlockSpec((1, tc, CONV_CH), lambda b, i: (b, i, 0)),
        out_shape=jax.ShapeDtypeStruct((B, T, CONV_CH), bf16),
        scratch_shapes=[pltpu.VMEM((tc + 2 * CONV_HALO, CONV_CH), f32)],
        compiler_params=_cparams(("parallel", "parallel")),
        name="conv_branch",
    )(p, p, p, conv_w, vec(conv_b), vec(ln_g), vec(ln_b))


FNET_GROUP_DIM = 64


def _fft_kernel(f_ref, cs_ref, bd_ref, o_ref, rhs_ref, *, T, scale):
    i = pl.program_id(0)
    b = pl.program_id(1)

    @pl.when(i == 0)
    def _():
        rows = min(512, T)
        for r in range(T // rows):
            z = _dot(f_ref[0, r * rows:(r + 1) * rows, :], bd_ref[...])
            rhs_ref[b, r * rows:(r + 1) * rows, :] = z[:, :FNET_W].astype(bf16)
            rhs_ref[b, T + r * rows:T + (r + 1) * rows, :] = (-z[:, FNET_W:]).astype(bf16)

    o_ref[0] = (_dot(cs_ref[...], rhs_ref[b]) * scale).astype(o_ref.dtype)


def _dft_time_mats(T):
    t = jnp.arange(T, dtype=jnp.int32)
    ang = ((t[:, None] * t[None, :]) % T).astype(f32) * f32(2.0 * math.pi / T)
    return jnp.concatenate([jnp.cos(ang), jnp.sin(ang)], axis=1).astype(bf16)


def _dft_chan_mats():
    n = FNET_GROUP_DIM
    k = np.arange(n)
    ang = 2.0 * np.pi * ((k[:, None] * k[None, :]) % n) / n
    eye = np.eye(FNET_W // n)
    return jnp.asarray(np.concatenate([np.kron(eye, np.cos(ang)), np.kron(eye, np.sin(ang))], axis=1), f32).astype(bf16)


def _fft_call(pf, cs, bd):
    B, T, _ = pf.shape
    p = pf
    tm = min(512, T)
    kern = functools.partial(_fft_kernel, T=T, scale=1.0 / math.sqrt(T * FNET_GROUP_DIM))
    return pl.pallas_call(
        kern,
        grid=(T // tm, B),
        in_specs=[pl.BlockSpec((1, T, FNET_W), lambda i, b: (b, 0, 0)),
                  pl.BlockSpec((tm, 2 * T), lambda i, b: (i, 0)),
                  pl.BlockSpec((FNET_W, 2 * FNET_W), lambda i, b: (0, 0))],
        out_specs=pl.BlockSpec((1, tm, FNET_W), lambda i, b: (b, i, 0)),
        out_shape=jax.ShapeDtypeStruct((B, T, FNET_W), bf16),
        scratch_shapes=[pltpu.VMEM((B, 2 * T, FNET_W), bf16)],
        compiler_params=_cparams(("arbitrary", "arbitrary")),
        name="fourier_branch",
    )(p, cs, bd)


FFT_N = 64
FFT_CHUNK = 16


def _fft_s1_kernel(x_ref, cs_ref, cd_ref, twr_ref, twi_ref, o_ref, scr_ref):
    n, w = FFT_N, FNET_W
    a = _dot(cs_ref[...], x_ref[0])
    for j in range(FFT_CHUNK):
        scr_ref[j * n:(j + 1) * n, 0:w] = a[0:n, j * w:(j + 1) * w].astype(bf16)
        scr_ref[j * n:(j + 1) * n, w:2 * w] = a[n:2 * n, j * w:(j + 1) * w].astype(bf16)
    z = _dot(scr_ref[...], cd_ref[...])
    twr = jnp.concatenate([twr_ref[...]] * (w // LANES), axis=1)
    twi = jnp.concatenate([twi_ref[...]] * (w // LANES), axis=1)
    zr, zi = z[:, :w], z[:, w:]
    o_ref[0, :, 0:w] = (zr * twr - zi * twi).astype(o_ref.dtype)
    o_ref[0, :, w:2 * w] = (zr * twi + zi * twr).astype(o_ref.dtype)


def _fft_s2_kernel(b_ref, cs_ref, o_ref, *, scale):
    n, w = FFT_N, FNET_W
    r = _dot(cs_ref[...], b_ref[0])
    for j in range(FFT_CHUNK):
        y = r[0:n, j * 2 * w:j * 2 * w + w] + r[n:2 * n, j * 2 * w + w:(j + 1) * 2 * w]
        o_ref[0, :, j * w:(j + 1) * w] = (y * scale).astype(o_ref.dtype)


def _fft_ct_mats():
    n, w = FFT_N, FNET_W
    k = np.arange(n)
    ang = 2.0 * np.pi * ((k[:, None] * k[None, :]) % n) / n
    c, s = np.cos(ang), np.sin(ang)
    g = FNET_GROUP_DIM
    kg = np.arange(g)
    ang_g = 2.0 * np.pi * ((kg[:, None] * kg[None, :]) % g) / g
    eye = np.eye(w // g)
    cbd, sbd = np.kron(eye, np.cos(ang_g)), np.kron(eye, np.sin(ang_g))
    cs1 = np.concatenate([c, -s], axis=0)
    cd = np.block([[cbd, -sbd], [sbd, cbd]])
    cs2 = np.concatenate([c, s], axis=0)
    t2 = np.arange(n)[:, None]
    k1 = np.arange(n)[None, :]
    tw = 2.0 * np.pi * ((t2 * k1) % (n * n)) / (n * n)
    twr = np.broadcast_to(np.cos(tw).reshape(n * n, 1), (n * n, LANES))
    twi = np.broadcast_to(-np.sin(tw).reshape(n * n, 1), (n * n, LANES))
    as_bf = lambda m: jnp.asarray(m, f32).astype(bf16)
    return as_bf(cs1), as_bf(cd), as_bf(cs2), jnp.asarray(twr, f32), jnp.asarray(twi, f32)


def _fft_ct_call(pf, mats):
    B, T, w = pf.shape
    n, ch = FFT_N, FFT_CHUNK
    cs1, cd, cs2, twr, twi = mats
    full = lambda a: pl.BlockSpec(a.shape, lambda b, i: (0,) * a.ndim)
    stage1 = pl.pallas_call(
        _fft_s1_kernel,
        grid=(B, n // ch),
        in_specs=[pl.BlockSpec((1, n, ch * w), lambda b, i: (b, 0, i)), full(cs1), full(cd),
                  pl.BlockSpec((ch * n, LANES), lambda b, i: (i, 0)),
                  pl.BlockSpec((ch * n, LANES), lambda b, i: (i, 0))],
        out_specs=pl.BlockSpec((1, ch * n, 2 * w), lambda b, i: (b, i, 0)),
        out_shape=jax.ShapeDtypeStruct((B, T, 2 * w), bf16),
        scratch_shapes=[pltpu.VMEM((ch * n, 2 * w), bf16)],
        compiler_params=_cparams(("parallel", "parallel")),
        name="fourier_stage1",
    )(pf.reshape(B, n, n * w), cs1, cd, twr, twi)
    kern2 = functools.partial(_fft_s2_kernel, scale=1.0 / math.sqrt(T * FNET_GROUP_DIM))
    out = pl.pallas_call(
        kern2,
        grid=(B, n // ch),
        in_specs=[pl.BlockSpec((1, n, ch * 2 * w), lambda b, i: (b, 0, i)), full(cs2)],
        out_specs=pl.BlockSpec((1, n, ch * w), lambda b, i: (b, 0, i)),
        out_shape=jax.ShapeDtypeStruct((B, n, n * w), bf16),
        compiler_params=_cparams(("parallel", "parallel")),
        name="fourier_stage2",
    )(stage1.reshape(B, n, n * 2 * w), cs2)
    return out.reshape(B, T, w)


def _lane_is_low(shape):
    return lax.broadcasted_iota(jnp.int32, shape, len(shape) - 1) < HEAD_DIM


def _softmax_pv(logits, values, extra=None):
    def fold(x, op, acc):
        for c in range(x.shape[1] // LANES):
            blk = x[:, c * LANES:(c + 1) * LANES]
            acc = blk if acc is None else op(acc, blk)
        return acc

    m_el = None
    for s in logits:
        m_el = fold(s, jnp.maximum, m_el)
    m = m_el.max(axis=-1, keepdims=True)
    if extra is not None:
        m = jnp.maximum(m, extra)
    d_el = None
    out = None
    for s, v in zip(logits, values):
        p = jnp.exp(s - m)
        d_el = fold(p, jnp.add, d_el)
        o = _dot(p.astype(bf16), v)
        out = o if out is None else out + o
    den = d_el.sum(axis=-1, keepdims=True)
    if extra is not None:
        den = den + jnp.exp(extra - m)
    return out * (1.0 / den)


NAT_QROWS = 8
NAT_BAND = 16
NAT_TQ = NAT_QROWS * GRID_W
NAT_TK = NAT_BAND * GRID_W
NQ_COL, NK_COL, NV_COL = 2, 3, 4


def _nat_kernel(q_ref, k_ref, v_ref, kc_ref, vc_ref, bias_ref, o_ref, *, nsteps):
    j = pl.program_id(1)
    start = jnp.clip(j * NAT_QROWS - NAT_KH // 2, 0, nsteps * NAT_QROWS - NAT_BAND) * GRID_W
    start = pl.multiple_of(start, GRID_W * 4)
    low = _lane_is_low((NAT_TQ, LANES))
    for c in range(NAT_HEADS // 2):
        cols = slice(c * LANES, (c + 1) * LANES)
        q2 = q_ref[0, :, cols] * 0.125
        k2 = k_ref[0, pl.ds(start, NAT_TK), cols]
        v2 = v_ref[0, pl.ds(start, NAT_TK), cols]
        kc2 = kc_ref[0, :, cols]
        vc2 = vc_ref[0, :, cols]
        halves = []
        for e in range(2):
            qm = jnp.where(low if e == 0 else jnp.logical_not(low), q2, jnp.zeros_like(q2))
            s = _dot_nt(qm, k2) + bias_ref[0, 2 * c + e]
            sc = _dot_nt(qm, kc2)
            halves.append(_softmax_pv([s, sc], [v2, vc2]))
        o_ref[0, :, cols] = jnp.where(low, halves[0], halves[1]).astype(o_ref.dtype)


def _nat_bias_tables(rpb):
    rows = 64
    nsteps = rows // NAT_QROWS
    a = np.arange(NAT_QROWS)[:, None]
    m = np.arange(NAT_BAND)[None, :]
    sel_r = np.zeros((3, NAT_QROWS, NAT_BAND, 2 * NAT_KH - 1), np.float32)
    ok_r = np.zeros((3, NAT_QROWS, NAT_BAND), bool)
    for v, jj in enumerate((1, 0, nsteps - 1)):
        band0 = int(np.clip(jj * NAT_QROWS - NAT_KH // 2, 0, rows - NAT_BAND))
        qr = jj * NAT_QROWS + a
        kr = band0 + m
        rstart = np.clip(qr - NAT_KH // 2, 0, rows - NAT_KH)
        ok = (kr >= rstart) & (kr < rstart + NAT_KH)
        dr = np.clip(kr - qr + NAT_KH - 1, 0, 2 * NAT_KH - 2)
        ok_r[v] = ok
        sel_r[v] = np.eye(2 * NAT_KH - 1, dtype=np.float32)[dr] * ok[..., None]
    qc = np.arange(GRID_W)[:, None]
    kc = np.arange(GRID_W)[None, :]
    cstart = np.clip(qc - NAT_KW // 2, 0, GRID_W - NAT_KW)
    ok_c = (kc >= cstart) & (kc < cstart + NAT_KW)
    dc = np.clip(kc - qc + NAT_KW - 1, 0, 2 * NAT_KW - 2)
    sel_c = np.eye(2 * NAT_KW - 1, dtype=np.float32)[dc] * ok_c[..., None]
    valid = ok_r[:, :, None, :, None] & ok_c[None, None, :, None, :]
    valid = valid.reshape(3, 1, NAT_TQ, NAT_TK)
    t = jnp.einsum('vamr,hrc,qkc->vhaqmk', jnp.asarray(sel_r), rpb.astype(f32), jnp.asarray(sel_c),
                   precision=lax.Precision.HIGHEST)
    t = t.reshape(3, NAT_HEADS, NAT_TQ, NAT_TK)
    return jnp.where(jnp.asarray(valid), t, NEG_INF)


def _nat_call(p, pc, bias):
    B, T, _ = p.shape
    L = pc.shape[1]
    W = NAT_HEADS * HEAD_DIM
    nsteps = T // NAT_TQ
    kern = functools.partial(_nat_kernel, nsteps=nsteps)
    var = lambda j: jnp.where(j == 0, 1, jnp.where(j == nsteps - 1, 2, 0))
    return pl.pallas_call(
        kern,
        grid=(B, nsteps),
        in_specs=[pl.BlockSpec((1, NAT_TQ, W), lambda b, j: (b, j, NQ_COL)),
                  pl.BlockSpec((1, T, W), lambda b, j: (b, 0, NK_COL)),
                  pl.BlockSpec((1, T, W), lambda b, j: (b, 0, NV_COL)),
                  pl.BlockSpec((1, L, W), lambda b, j: (b, 0, NK_COL)),
                  pl.BlockSpec((1, L, W), lambda b, j: (b, 0, NV_COL)),
                  pl.BlockSpec((1, NAT_HEADS, NAT_TQ, NAT_TK), lambda b, j: (var(j), 0, 0, 0))],
        out_specs=pl.BlockSpec((1, NAT_TQ, W), lambda b, j: (b, j, 0)),
        out_shape=jax.ShapeDtypeStruct((B, T, W), bf16),
        compiler_params=_cparams(("parallel", "arbitrary")),
        name="nat_attention",
    )(p, p, p, pc, pc, bias)


GQ_COL = 1536 // 512
GK_COL = 2048 // LANES
GV_COL = 2176 // LANES
GQA_R = GQA_Q_HEADS // GQA_KV_HEADS


def _rope(x, cos, sin):
    lane = lax.broadcasted_iota(jnp.int32, x.shape, 1)
    first = (lane % (HEAD_DIM // 2)) < (HEAD_DIM // 4)
    swapped = jnp.where(first, pltpu.roll(x, LANES - HEAD_DIM // 4, 1), pltpu.roll(x, HEAD_DIM // 4, 1))
    return x * cos + swapped * sin


def _rope_tables(S):
    t = np.arange(S)
    pos = np.stack([t // GRID_W, t % GRID_W], axis=1).astype(np.float64)
    quarter = HEAD_DIM // 4
    freqs = ROPE_BASE ** (-np.arange(quarter, dtype=np.float64) / quarter)
    lane = np.arange(LANES)
    which = (lane % HEAD_DIM) // (HEAD_DIM // 2)
    ang = pos[:, which] * freqs[lane % quarter][None, :]
    sign = np.where((lane % (HEAD_DIM // 2)) < quarter, -1.0, 1.0)[None, :]
    return jnp.asarray(np.cos(ang), f32), jnp.asarray(np.sin(ang) * sign, f32)


def _gqa_kernel(q_ref, kp_ref, kc_ref, kn_ref, vp_ref, vc_ref, vn_ref, kx_ref, vx_ref,
                cp_ref, cc_ref, cn_ref, sp_ref, sc_ref, sn_ref, sink_ref, band_ref, o_ref, *, nb):
    n = pl.program_id(1)
    blk = GQA_BLOCK
    rows = GQA_R * blk
    cos_q, sin_q = cc_ref[...], sc_ref[...]
    qs = [(_rope(q_ref[0, :, r * LANES:(r + 1) * LANES].astype(f32), cos_q, sin_q) * 0.125) for r in range(GQA_R)]
    q_all = jnp.concatenate(qs, axis=0)
    low = _lane_is_low((rows, LANES))
    kp = _rope(kp_ref[0].astype(f32), cp_ref[...], sp_ref[...]).astype(bf16)
    kc = _rope(kc_ref[0].astype(f32), cos_q, sin_q).astype(bf16)
    kn = _rope(kn_ref[0].astype(f32), cn_ref[...], sn_ref[...]).astype(bf16)
    bias_p = jnp.minimum(band_ref[0], jnp.where(n > 0, 0.0, NEG_INF))
    bias_n = jnp.minimum(band_ref[1], jnp.where(n < nb - 1, 0.0, NEG_INF))
    halves = []
    for g in range(GQA_KV_HEADS):
        qm = jnp.where(low if g == 0 else jnp.logical_not(low), q_all, 0.0).astype(bf16)
        logits = [_dot_nt(qm, kp) + bias_p, _dot_nt(qm, kc), _dot_nt(qm, kn) + bias_n, _dot_nt(qm, kx_ref[0])]
        vals = [vp_ref[0], vc_ref[0], vn_ref[0], vx_ref[0]]
        halves.append(_softmax_pv(logits, vals, extra=sink_ref[g][:, 0:1]))
    out = jnp.where(low, halves[0], halves[1])
    for r in range(GQA_R):
        o_ref[0, :, r * LANES:(r + 1) * LANES] = out[r * blk:(r + 1) * blk].astype(o_ref.dtype)


def _sink_cols(sink, blk):
    s = sink.astype(f32).reshape(GQA_KV_HEADS, GQA_R, 1, 1)
    return jnp.broadcast_to(s, (GQA_KV_HEADS, GQA_R, blk, LANES)).reshape(GQA_KV_HEADS, GQA_R * blk, LANES)


def _gqa_band_masks():
    qi = np.arange(GQA_R * GQA_BLOCK)[:, None] % GQA_BLOCK
    kj = np.arange(GQA_BLOCK)[None, :]
    return jnp.asarray(np.stack([np.where(kj >= qi, 0.0, NEG_INF), np.where(kj <= qi, 0.0, NEG_INF)]), f32)


def _gqa_call(p, pc, cos, sin, sink):
    B, T, _ = p.shape
    L = pc.shape[1]
    blk = GQA_BLOCK
    nb = T // blk
    QW = GQA_Q_HEADS * HEAD_DIM
    kern = functools.partial(_gqa_kernel, nb=nb)
    prev = lambda n: jnp.maximum(n - 1, 0)
    nxt = lambda n: jnp.minimum(n + 1, nb - 1)
    kv = lambda col, f: pl.BlockSpec((1, blk, LANES), lambda b, n: (b, f(n), col))
    tab = lambda f: pl.BlockSpec((blk, LANES), lambda b, n: (f(n), 0))
    same = lambda n: n
    return pl.pallas_call(
        kern,
        grid=(B, nb),
        in_specs=[pl.BlockSpec((1, blk, QW), lambda b, n: (b, n, GQ_COL)),
                  kv(GK_COL, prev), kv(GK_COL, same), kv(GK_COL, nxt),
                  kv(GV_COL, prev), kv(GV_COL, same), kv(GV_COL, nxt),
                  pl.BlockSpec((1, L, LANES), lambda b, n: (b, 0, GK_COL)),
                  pl.BlockSpec((1, L, LANES), lambda b, n: (b, 0, GV_COL)),
                  tab(prev), tab(same), tab(nxt), tab(prev), tab(same), tab(nxt),
                  pl.BlockSpec((GQA_KV_HEADS, GQA_R * blk, LANES), lambda b, n: (0, 0, 0)),
                  pl.BlockSpec((2, GQA_R * blk, blk), lambda b, n: (0, 0, 0))],
        out_specs=pl.BlockSpec((1, blk, QW), lambda b, n: (b, n, 0)),
        out_shape=jax.ShapeDtypeStruct((B, T, QW), bf16),
        compiler_params=_cparams(("parallel", "parallel")),
        name="window_gqa",
    )(p, p, p, p, p, p, p, pc, pc, cos, cos, cos, sin, sin, sin, _sink_cols(sink, blk), _gqa_band_masks())


def _ctx_attn_kernel(nq_ref, nk_ref, nv_ref, gq_ref, gk_ref, gv_ref, sink_ref, ob_ref, od_ref, *, L):
    low = _lane_is_low((L, LANES))
    for c in range(NAT_HEADS // 2):
        cols = slice(c * LANES, (c + 1) * LANES)
        q2 = nq_ref[0, :, cols] * 0.125
        halves = []
        for e in range(2):
            qm = jnp.where(low if e == 0 else jnp.logical_not(low), q2, jnp.zeros_like(q2))
            halves.append(_softmax_pv([_dot_nt(qm, nk_ref[0, :, cols])], [nv_ref[0, :, cols]]))
        ob_ref[0, :, cols] = jnp.where(low, halves[0], halves[1]).astype(ob_ref.dtype)
    q_all = jnp.concatenate([gq_ref[0, :, r * LANES:(r + 1) * LANES] for r in range(GQA_R)], axis=0) * 0.125
    low4 = _lane_is_low((GQA_R * L, LANES))
    halves = []
    for g in range(GQA_KV_HEADS):
        qm = jnp.where(low4 if g == 0 else jnp.logical_not(low4), q_all, jnp.zeros_like(q_all))
        halves.append(_softmax_pv([_dot_nt(qm, gk_ref[0])], [gv_ref[0]], extra=sink_ref[g][:, 0:1]))
    out = jnp.where(low4, halves[0], halves[1])
    for r in range(GQA_R):
        od_ref[0, :, r * LANES:(r + 1) * LANES] = out[r * L:(r + 1) * L].astype(od_ref.dtype)


def _ctx_attn_call(pc, sink):
    B, L, _ = pc.shape
    W = NAT_HEADS * HEAD_DIM
    QW = GQA_Q_HEADS * HEAD_DIM
    kern = functools.partial(_ctx_attn_kernel, L=L)
    return pl.pallas_call(
        kern,
        grid=(B,),
        in_specs=[pl.BlockSpec((1, L, W), lambda b: (b, 0, NQ_COL)),
                  pl.BlockSpec((1, L, W), lambda b: (b, 0, NK_COL)),
                  pl.BlockSpec((1, L, W), lambda b: (b, 0, NV_COL)),
                  pl.BlockSpec((1, L, QW), lambda b: (b, 0, GQ_COL)),
                  pl.BlockSpec((1, L, LANES), lambda b: (b, 0, GK_COL)),
                  pl.BlockSpec((1, L, LANES), lambda b: (b, 0, GV_COL)),
                  pl.BlockSpec((GQA_KV_HEADS, GQA_R * L, LANES), lambda b: (0, 0, 0))],
        out_specs=[pl.BlockSpec((1, L, W), lambda b: (b, 0, 0)),
                   pl.BlockSpec((1, L, QW), lambda b: (b, 0, 0))],
        out_shape=[jax.ShapeDtypeStruct((B, L, W), bf16), jax.ShapeDtypeStruct((B, L, QW), bf16)],
        compiler_params=_cparams(("parallel",)),
        name="context_attention",
    )(pc, pc, pc, pc, pc, pc, _sink_cols(sink, L))


def _merge_kernel(x_ref, sh_ref, sc_ref, g1_ref, ya_ref, yb_ref, yc_ref, yd_ref, wg_ref, bg_ref,
                  wa_ref, wb_ref, wc_ref, wd_ref, wo_ref, lg_ref, lb_ref, o_ref):
    D = D_MODEL
    x = x_ref[0]
    xm = (_ln(x) * (1.0 + sc_ref[0]) + sh_ref[0]).astype(bf16)
    m = None
    for j, (y_ref, w_ref) in enumerate(((ya_ref, wa_ref), (yb_ref, wb_ref), (yc_ref, wc_ref), (yd_ref, wd_ref))):
        gate = jax.nn.sigmoid(_dot(xm, wg_ref[:, j * D:(j + 1) * D]) + bg_ref[:, j * D:(j + 1) * D])
        t = gate * _dot(y_ref[0], w_ref[...])
        m = t if m is None else m + t
    mix = _dot(m.astype(bf16), wo_ref[...])
    z = DEEPNORM_ALPHA * x + g1_ref[0] * mix
    o_ref[0] = _ln(z) * lg_ref[...] + lb_ref[...]


def _merge_call(x, sh, sc, g1, ya, yb, yc, yd, wg, bg, wa, wb, wc, wd, wo, lg, lb, tm):
    B, T, D = x.shape
    tok = lambda w: pl.BlockSpec((1, tm, w), lambda b, i: (b, i, 0))
    mod = pl.BlockSpec((1, 1, D), lambda b, i: (b, 0, 0))
    full = lambda a: pl.BlockSpec(a.shape, lambda b, i: (0,) * a.ndim)
    return pl.pallas_call(
        _merge_kernel,
        grid=(B, T // tm),
        in_specs=[tok(D), mod, mod, mod, tok(ya.shape[-1]), tok(yb.shape[-1]), tok(yc.shape[-1]), tok(yd.shape[-1]),
                  full(wg), full(bg), full(wa), full(wb), full(wc), full(wd), full(wo), full(lg), full(lb)],
        out_specs=tok(D),
        out_shape=jax.ShapeDtypeStruct((B, T, D), f32),
        compiler_params=_cparams(("parallel", "parallel")),
        name="merge_ln",
    )(x, sh, sc, g1, ya, yb, yc, yd, wg, bg, wa, wb, wc, wd, wo, lg, lb)


PEER_TM = 512
PEER_EBLK = 8
PEER_NTOP = PEER_TOPK + 1
BIG_NEG = -3.0e38
SQRT_HALF = 0.7071067811865476


def _top_values(s, count, rank_below=None):
    vals = []
    cur = s
    rank = None if rank_below is None else jnp.full(s.shape, float(rank_below), f32)
    for r in range(count):
        m = jnp.max(cur, axis=0, keepdims=True)
        vals.append(m)
        hit = cur == m
        if rank is not None and r < rank_below:
            rank = jnp.where(hit, float(r), rank)
        if r + 1 < count:
            cur = jnp.where(hit, BIG_NEG, cur)
    return vals, rank


def _peer_kernel(x_ref, sh_ref, sc_ref, g2_ref, wq_ref, k1_ref, k2_ref, eu_ref, ev_ref, lg_ref, lb_ref, o_ref,
                 ut_ref, n1_ref, e1_ref, r2_ref, e2_ref, acc_ref, *, tm, nsteps):
    step = pl.program_id(2)
    nk = PEER_N_KEYS

    @pl.when(step == 0)
    def _select():
        u = _ln(x_ref[0]) * (1.0 + sc_ref[0]) + sh_ref[0]
        ut_ref[...] = u.T.astype(bf16)
        acc_ref[...] = jnp.zeros_like(acc_ref)
        pairs = [(i, j) for i in range(PEER_NTOP) for j in range(PEER_NTOP) if (i + 1) * (j + 1) <= PEER_NTOP]
        pad = (-len(pairs)) % 8

        def head(h, carry):
            r0 = pl.multiple_of(h * 2 * nk, 2 * nk)
            q1 = _dot(wq_ref[pl.ds(r0, nk), :], ut_ref[...]).astype(bf16)
            q2 = _dot(wq_ref[pl.ds(r0 + nk, nk), :], ut_ref[...]).astype(bf16)
            s1 = _dot(k1_ref[h], q1)
            s2 = _dot(k2_ref[h], q2)
            for t in range(tm // LANES):
                ln = slice(t * LANES, (t + 1) * LANES)
                s1t, s2t = s1[:, ln], s2[:, ln]
                a, _ = _top_values(s1t, PEER_NTOP)
                b, rank2 = _top_values(s2t, PEER_NTOP, rank_below=PEER_TOPK)
                cand = jnp.concatenate([a[i] + b[j] for i, j in pairs]
                                       + [jnp.full((pad, LANES), BIG_NEG, f32)], axis=0)
                top, _ = _top_values(cand, PEER_NTOP)
                tau = 0.5 * (top[PEER_TOPK - 1] + top[PEER_TOPK])
                z = jnp.sum(jnp.where(cand > tau, jnp.exp(cand - (a[0] + b[0])), 0.0), axis=0, keepdims=True)
                n1 = jnp.zeros_like(s1t)
                for j in range(PEER_TOPK):
                    n1 = n1 + jnp.where(s1t + b[j] > tau, 1.0, 0.0)
                n1_ref[h, :, ln] = n1
                e1_ref[h, :, ln] = jnp.exp(s1t - a[0]) / z
                r2_ref[h, :, ln] = rank2.astype(bf16)
                e2_ref[h, :, ln] = jnp.exp(s2t - b[0]).astype(bf16)
            return carry

        lax.fori_loop(0, PEER_HEADS, head, 0)

    gs = []
    for blk in range(PEER_EBLK):
        i1 = step * PEER_EBLK + blk
        ht = _dot(eu_ref[blk * nk:(blk + 1) * nk, :], ut_ref[...])
        w = jnp.zeros((nk, tm), bf16)
        for h in range(PEER_HEADS):
            n1 = n1_ref[h, pl.ds(i1, 1), :].astype(bf16)
            e1 = e1_ref[h, pl.ds(i1, 1), :].astype(bf16)
            w = w + jnp.where(r2_ref[h] < n1, e2_ref[h] * e1, jnp.zeros((), bf16))
        gelu = 0.5 * ht * (1.0 + lax.erf(ht * SQRT_HALF))
        gs.append(w * gelu.astype(bf16))
    acc_ref[...] += _dot(ev_ref[...], jnp.concatenate(gs, axis=0))

    @pl.when(step == nsteps - 1)
    def _finish():
        z = DEEPNORM_ALPHA * x_ref[0] + g2_ref[0] * acc_ref[...].T
        o_ref[0] = _ln(z) * lg_ref[...] + lb_ref[...]


def _peer_call(x, sh, sc, g2, wq_t, k1, k2, eu, ev_t, lg, lb):
    B, T, D = x.shape
    tm = PEER_TM
    nk = PEER_N_KEYS
    ne = eu.shape[0]
    eb = PEER_EBLK * nk
    nsteps = ne // eb
    kern = functools.partial(_peer_kernel, tm=tm, nsteps=nsteps)
    mod = pl.BlockSpec((1, 1, D), lambda b, i, s: (b, 0, 0))
    full = lambda a: pl.BlockSpec(a.shape, lambda b, i, s: (0,) * a.ndim)
    sel = lambda dt: pltpu.VMEM((PEER_HEADS, nk, tm), dt)
    return pl.pallas_call(
        kern,
        grid=(B, T // tm, nsteps),
        in_specs=[pl.BlockSpec((1, tm, D), lambda b, i, s: (b, i, 0)), mod, mod, mod,
                  full(wq_t), full(k1), full(k2),
                  pl.BlockSpec((eb, D), lambda b, i, s: (s, 0)),
                  pl.BlockSpec((D, eb), lambda b, i, s: (0, s)),
                  full(lg), full(lb)],
        out_specs=pl.BlockSpec((1, tm, D), lambda b, i, s: (b, i, 0)),
        out_shape=jax.ShapeDtypeStruct((B, T, D), f32),
        scratch_shapes=[pltpu.VMEM((D, tm), bf16), sel(f32), sel(f32), sel(bf16), sel(bf16),
                        pltpu.VMEM((D, tm), f32)],
        compiler_params=_cparams(("parallel", "parallel", "arbitrary")),
        name="peer_ffn",
    )(x, sh, sc, g2, wq_t, k1, k2, eu, ev_t, lg, lb)


def _gqa_head_perm():
    cols = []
    for c in range(GQA_R):
        cols += list(range(c * HEAD_DIM, (c + 1) * HEAD_DIM))
        cols += list(range((GQA_R + c) * HEAD_DIM, (GQA_R + c + 1) * HEAD_DIM))
    return np.asarray(cols)


def kernel(x, c, ctx, c_ctx, w_ada, b_ada, w_in, b_in, conv_w, conv_b, conv_ln_g, conv_ln_b, nat_rpb, gqa_sink,
           w_branch_a, w_branch_b, w_branch_c, w_branch_d, w_out, ln1_g, ln1_b, peer_wq, peer_k1, peer_k2,
           peer_u, peer_v, ln2_g, ln2_b):
    B, S, D = x.shape
    L = ctx.shape[1]
    gq0, gq1 = 1536, 2048
    perm = _gqa_head_perm()

    cvec = jnp.concatenate([c, c_ctx[None, :], jnp.zeros((8 - B - 1, D), f32)], axis=0)
    mod = _ada_call(cvec, w_ada, b_ada)
    cos, sin = _rope_tables(S)
    assert S == FFT_N * FFT_N
    ct_mats = _fft_ct_mats()
    cs_ctx = _dft_time_mats(L)
    bd = _dft_chan_mats()
    row = lambda v: v.reshape(1, -1)

    xc = ctx.reshape(1, B * L, D)
    for i in range(DEPTH):
        lat = [mod[i, :B, k * D:(k + 1) * D].reshape(B, 1, D) for k in range(6)]
        con = [mod[i, B:B + 1, k * D:(k + 1) * D].reshape(1, 1, D) for k in range(6)]
        wi, bi = w_in[i], b_in[i]
        w_small = jnp.concatenate([wi[:, :gq0], wi[:, gq0:gq1][:, perm], wi[:, gq1:N_SMALL]], axis=1).astype(bf16)
        b_small = row(jnp.concatenate([bi[:gq0], bi[gq0:gq1][perm], bi[gq1:N_SMALL]]))
        w_gate = wi[:, N_SMALL:].astype(bf16)
        b_gate = row(bi[N_SMALL:])
        wa, wb, wc = w_branch_a[i].astype(bf16), w_branch_b[i].astype(bf16), w_branch_c[i].astype(bf16)
        wd = w_branch_d[i][perm].astype(bf16)
        wo = w_out[i].astype(bf16)
        merge_w = (w_gate, b_gate, wa, wb, wc, wd, wo, row(ln1_g[i]), row(ln1_b[i]))
        peer_w = (peer_wq[i].T.astype(bf16), peer_k1[i].astype(bf16), peer_k2[i].astype(bf16),
                  peer_u[i].astype(bf16), peer_v[i].T.astype(bf16), row(ln2_g[i]), row(ln2_b[i]))
        conv_p = (conv_w[i], conv_b[i], conv_ln_g[i], conv_ln_b[i])

        pc, pcf = _inproj_call(xc, con[0], con[1], w_small, b_small, 512)
        pc, pcf = pc.reshape(B, L, N_SMALL), pcf.reshape(B, L, FNET_W)
        p, pf = _inproj_call(x, lat[0], lat[1], w_small, b_small, 512)
        y_a = _conv_call(p, *conv_p)
        y_b = _nat_call(p, pc, _nat_bias_tables(nat_rpb[i]))
        y_c = _fft_ct_call(pf, ct_mats)
        y_d = _gqa_call(p, pc, cos, sin, gqa_sink[i])
        x = _merge_call(x, lat[0], lat[1], lat[2], y_a, y_b, y_c, y_d, *merge_w, 256)
        x = _peer_call(x, lat[3], lat[4], lat[5], *peer_w)

        if i < DEPTH - 1:
            yc_a = _conv_call(pc, *conv_p)
            yc_b, yc_d = _ctx_attn_call(pc, gqa_sink[i])
            yc_c = _fft_call(pcf, cs_ctx, bd)
            flat = lambda a: a.reshape(1, B * L, a.shape[-1])
            xc = _merge_call(xc, con[0], con[1], con[2], flat(yc_a), flat(yc_b), flat(yc_c), flat(yc_d),
                             *merge_w, 256)
            xc = _peer_call(xc, con[3], con[4], con[5], *peer_w)
    return x
```

```python
import functools
import math

import numpy as np
import jax
import jax.numpy as jnp
from jax import lax
from jax.experimental import pallas as pl
from jax.experimental.pallas import tpu as pltpu

f32 = jnp.float32
bf16 = jnp.bfloat16

D_MODEL = 1024
DEPTH = 2
GRID_W = 64
CONV_CH = 256
CONV_WIDTH = 31
NAT_HEADS = 4
NAT_KH = 8
NAT_KW = 16
GQA_Q_HEADS = 8
GQA_KV_HEADS = 2
GQA_BLOCK = 128
HEAD_DIM = 64
ROPE_BASE = 10000.0
PEER_HEADS = 8
PEER_N_KEYS = 128
PEER_TOPK = 16
LN_EPS = 1e-5
NEG_INF = -1e30
DEEPNORM_ALPHA = (2 * DEPTH) ** 0.25

LANES = 128
SUBLANES = 8
N_SMALL = 2304
VMEM_LIMIT = 56 * 1024 * 1024


def _cparams(sem, vmem=VMEM_LIMIT):
    return pltpu.CompilerParams(dimension_semantics=sem, vmem_limit_bytes=vmem)


def _ln(x):
    mu = jnp.mean(x, axis=-1, keepdims=True)
    xc = x - mu
    var = jnp.mean(xc * xc, axis=-1, keepdims=True)
    return xc * lax.rsqrt(var + LN_EPS)


def _dot(a, b):
    return jnp.dot(a, b, preferred_element_type=f32)


def _dot_nt(a, b):
    return lax.dot_general(a, b, (((1,), (1,)), ((), ())), preferred_element_type=f32)


def _ada_kernel(c_ref, w_ref, b_ref, o_ref):
    c = c_ref[...]
    h = c * jax.nn.sigmoid(c)
    o_ref[0] = jnp.dot(h, w_ref[0], preferred_element_type=f32,
                       precision=lax.Precision.HIGHEST) + b_ref[0]


def _ada_call(cvec, w_ada, b_ada):
    L, D, N = w_ada.shape
    tn = 1536
    return pl.pallas_call(
        _ada_kernel,
        grid=(L, N // tn),
        in_specs=[pl.BlockSpec((8, D), lambda l, j: (0, 0)),
                  pl.BlockSpec((1, D, tn), lambda l, j: (l, 0, j)),
                  pl.BlockSpec((1, 1, tn), lambda l, j: (l, 0, j))],
        out_specs=pl.BlockSpec((1, 8, tn), lambda l, j: (l, 0, j)),
        out_shape=jax.ShapeDtypeStruct((L, 8, N), f32),
        compiler_params=_cparams(("parallel", "parallel")),
        name="ada_mod",
    )(cvec, w_ada, b_ada.reshape(L, 1, N))


F_OFF = 1280
FNET_W = 256


def _inproj_kernel(x_ref, sh_ref, sc_ref, w_ref, b_ref, o_ref, of_ref):
    xm = _ln(x_ref[0]) * (1.0 + sc_ref[0]) + sh_ref[0]
    y = (_dot(xm.astype(bf16), w_ref[...]) + b_ref[...]).astype(o_ref.dtype)
    o_ref[0] = y
    of_ref[0] = y[:, F_OFF:F_OFF + FNET_W]


def _inproj_call(x, sh, sc, w, b, tm):
    B, T, D = x.shape
    N = w.shape[1]
    return pl.pallas_call(
        _inproj_kernel,
        grid=(B, T // tm),
        in_specs=[pl.BlockSpec((1, tm, D), lambda b, i: (b, i, 0)),
                  pl.BlockSpec((1, 1, D), lambda b, i: (b, 0, 0)),
                  pl.BlockSpec((1, 1, D), lambda b, i: (b, 0, 0)),
                  pl.BlockSpec((D, N), lambda b, i: (0, 0)),
                  pl.BlockSpec((1, N), lambda b, i: (0, 0))],
        out_specs=[pl.BlockSpec((1, tm, N), lambda b, i: (b, i, 0)),
                   pl.BlockSpec((1, tm, FNET_W), lambda b, i: (b, i, 0))],
        out_shape=[jax.ShapeDtypeStruct((B, T, N), bf16), jax.ShapeDtypeStruct((B, T, FNET_W), bf16)],
        compiler_params=_cparams(("parallel", "parallel")),
        name="in_proj",
    )(x, sh, sc, w, b)


CONV_HALO = 16
CONV_SUB = 128


def _conv_kernel(prev_ref, cur_ref, next_ref, w_ref, cb_ref, g_ref, b_ref, o_ref, hs_ref, rot_ref, *, tc, nchunks):
    i = pl.program_id(1)

    def glu(v):
        v = v.astype(f32)
        return v[:, :CONV_CH] * jax.nn.sigmoid(v[:, CONV_CH:])

    hs_ref[0:CONV_HALO, :] = jnp.where(i > 0, glu(prev_ref[0]), 0.0)
    hs_ref[CONV_HALO:CONV_HALO + tc, :] = glu(cur_ref[0])
    hs_ref[CONV_HALO + tc:2 * CONV_HALO + tc, :] = jnp.where(i < nchunks - 1, glu(next_ref[0]), 0.0)
    base = CONV_HALO - CONV_WIDTH // 2
    span = tc + 2 * CONV_HALO - SUBLANES
    for r in range(1, SUBLANES):
        rot_ref[r] = hs_ref[pl.ds(r, span), :]
    for s in range(tc // CONV_SUB):
        acc = jnp.zeros((CONV_SUB, CONV_CH), f32)
        for j in range(CONV_WIDTH):
            off = base + j
            r, q = off % SUBLANES, off - off % SUBLANES
            src = hs_ref[pl.ds(s * CONV_SUB + q, CONV_SUB), :] if r == 0 else rot_ref[r, pl.ds(s * CONV_SUB + q, CONV_SUB), :]
            acc = acc + src * w_ref[j:j + 1, :]
        y = _ln(acc + cb_ref[...]) * g_ref[...] + b_ref[...]
        y = y * jax.nn.sigmoid(y)
        o_ref[0, s * CONV_SUB:(s + 1) * CONV_SUB, :] = y.astype(o_ref.dtype)


def _conv_call(p, conv_w, conv_b, ln_g, ln_b):
    B, T, _ = p.shape
    tc = min(512, T)
    nchunks = T // tc
    hb = tc // CONV_HALO
    nhb = T // CONV_HALO
    width = 2 * CONV_CH
    kern = functools.partial(_conv_kernel, tc=tc, nchunks=nchunks)
    vec = lambda v: v.reshape(1, CONV_CH)
    return pl.pallas_call(
        kern,
        grid=(B, nchunks),
        in_specs=[pl.BlockSpec((1, CONV_HALO, width), lambda b, i: (b, jnp.maximum(i * hb - 1, 0), 0)),
                  pl.BlockSpec((1, tc, width), lambda b, i: (b, i, 0)),
                  pl.BlockSpec((1, CONV_HALO, width), lambda b, i: (b, jnp.minimum((i + 1) * hb, nhb - 1), 0)),
                  pl.BlockSpec((CONV_WIDTH, CONV_CH), lambda b, i: (0, 0)),
                  pl.BlockSpec((1, CONV_CH), lambda b, i: (0, 0)),
                  pl.BlockSpec((1, CONV_CH), lambda b, i: (0, 0)),
                  pl.BlockSpec((1, CONV_CH), lambda b, i: (0, 0))],
        out_specs=pl.BlockSpec((1, tc, CONV_CH), lambda b, i: (b, i, 0)),
        out_shape=jax.ShapeDtypeStruct((B, T, CONV_CH), bf16),
        scratch_shapes=[pltpu.VMEM((tc + 2 * CONV_HALO, CONV_CH), f32),
                        pltpu.VMEM((SUBLANES, tc + 2 * CONV_HALO - SUBLANES, CONV_CH), f32)],
        compiler_params=_cparams(("parallel", "parallel")),
        name="conv_branch",
    )(p, p, p, conv_w, vec(conv_b), vec(ln_g), vec(ln_b))


FNET_GROUP_DIM = 64


def _fft_kernel(f_ref, cs_ref, bd_ref, o_ref, rhs_ref, *, T, scale):
    i = pl.program_id(0)
    b = pl.program_id(1)

    @pl.when(i == 0)
    def _():
        rows = min(512, T)
        for r in range(T // rows):
            z = _dot(f_ref[0, r * rows:(r + 1) * rows, :], bd_ref[...])
            rhs_ref[b, r * rows:(r + 1) * rows, :] = z[:, :FNET_W].astype(bf16)
            rhs_ref[b, T + r * rows:T + (r + 1) * rows, :] = (-z[:, FNET_W:]).astype(bf16)

    o_ref[0] = (_dot(cs_ref[...], rhs_ref[b]) * scale).astype(o_ref.dtype)


def _dft_time_mats(T):
    t = jnp.arange(T, dtype=jnp.int32)
    ang = ((t[:, None] * t[None, :]) % T).astype(f32) * f32(2.0 * math.pi / T)
    return jnp.concatenate([jnp.cos(ang), jnp.sin(ang)], axis=1).astype(bf16)


def _dft_chan_mats():
    n = FNET_GROUP_DIM
    k = np.arange(n)
    ang = 2.0 * np.pi * ((k[:, None] * k[None, :]) % n) / n
    eye = np.eye(FNET_W // n)
    return jnp.asarray(np.concatenate([np.kron(eye, np.cos(ang)), np.kron(eye, np.sin(ang))], axis=1), f32).astype(bf16)


def _fft_call(pf, cs, bd):
    B, T, _ = pf.shape
    p = pf
    tm = min(512, T)
    kern = functools.partial(_fft_kernel, T=T, scale=1.0 / math.sqrt(T * FNET_GROUP_DIM))
    return pl.pallas_call(
        kern,
        grid=(T // tm, B),
        in_specs=[pl.BlockSpec((1, T, FNET_W), lambda i, b: (b, 0, 0)),
                  pl.BlockSpec((tm, 2 * T), lambda i, b: (i, 0)),
                  pl.BlockSpec((FNET_W, 2 * FNET_W), lambda i, b: (0, 0))],
        out_specs=pl.BlockSpec((1, tm, FNET_W), lambda i, b: (b, i, 0)),
        out_shape=jax.ShapeDtypeStruct((B, T, FNET_W), bf16),
        scratch_shapes=[pltpu.VMEM((B, 2 * T, FNET_W), bf16)],
        compiler_params=_cparams(("arbitrary", "arbitrary")),
        name="fourier_branch",
    )(p, cs, bd)


FFT_N = 64
FFT_CHUNK = 16


def _fft_s1_kernel(x_ref, cs_ref, cd_ref, twr_ref, twi_ref, o_ref, scr_ref):
    n, w = FFT_N, FNET_W
    a = _dot(cs_ref[...], x_ref[0])
    for j in range(FFT_CHUNK):
        scr_ref[j * n:(j + 1) * n, 0:w] = a[0:n, j * w:(j + 1) * w].astype(bf16)
        scr_ref[j * n:(j + 1) * n, w:2 * w] = a[n:2 * n, j * w:(j + 1) * w].astype(bf16)
    z = _dot(scr_ref[...], cd_ref[...])
    twr = jnp.concatenate([twr_ref[...]] * (w // LANES), axis=1)
    twi = jnp.concatenate([twi_ref[...]] * (w // LANES), axis=1)
    zr, zi = z[:, :w], z[:, w:]
    o_ref[0, :, 0:w] = (zr * twr - zi * twi).astype(o_ref.dtype)
    o_ref[0, :, w:2 * w] = (zr * twi + zi * twr).astype(o_ref.dtype)


def _fft_s2_kernel(b_ref, cs_ref, o_ref, *, scale):
    n, w = FFT_N, FNET_W
    r = _dot(cs_ref[...], b_ref[0])
    for j in range(FFT_CHUNK):
        y = r[0:n, j * 2 * w:j * 2 * w + w] + r[n:2 * n, j * 2 * w + w:(j + 1) * 2 * w]
        o_ref[0, :, j * w:(j + 1) * w] = (y * scale).astype(o_ref.dtype)


def _fft_ct_mats():
    n, w = FFT_N, FNET_W
    k = np.arange(n)
    ang = 2.0 * np.pi * ((k[:, None] * k[None, :]) % n) / n
    c, s = np.cos(ang), np.sin(ang)
    g = FNET_GROUP_DIM
    kg = np.arange(g)
    ang_g = 2.0 * np.pi * ((kg[:, None] * kg[None, :]) % g) / g
    eye = np.eye(w // g)
    cbd, sbd = np.kron(eye, np.cos(ang_g)), np.kron(eye, np.sin(ang_g))
    cs1 = np.concatenate([c, -s], axis=0)
    cd = np.block([[cbd, -sbd], [sbd, cbd]])
    cs2 = np.concatenate([c, s], axis=0)
    t2 = np.arange(n)[:, None]
    k1 = np.arange(n)[None, :]
    tw = 2.0 * np.pi * ((t2 * k1) % (n * n)) / (n * n)
    twr = np.broadcast_to(np.cos(tw).reshape(n * n, 1), (n * n, LANES))
    twi = np.broadcast_to(-np.sin(tw).reshape(n * n, 1), (n * n, LANES))
    as_bf = lambda m: jnp.asarray(m, f32).astype(bf16)
    return as_bf(cs1), as_bf(cd), as_bf(cs2), jnp.asarray(twr, f32), jnp.asarray(twi, f32)


def _fft_ct_call(pf, mats):
    B, T, w = pf.shape
    n, ch = FFT_N, FFT_CHUNK
    cs1, cd, cs2, twr, twi = mats
    full = lambda a: pl.BlockSpec(a.shape, lambda b, i: (0,) * a.ndim)
    stage1 = pl.pallas_call(
        _fft_s1_kernel,
        grid=(B, n // ch),
        in_specs=[pl.BlockSpec((1, n, ch * w), lambda b, i: (b, 0, i)), full(cs1), full(cd),
                  pl.BlockSpec((ch * n, LANES), lambda b, i: (i, 0)),
                  pl.BlockSpec((ch * n, LANES), lambda b, i: (i, 0))],
        out_specs=pl.BlockSpec((1, ch * n, 2 * w), lambda b, i: (b, i, 0)),
        out_shape=jax.ShapeDtypeStruct((B, T, 2 * w), bf16),
        scratch_shapes=[pltpu.VMEM((ch * n, 2 * w), bf16)],
        compiler_params=_cparams(("parallel", "parallel")),
        name="fourier_stage1",
    )(pf.reshape(B, n, n * w), cs1, cd, twr, twi)
    kern2 = functools.partial(_fft_s2_kernel, scale=1.0 / math.sqrt(T * FNET_GROUP_DIM))
    out = pl.pallas_call(
        kern2,
        grid=(B, n // ch),
        in_specs=[pl.BlockSpec((1, n, ch * 2 * w), lambda b, i: (b, 0, i)), full(cs2)],
        out_specs=pl.BlockSpec((1, n, ch * w), lambda b, i: (b, 0, i)),
        out_shape=jax.ShapeDtypeStruct((B, n, n * w), bf16),
        compiler_params=_cparams(("parallel", "parallel")),
        name="fourier_stage2",
    )(stage1.reshape(B, n, n * 2 * w), cs2)
    return out.reshape(B, T, w)


def _lane_is_low(shape):
    return lax.broadcasted_iota(jnp.int32, shape, len(shape) - 1) < HEAD_DIM


def _softmax_pv(logits, values, extra=None):
    def fold(x, op, acc):
        for c in range(x.shape[1] // LANES):
            blk = x[:, c * LANES:(c + 1) * LANES]
            acc = blk if acc is None else op(acc, blk)
        return acc

    m_el = None
    for s in logits:
        m_el = fold(s, jnp.maximum, m_el)
    m = m_el.max(axis=-1, keepdims=True)
    if extra is not None:
        m = jnp.maximum(m, extra)
    d_el = None
    out = None
    for s, v in zip(logits, values):
        p = jnp.exp(s - m)
        d_el = fold(p, jnp.add, d_el)
        o = _dot(p.astype(bf16), v)
        out = o if out is None else out + o
    den = d_el.sum(axis=-1, keepdims=True)
    if extra is not None:
        den = den + jnp.exp(extra - m)
    return out * (1.0 / den)


NAT_QROWS = 8
NAT_BAND = 16
NAT_TQ = NAT_QROWS * GRID_W
NAT_TK = NAT_BAND * GRID_W
NQ_COL, NK_COL, NV_COL = 2, 3, 4


def _nat_kernel(q_ref, k_ref, v_ref, kc_ref, vc_ref, bias_ref, o_ref, *, nsteps):
    j = pl.program_id(1)
    start = jnp.clip(j * NAT_QROWS - NAT_KH // 2, 0, nsteps * NAT_QROWS - NAT_BAND) * GRID_W
    start = pl.multiple_of(start, GRID_W * 4)
    low = _lane_is_low((NAT_TQ, LANES))
    for c in range(NAT_HEADS // 2):
        cols = slice(c * LANES, (c + 1) * LANES)
        q2 = q_ref[0, :, cols] * 0.125
        k2 = k_ref[0, pl.ds(start, NAT_TK), cols]
        v2 = v_ref[0, pl.ds(start, NAT_TK), cols]
        kc2 = kc_ref[0, :, cols]
        vc2 = vc_ref[0, :, cols]
        halves = []
        for e in range(2):
            qm = jnp.where(low if e == 0 else jnp.logical_not(low), q2, jnp.zeros_like(q2))
            s = _dot_nt(qm, k2) + bias_ref[0, 2 * c + e]
            sc = _dot_nt(qm, kc2)
            halves.append(_softmax_pv([s, sc], [v2, vc2]))
        o_ref[0, :, cols] = jnp.where(low, halves[0], halves[1]).astype(o_ref.dtype)


def _nat_bias_tables(rpb):
    rows = 64
    nsteps = rows // NAT_QROWS
    a = np.arange(NAT_QROWS)[:, None]
    m = np.arange(NAT_BAND)[None, :]
    sel_r = np.zeros((3, NAT_QROWS, NAT_BAND, 2 * NAT_KH - 1), np.float32)
    ok_r = np.zeros((3, NAT_QROWS, NAT_BAND), bool)
    for v, jj in enumerate((1, 0, nsteps - 1)):
        band0 = int(np.clip(jj * NAT_QROWS - NAT_KH // 2, 0, rows - NAT_BAND))
        qr = jj * NAT_QROWS + a
        kr = band0 + m
        rstart = np.clip(qr - NAT_KH // 2, 0, rows - NAT_KH)
        ok = (kr >= rstart) & (kr < rstart + NAT_KH)
        dr = np.clip(kr - qr + NAT_KH - 1, 0, 2 * NAT_KH - 2)
        ok_r[v] = ok
        sel_r[v] = np.eye(2 * NAT_KH - 1, dtype=np.float32)[dr] * ok[..., None]
    qc = np.arange(GRID_W)[:, None]
    kc = np.arange(GRID_W)[None, :]
    cstart = np.clip(qc - NAT_KW // 2, 0, GRID_W - NAT_KW)
    ok_c = (kc >= cstart) & (kc < cstart + NAT_KW)
    dc = np.clip(kc - qc + NAT_KW - 1, 0, 2 * NAT_KW - 2)
    sel_c = np.eye(2 * NAT_KW - 1, dtype=np.float32)[dc] * ok_c[..., None]
    valid = ok_r[:, :, None, :, None] & ok_c[None, None, :, None, :]
    valid = valid.reshape(3, 1, NAT_TQ, NAT_TK)
    t = jnp.einsum('vamr,hrc,qkc->vhaqmk', jnp.asarray(sel_r), rpb.astype(f32), jnp.asarray(sel_c),
                   precision=lax.Precision.HIGHEST)
    t = t.reshape(3, NAT_HEADS, NAT_TQ, NAT_TK)
    return jnp.where(jnp.asarray(valid), t, NEG_INF)


def _nat_call(p, pc, bias):
    B, T, _ = p.shape
    L = pc.shape[1]
    W = NAT_HEADS * HEAD_DIM
    nsteps = T // NAT_TQ
    kern = functools.partial(_nat_kernel, nsteps=nsteps)
    var = lambda j: jnp.where(j == 0, 1, jnp.where(j == nsteps - 1, 2, 0))
    return pl.pallas_call(
        kern,
        grid=(B, nsteps),
        in_specs=[pl.BlockSpec((1, NAT_TQ, W), lambda b, j: (b, j, NQ_COL)),
                  pl.BlockSpec((1, T, W), lambda b, j: (b, 0, NK_COL)),
                  pl.BlockSpec((1, T, W), lambda b, j: (b, 0, NV_COL)),
                  pl.BlockSpec((1, L, W), lambda b, j: (b, 0, NK_COL)),
                  pl.BlockSpec((1, L, W), lambda b, j: (b, 0, NV_COL)),
                  pl.BlockSpec((1, NAT_HEADS, NAT_TQ, NAT_TK), lambda b, j: (var(j), 0, 0, 0))],
        out_specs=pl.BlockSpec((1, NAT_TQ, W), lambda b, j: (b, j, 0)),
        out_shape=jax.ShapeDtypeStruct((B, T, W), bf16),
        compiler_params=_cparams(("parallel", "arbitrary")),
        name="nat_attention",
    )(p, p, p, pc, pc, bias)


GQ_COL = 1536 // 512
GK_COL = 2048 // LANES
GV_COL = 2176 // LANES
GQA_R = GQA_Q_HEADS // GQA_KV_HEADS


def _rope(x, cos, sin):
    lane = lax.broadcasted_iota(jnp.int32, x.shape, 1)
    first = (lane % (HEAD_DIM // 2)) < (HEAD_DIM // 4)
    swapped = jnp.where(first, pltpu.roll(x, LANES - HEAD_DIM // 4, 1), pltpu.roll(x, HEAD_DIM // 4, 1))
    return x * cos + swapped * sin


def _rope_tables(S):
    t = np.arange(S)
    pos = np.stack([t // GRID_W, t % GRID_W], axis=1).astype(np.float64)
    quarter = HEAD_DIM // 4
    freqs = ROPE_BASE ** (-np.arange(quarter, dtype=np.float64) / quarter)
    lane = np.arange(LANES)
    which = (lane % HEAD_DIM) // (HEAD_DIM // 2)
    ang = pos[:, which] * freqs[lane % quarter][None, :]
    sign = np.where((lane % (HEAD_DIM // 2)) < quarter, -1.0, 1.0)[None, :]
    return jnp.asarray(np.cos(ang), f32), jnp.asarray(np.sin(ang) * sign, f32)


def _gqa_kernel(q_ref, kp_ref, kc_ref, kn_ref, vp_ref, vc_ref, vn_ref, kx_ref, vx_ref,
                cp_ref, cc_ref, cn_ref, sp_ref, sc_ref, sn_ref, sink_ref, band_ref, o_ref, *, nb):
    n = pl.program_id(1)
    blk = GQA_BLOCK
    rows = GQA_R * blk
    cos_q, sin_q = cc_ref[...], sc_ref[...]
    qs = [(_rope(q_ref[0, :, r * LANES:(r + 1) * LANES].astype(f32), cos_q, sin_q) * 0.125) for r in range(GQA_R)]
    q_all = jnp.concatenate(qs, axis=0)
    low = _lane_is_low((rows, LANES))
    kp = _rope(kp_ref[0].astype(f32), cp_ref[...], sp_ref[...]).astype(bf16)
    kc = _rope(kc_ref[0].astype(f32), cos_q, sin_q).astype(bf16)
    kn = _rope(kn_ref[0].astype(f32), cn_ref[...], sn_ref[...]).astype(bf16)
    bias_p = jnp.minimum(band_ref[0], jnp.where(n > 0, 0.0, NEG_INF))
    bias_n = jnp.minimum(band_ref[1], jnp.where(n < nb - 1, 0.0, NEG_INF))
    halves = []
    for g in range(GQA_KV_HEADS):
        qm = jnp.where(low if g == 0 else jnp.logical_not(low), q_all, 0.0).astype(bf16)
        logits = [_dot_nt(qm, kp) + bias_p, _dot_nt(qm, kc), _dot_nt(qm, kn) + bias_n, _dot_nt(qm, kx_ref[0])]
        vals = [vp_ref[0], vc_ref[0], vn_ref[0], vx_ref[0]]
        halves.append(_softmax_pv(logits, vals, extra=sink_ref[g][:, 0:1]))
    out = jnp.where(low, halves[0], halves[1])
    for r in range(GQA_R):
        o_ref[0, :, r * LANES:(r + 1) * LANES] = out[r * blk:(r + 1) * blk].astype(o_ref.dtype)


def _sink_cols(sink, blk):
    s = sink.astype(f32).reshape(GQA_KV_HEADS, GQA_R, 1, 1)
    return jnp.broadcast_to(s, (GQA_KV_HEADS, GQA_R, blk, LANES)).reshape(GQA_KV_HEADS, GQA_R * blk, LANES)


def _gqa_band_masks():
    qi = np.arange(GQA_R * GQA_BLOCK)[:, None] % GQA_BLOCK
    kj = np.arange(GQA_BLOCK)[None, :]
    return jnp.asarray(np.stack([np.where(kj >= qi, 0.0, NEG_INF), np.where(kj <= qi, 0.0, NEG_INF)]), f32)


def _gqa_call(p, pc, cos, sin, sink):
    B, T, _ = p.shape
    L = pc.shape[1]
    blk = GQA_BLOCK
    nb = T // blk
    QW = GQA_Q_HEADS * HEAD_DIM
    kern = functools.partial(_gqa_kernel, nb=nb)
    prev = lambda n: jnp.maximum(n - 1, 0)
    nxt = lambda n: jnp.minimum(n + 1, nb - 1)
    kv = lambda col, f: pl.BlockSpec((1, blk, LANES), lambda b, n: (b, f(n), col))
    tab = lambda f: pl.BlockSpec((blk, LANES), lambda b, n: (f(n), 0))
    same = lambda n: n
    return pl.pallas_call(
        kern,
        grid=(B, nb),
        in_specs=[pl.BlockSpec((1, blk, QW), lambda b, n: (b, n, GQ_COL)),
                  kv(GK_COL, prev), kv(GK_COL, same), kv(GK_COL, nxt),
                  kv(GV_COL, prev), kv(GV_COL, same), kv(GV_COL, nxt),
                  pl.BlockSpec((1, L, LANES), lambda b, n: (b, 0, GK_COL)),
                  pl.BlockSpec((1, L, LANES), lambda b, n: (b, 0, GV_COL)),
                  tab(prev), tab(same), tab(nxt), tab(prev), tab(same), tab(nxt),
                  pl.BlockSpec((GQA_KV_HEADS, GQA_R * blk, LANES), lambda b, n: (0, 0, 0)),
                  pl.BlockSpec((2, GQA_R * blk, blk), lambda b, n: (0, 0, 0))],
        out_specs=pl.BlockSpec((1, blk, QW), lambda b, n: (b, n, 0)),
        out_shape=jax.ShapeDtypeStruct((B, T, QW), bf16),
        compiler_params=_cparams(("parallel", "parallel")),
        name="window_gqa",
    )(p, p, p, p, p, p, p, pc, pc, cos, cos, cos, sin, sin, sin, _sink_cols(sink, blk), _gqa_band_masks())


def _ctx_attn_kernel(nq_ref, nk_ref, nv_ref, gq_ref, gk_ref, gv_ref, sink_ref, ob_ref, od_ref, *, L):
    low = _lane_is_low((L, LANES))
    for c in range(NAT_HEADS // 2):
        cols = slice(c * LANES, (c + 1) * LANES)
        q2 = nq_ref[0, :, cols] * 0.125
        halves = []
        for e in range(2):
            qm = jnp.where(low if e == 0 else jnp.logical_not(low), q2, jnp.zeros_like(q2))
            halves.append(_softmax_pv([_dot_nt(qm, nk_ref[0, :, cols])], [nv_ref[0, :, cols]]))
        ob_ref[0, :, cols] = jnp.where(low, halves[0], halves[1]).astype(ob_ref.dtype)
    q_all = jnp.concatenate([gq_ref[0, :, r * LANES:(r + 1) * LANES] for r in range(GQA_R)], axis=0) * 0.125
    low4 = _lane_is_low((GQA_R * L, LANES))
    halves = []
    for g in range(GQA_KV_HEADS):
        qm = jnp.where(low4 if g == 0 else jnp.logical_not(low4), q_all, jnp.zeros_like(q_all))
        halves.append(_softmax_pv([_dot_nt(qm, gk_ref[0])], [gv_ref[0]], extra=sink_ref[g][:, 0:1]))
    out = jnp.where(low4, halves[0], halves[1])
    for r in range(GQA_R):
        od_ref[0, :, r * LANES:(r + 1) * LANES] = out[r * L:(r + 1) * L].astype(od_ref.dtype)


def _ctx_attn_call(pc, sink):
    B, L, _ = pc.shape
    W = NAT_HEADS * HEAD_DIM
    QW = GQA_Q_HEADS * HEAD_DIM
    kern = functools.partial(_ctx_attn_kernel, L=L)
    return pl.pallas_call(
        kern,
        grid=(B,),
        in_specs=[pl.BlockSpec((1, L, W), lambda b: (b, 0, NQ_COL)),
                  pl.BlockSpec((1, L, W), lambda b: (b, 0, NK_COL)),
                  pl.BlockSpec((1, L, W), lambda b: (b, 0, NV_COL)),
                  pl.BlockSpec((1, L, QW), lambda b: (b, 0, GQ_COL)),
                  pl.BlockSpec((1, L, LANES), lambda b: (b, 0, GK_COL)),
                  pl.BlockSpec((1, L, LANES), lambda b: (b, 0, GV_COL)),
                  pl.BlockSpec((GQA_KV_HEADS, GQA_R * L, LANES), lambda b: (0, 0, 0))],
        out_specs=[pl.BlockSpec((1, L, W), lambda b: (b, 0, 0)),
                   pl.BlockSpec((1, L, QW), lambda b: (b, 0, 0))],
        out_shape=[jax.ShapeDtypeStruct((B, L, W), bf16), jax.ShapeDtypeStruct((B, L, QW), bf16)],
        compiler_params=_cparams(("parallel",)),
        name="context_attention",
    )(pc, pc, pc, pc, pc, pc, _sink_cols(sink, L))


def _merge_kernel(x_ref, sh_ref, sc_ref, g1_ref, ya_ref, yb_ref, yc_ref, yd_ref, wg_ref, bg_ref,
                  wa_ref, wb_ref, wc_ref, wd_ref, wo_ref, lg_ref, lb_ref, o_ref):
    D = D_MODEL
    x = x_ref[0]
    xm = (_ln(x) * (1.0 + sc_ref[0]) + sh_ref[0]).astype(bf16)
    m = None
    for j, (y_ref, w_ref) in enumerate(((ya_ref, wa_ref), (yb_ref, wb_ref), (yc_ref, wc_ref), (yd_ref, wd_ref))):
        gate = jax.nn.sigmoid(_dot(xm, wg_ref[:, j * D:(j + 1) * D]) + bg_ref[:, j * D:(j + 1) * D])
        t = gate * _dot(y_ref[0], w_ref[...])
        m = t if m is None else m + t
    mix = _dot(m.astype(bf16), wo_ref[...])
    z = DEEPNORM_ALPHA * x + g1_ref[0] * mix
    o_ref[0] = _ln(z) * lg_ref[...] + lb_ref[...]


def _merge_call(x, sh, sc, g1, ya, yb, yc, yd, wg, bg, wa, wb, wc, wd, wo, lg, lb, tm):
    B, T, D = x.shape
    tok = lambda w: pl.BlockSpec((1, tm, w), lambda b, i: (b, i, 0))
    mod = pl.BlockSpec((1, 1, D), lambda b, i: (b, 0, 0))
    full = lambda a: pl.BlockSpec(a.shape, lambda b, i: (0,) * a.ndim)
    return pl.pallas_call(
        _merge_kernel,
        grid=(B, T // tm),
        in_specs=[tok(D), mod, mod, mod, tok(ya.shape[-1]), tok(yb.shape[-1]), tok(yc.shape[-1]), tok(yd.shape[-1]),
                  full(wg), full(bg), full(wa), full(wb), full(wc), full(wd), full(wo), full(lg), full(lb)],
        out_specs=tok(D),
        out_shape=jax.ShapeDtypeStruct((B, T, D), f32),
        compiler_params=_cparams(("parallel", "parallel")),
        name="merge_ln",
    )(x, sh, sc, g1, ya, yb, yc, yd, wg, bg, wa, wb, wc, wd, wo, lg, lb)


PEER_TM = 512
PEER_EBLK = 8
PEER_DENSE_J = 4
BIG_NEG = -3.0e38
SQRT_HALF = 0.7071067811865476


def _top_values(s, count, rank_below=None):
    vals = []
    cur = s
    rank = None if rank_below is None else jnp.full(s.shape, float(rank_below), f32)
    for r in range(count):
        m = jnp.max(cur, axis=0, keepdims=True)
        vals.append(m)
        hit = cur == m
        if rank is not None and r < rank_below:
            rank = jnp.where(hit, float(r), rank)
        if r + 1 < count:
            cur = jnp.where(hit, BIG_NEG, cur)
    return vals, rank


def _peer_kernel(x_ref, sh_ref, sc_ref, g2_ref, wq_ref, k1_ref, k2_ref, eu_ref, ev_ref, lg_ref, lb_ref, o_ref,
                 ut_ref, n1_ref, e1_ref, r2_ref, e2_ref, acc_ref, *, tm, nsteps):
    step = pl.program_id(2)
    nk = PEER_N_KEYS

    @pl.when(step == 0)
    def _select():
        u = _ln(x_ref[0]) * (1.0 + sc_ref[0]) + sh_ref[0]
        ut_ref[...] = u.T.astype(bf16)
        acc_ref[...] = jnp.zeros_like(acc_ref)
        k = PEER_TOPK
        pairs = [(i, j) for i in range(k) for j in range(k) if (i + 1) * (j + 1) <= k]
        pad = (-len(pairs)) % 8
        top_ranks = [a for a in range(k) if (a + 1) * (PEER_DENSE_J + 1) <= k]

        def head(h, carry):
            r0 = pl.multiple_of(h * 2 * nk, 2 * nk)
            q1 = _dot(wq_ref[pl.ds(r0, nk), :], ut_ref[...]).astype(bf16)
            q2 = _dot(wq_ref[pl.ds(r0 + nk, nk), :], ut_ref[...]).astype(bf16)
            s1 = _dot(k1_ref[h], q1)
            s2 = _dot(k2_ref[h], q2)
            for t in range(tm // LANES):
                ln = slice(t * LANES, (t + 1) * LANES)
                s1t, s2t = s1[:, ln], s2[:, ln]
                a, _ = _top_values(s1t, k)
                b, rank2 = _top_values(s2t, k, rank_below=k)
                cand = jnp.concatenate([a[i] + b[j] for i, j in pairs]
                                       + [jnp.full((pad, LANES), BIG_NEG, f32)], axis=0)
                tau = _top_values(cand, k)[0][k - 1]
                z = jnp.sum(jnp.where(cand >= tau, jnp.exp(cand - (a[0] + b[0])), 0.0), axis=0, keepdims=True)
                n1 = jnp.zeros_like(s1t)
                for j in range(PEER_DENSE_J):
                    n1 = n1 + jnp.where(s1t + b[j] >= tau, 1.0, 0.0)
                for r in top_ranks:
                    extra = jnp.zeros_like(tau)
                    for j in range(PEER_DENSE_J, k // (r + 1)):
                        extra = extra + jnp.where(a[r] + b[j] >= tau, 1.0, 0.0)
                    n1 = n1 + jnp.where(s1t == a[r], extra, 0.0)
                n1_ref[h, :, ln] = n1
                e1_ref[h, :, ln] = jnp.exp(s1t - a[0]) / z
                r2_ref[h, :, ln] = rank2.astype(bf16)
                e2_ref[h, :, ln] = jnp.exp(s2t - b[0]).astype(bf16)
            return carry

        lax.fori_loop(0, PEER_HEADS, head, 0)

    gs = []
    for blk in range(PEER_EBLK):
        i1 = step * PEER_EBLK + blk
        ht = _dot(eu_ref[blk * nk:(blk + 1) * nk, :], ut_ref[...])
        w = jnp.zeros((nk, tm), bf16)
        for h in range(PEER_HEADS):
            n1 = n1_ref[h, pl.ds(i1, 1), :].astype(bf16)
            e1 = e1_ref[h, pl.ds(i1, 1), :].astype(bf16)
            w = w + jnp.where(r2_ref[h] < n1, e2_ref[h] * e1, jnp.zeros((), bf16))
        gelu = 0.5 * ht * (1.0 + lax.erf(ht * SQRT_HALF))
        gs.append(w * gelu.astype(bf16))
    acc_ref[...] += _dot(ev_ref[...], jnp.concatenate(gs, axis=0))

    @pl.when(step == nsteps - 1)
    def _finish():
        z = DEEPNORM_ALPHA * x_ref[0] + g2_ref[0] * acc_ref[...].T
        o_ref[0] = _ln(z) * lg_ref[...] + lb_ref[...]


def _peer_call(x, sh, sc, g2, wq_t, k1, k2, eu, ev_t, lg, lb):
    B, T, D = x.shape
    tm = PEER_TM
    nk = PEER_N_KEYS
    ne = eu.shape[0]
    eb = PEER_EBLK * nk
    nsteps = ne // eb
    kern = functools.partial(_peer_kernel, tm=tm, nsteps=nsteps)
    mod = pl.BlockSpec((1, 1, D), lambda b, i, s: (b, 0, 0))
    full = lambda a: pl.BlockSpec(a.shape, lambda b, i, s: (0,) * a.ndim)
    sel = lambda dt: pltpu.VMEM((PEER_HEADS, nk, tm), dt)
    return pl.pallas_call(
        kern,
        grid=(B, T // tm, nsteps),
        in_specs=[pl.BlockSpec((1, tm, D), lambda b, i, s: (b, i, 0)), mod, mod, mod,
                  full(wq_t), full(k1), full(k2),
                  pl.BlockSpec((eb, D), lambda b, i, s: (s, 0)),
                  pl.BlockSpec((D, eb), lambda b, i, s: (0, s)),
                  full(lg), full(lb)],
        out_specs=pl.BlockSpec((1, tm, D), lambda b, i, s: (b, i, 0)),
        out_shape=jax.ShapeDtypeStruct((B, T, D), f32),
        scratch_shapes=[pltpu.VMEM((D, tm), bf16), sel(f32), sel(f32), sel(bf16), sel(bf16),
                        pltpu.VMEM((D, tm), f32)],
        compiler_params=_cparams(("parallel", "parallel", "arbitrary")),
        name="peer_ffn",
    )(x, sh, sc, g2, wq_t, k1, k2, eu, ev_t, lg, lb)


def _gqa_head_perm():
    cols = []
    for c in range(GQA_R):
        cols += list(range(c * HEAD_DIM, (c + 1) * HEAD_DIM))
        cols += list(range((GQA_R + c) * HEAD_DIM, (GQA_R + c + 1) * HEAD_DIM))
    return np.asarray(cols)


def kernel(x, c, ctx, c_ctx, w_ada, b_ada, w_in, b_in, conv_w, conv_b, conv_ln_g, conv_ln_b, nat_rpb, gqa_sink,
           w_branch_a, w_branch_b, w_branch_c, w_branch_d, w_out, ln1_g, ln1_b, peer_wq, peer_k1, peer_k2,
           peer_u, peer_v, ln2_g, ln2_b):
    B, S, D = x.shape
    L = ctx.shape[1]
    gq0, gq1 = 1536, 2048
    perm = _gqa_head_perm()

    cvec = jnp.concatenate([c, c_ctx[None, :], jnp.zeros((8 - B - 1, D), f32)], axis=0)
    mod = _ada_call(cvec, w_ada, b_ada)
    cos, sin = _rope_tables(S)
    assert S == FFT_N * FFT_N
    ct_mats = _fft_ct_mats()
    cs_ctx = _dft_time_mats(L)
    bd = _dft_chan_mats()
    row = lambda v: v.reshape(1, -1)

    xc = ctx.reshape(1, B * L, D)
    for i in range(DEPTH):
        lat = [mod[i, :B, k * D:(k + 1) * D].reshape(B, 1, D) for k in range(6)]
        con = [mod[i, B:B + 1, k * D:(k + 1) * D].reshape(1, 1, D) for k in range(6)]
        wi, bi = w_in[i], b_in[i]
        w_small = jnp.concatenate([wi[:, :gq0], wi[:, gq0:gq1][:, perm], wi[:, gq1:N_SMALL]], axis=1).astype(bf16)
        b_small = row(jnp.concatenate([bi[:gq0], bi[gq0:gq1][perm], bi[gq1:N_SMALL]]))
        w_gate = wi[:, N_SMALL:].astype(bf16)
        b_gate = row(bi[N_SMALL:])
        wa, wb, wc = w_branch_a[i].astype(bf16), w_branch_b[i].astype(bf16), w_branch_c[i].astype(bf16)
        wd = w_branch_d[i][perm].astype(bf16)
        wo = w_out[i].astype(bf16)
        merge_w = (w_gate, b_gate, wa, wb, wc, wd, wo, row(ln1_g[i]), row(ln1_b[i]))
        peer_w = (peer_wq[i].T.astype(bf16), peer_k1[i].astype(bf16), peer_k2[i].astype(bf16),
                  peer_u[i].astype(bf16), peer_v[i].T.astype(bf16), row(ln2_g[i]), row(ln2_b[i]))
        conv_p = (conv_w[i], conv_b[i], conv_ln_g[i], conv_ln_b[i])

        pc, pcf = _inproj_call(xc, con[0], con[1], w_small, b_small, 512)
        pc, pcf = pc.reshape(B, L, N_SMALL), pcf.reshape(B, L, FNET_W)
        p, pf = _inproj_call(x, lat[0], lat[1], w_small, b_small, 512)
        y_a = _conv_call(p, *conv_p)
        y_b = _nat_call(p, pc, _nat_bias_tables(nat_rpb[i]))
        y_c = _fft_ct_call(pf, ct_mats)
        y_d = _gqa_call(p, pc, cos, sin, gqa_sink[i])
        x = _merge_call(x, lat[0], lat[1], lat[2], y_a, y_b, y_c, y_d, *merge_w, 256)
        x = _peer_call(x, lat[3], lat[4], lat[5], *peer_w)

        if i < DEPTH - 1:
            yc_a = _conv_call(pc, *conv_p)
            yc_b, yc_d = _ctx_attn_call(pc, gqa_sink[i])
            yc_c = _fft_call(pcf, cs_ctx, bd)
            flat = lambda a: a.reshape(1, B * L, a.shape[-1])
            xc = _merge_call(xc, con[0], con[1], con[2], flat(yc_a), flat(yc_b), flat(yc_c), flat(yc_d),
                             *merge_w, 256)
            xc = _peer_call(xc, con[3], con[4], con[5], *peer_w)
    return x
```

```python
import functools
import math

import numpy as np
import jax
import jax.numpy as jnp
from jax import lax
from jax.experimental import pallas as pl
from jax.experimental.pallas import tpu as pltpu

f32 = jnp.float32
bf16 = jnp.bfloat16

D_MODEL = 1024
DEPTH = 2
GRID_W = 64
CONV_CH = 256
CONV_WIDTH = 31
NAT_HEADS = 4
NAT_KH = 8
NAT_KW = 16
GQA_Q_HEADS = 8
GQA_KV_HEADS = 2
GQA_BLOCK = 128
HEAD_DIM = 64
ROPE_BASE = 10000.0
PEER_HEADS = 8
PEER_N_KEYS = 128
PEER_TOPK = 16
LN_EPS = 1e-5
NEG_INF = -1e30
DEEPNORM_ALPHA = (2 * DEPTH) ** 0.25

LANES = 128
SUBLANES = 8
N_SMALL = 2304
VMEM_LIMIT = 56 * 1024 * 1024


def _cparams(sem, vmem=VMEM_LIMIT, flags=None):
    return pltpu.CompilerParams(dimension_semantics=sem, vmem_limit_bytes=vmem, flags=flags)


def _ln(x):
    mu = jnp.mean(x, axis=-1, keepdims=True)
    xc = x - mu
    var = jnp.mean(xc * xc, axis=-1, keepdims=True)
    return xc * lax.rsqrt(var + LN_EPS)


def _dot(a, b):
    return jnp.dot(a, b, preferred_element_type=f32)


def _dot_nt(a, b):
    return lax.dot_general(a, b, (((1,), (1,)), ((), ())), preferred_element_type=f32)


def _ada_kernel(c_ref, w_ref, b_ref, o_ref):
    c = c_ref[...]
    h = c * jax.nn.sigmoid(c)
    o_ref[0] = jnp.dot(h, w_ref[0], preferred_element_type=f32,
                       precision=lax.Precision.HIGHEST) + b_ref[0]


def _ada_call(cvec, w_ada, b_ada):
    L, D, N = w_ada.shape
    tn = 1536
    return pl.pallas_call(
        _ada_kernel,
        grid=(L, N // tn),
        in_specs=[pl.BlockSpec((8, D), lambda l, j: (0, 0)),
                  pl.BlockSpec((1, D, tn), lambda l, j: (l, 0, j)),
                  pl.BlockSpec((1, 1, tn), lambda l, j: (l, 0, j))],
        out_specs=pl.BlockSpec((1, 8, tn), lambda l, j: (l, 0, j)),
        out_shape=jax.ShapeDtypeStruct((L, 8, N), f32),
        compiler_params=_cparams(("parallel", "parallel")),
        name="ada_mod",
    )(cvec, w_ada, b_ada.reshape(L, 1, N))


F_OFF = 1280
FNET_W = 256


def _inproj_kernel(x_ref, sh_ref, sc_ref, w_ref, b_ref, o_ref, of_ref):
    xm = _ln(x_ref[0]) * (1.0 + sc_ref[0]) + sh_ref[0]
    y = _dot(xm.astype(bf16), w_ref[...]) + b_ref[...]
    o_ref[0] = y.astype(o_ref.dtype)
    of_ref[0] = y[:, F_OFF:F_OFF + FNET_W]


def _inproj_call(x, sh, sc, w, b, tm):
    B, T, D = x.shape
    N = w.shape[1]
    return pl.pallas_call(
        _inproj_kernel,
        grid=(B, T // tm),
        in_specs=[pl.BlockSpec((1, tm, D), lambda b, i: (b, i, 0)),
                  pl.BlockSpec((1, 1, D), lambda b, i: (b, 0, 0)),
                  pl.BlockSpec((1, 1, D), lambda b, i: (b, 0, 0)),
                  pl.BlockSpec((D, N), lambda b, i: (0, 0)),
                  pl.BlockSpec((1, N), lambda b, i: (0, 0))],
        out_specs=[pl.BlockSpec((1, tm, N), lambda b, i: (b, i, 0)),
                   pl.BlockSpec((1, tm, FNET_W), lambda b, i: (b, i, 0))],
        out_shape=[jax.ShapeDtypeStruct((B, T, N), bf16), jax.ShapeDtypeStruct((B, T, FNET_W), f32)],
        compiler_params=_cparams(("parallel", "parallel")),
        name="in_proj",
    )(x, sh, sc, w, b)


CONV_HALO = 16
CONV_SUB = 128


def _conv_kernel(prev_ref, cur_ref, next_ref, w_ref, cb_ref, g_ref, b_ref, o_ref, hs_ref, rot_ref, *, tc, nchunks):
    i = pl.program_id(1)

    def glu(v):
        v = v.astype(f32)
        return v[:, :CONV_CH] * jax.nn.sigmoid(v[:, CONV_CH:])

    hs_ref[0:CONV_HALO, :] = jnp.where(i > 0, glu(prev_ref[0]), 0.0)
    hs_ref[CONV_HALO:CONV_HALO + tc, :] = glu(cur_ref[0])
    hs_ref[CONV_HALO + tc:2 * CONV_HALO + tc, :] = jnp.where(i < nchunks - 1, glu(next_ref[0]), 0.0)
    base = CONV_HALO - CONV_WIDTH // 2
    span = tc + 2 * CONV_HALO - SUBLANES
    for r in range(1, SUBLANES):
        rot_ref[r] = hs_ref[pl.ds(r, span), :]
    for s in range(tc // CONV_SUB):
        acc = jnp.zeros((CONV_SUB, CONV_CH), f32)
        for j in range(CONV_WIDTH):
            off = base + j
            r, q = off % SUBLANES, off - off % SUBLANES
            src = hs_ref[pl.ds(s * CONV_SUB + q, CONV_SUB), :] if r == 0 else rot_ref[r, pl.ds(s * CONV_SUB + q, CONV_SUB), :]
            acc = acc + src * w_ref[j:j + 1, :]
        y = _ln(acc + cb_ref[...]) * g_ref[...] + b_ref[...]
        y = y * jax.nn.sigmoid(y)
        o_ref[0, s * CONV_SUB:(s + 1) * CONV_SUB, :] = y.astype(o_ref.dtype)


def _conv_call(p, conv_w, conv_b, ln_g, ln_b):
    B, T, _ = p.shape
    tc = min(512, T)
    nchunks = T // tc
    hb = tc // CONV_HALO
    nhb = T // CONV_HALO
    width = 2 * CONV_CH
    kern = functools.partial(_conv_kernel, tc=tc, nchunks=nchunks)
    vec = lambda v: v.reshape(1, CONV_CH)
    return pl.pallas_call(
        kern,
        grid=(B, nchunks),
        in_specs=[pl.BlockSpec((1, CONV_HALO, width), lambda b, i: (b, jnp.maximum(i * hb - 1, 0), 0)),
                  pl.BlockSpec((1, tc, width), lambda b, i: (b, i, 0)),
                  pl.BlockSpec((1, CONV_HALO, width), lambda b, i: (b, jnp.minimum((i + 1) * hb, nhb - 1), 0)),
                  pl.BlockSpec((CONV_WIDTH, CONV_CH), lambda b, i: (0, 0)),
                  pl.BlockSpec((1, CONV_CH), lambda b, i: (0, 0)),
                  pl.BlockSpec((1, CONV_CH), lambda b, i: (0, 0)),
                  pl.BlockSpec((1, CONV_CH), lambda b, i: (0, 0))],
        out_specs=pl.BlockSpec((1, tc, CONV_CH), lambda b, i: (b, i, 0)),
        out_shape=jax.ShapeDtypeStruct((B, T, CONV_CH), bf16),
        scratch_shapes=[pltpu.VMEM((tc + 2 * CONV_HALO, CONV_CH), f32),
                        pltpu.VMEM((SUBLANES, tc + 2 * CONV_HALO - SUBLANES, CONV_CH), f32)],
        compiler_params=_cparams(("parallel", "parallel")),
        name="conv_branch",
    )(p, p, p, conv_w, vec(conv_b), vec(ln_g), vec(ln_b))


FNET_GROUP_DIM = 64


def _fft_kernel(f_ref, cs_ref, bd_ref, o_ref, rhs_ref, *, T, scale):
    i = pl.program_id(0)
    b = pl.program_id(1)

    @pl.when(i == 0)
    def _():
        rows = min(512, T)
        for r in range(T // rows):
            z = _dot(f_ref[0, r * rows:(r + 1) * rows, :].astype(bf16), bd_ref[...])
            rhs_ref[b, r * rows:(r + 1) * rows, :] = z[:, :FNET_W].astype(bf16)
            rhs_ref[b, T + r * rows:T + (r + 1) * rows, :] = (-z[:, FNET_W:]).astype(bf16)

    o_ref[0] = (_dot(cs_ref[...], rhs_ref[b]) * scale).astype(o_ref.dtype)


def _dft_time_mats(T):
    t = jnp.arange(T, dtype=jnp.int32)
    ang = ((t[:, None] * t[None, :]) % T).astype(f32) * f32(2.0 * math.pi / T)
    return jnp.concatenate([jnp.cos(ang), jnp.sin(ang)], axis=1).astype(bf16)


def _dft_chan_mats():
    n = FNET_GROUP_DIM
    k = np.arange(n)
    ang = 2.0 * np.pi * ((k[:, None] * k[None, :]) % n) / n
    eye = np.eye(FNET_W // n)
    return jnp.asarray(np.concatenate([np.kron(eye, np.cos(ang)), np.kron(eye, np.sin(ang))], axis=1), f32).astype(bf16)


def _fft_call(pf, cs, bd):
    B, T, _ = pf.shape
    p = pf
    tm = min(512, T)
    kern = functools.partial(_fft_kernel, T=T, scale=1.0 / math.sqrt(T * FNET_GROUP_DIM))
    return pl.pallas_call(
        kern,
        grid=(T // tm, B),
        in_specs=[pl.BlockSpec((1, T, FNET_W), lambda i, b: (b, 0, 0)),
                  pl.BlockSpec((tm, 2 * T), lambda i, b: (i, 0)),
                  pl.BlockSpec((FNET_W, 2 * FNET_W), lambda i, b: (0, 0))],
        out_specs=pl.BlockSpec((1, tm, FNET_W), lambda i, b: (b, i, 0)),
        out_shape=jax.ShapeDtypeStruct((B, T, FNET_W), bf16),
        scratch_shapes=[pltpu.VMEM((B, 2 * T, FNET_W), bf16)],
        compiler_params=_cparams(("arbitrary", "arbitrary")),
        name="fourier_branch",
    )(p, cs, bd)


FFT_N = 64
FFT_CHUNK = 16


def _fft_s1_kernel(x_ref, cs_ref, cd_ref, twr_ref, twi_ref, o_ref, scr_ref):
    n, w = FFT_N, FNET_W
    for j in range(FFT_CHUNK):
        a = _dot(cs_ref[...], x_ref[0, :, j, :].astype(bf16))
        scr_ref[j * n:(j + 1) * n, 0:w] = a[0:n].astype(bf16)
        scr_ref[j * n:(j + 1) * n, w:2 * w] = a[n:2 * n].astype(bf16)
    z = _dot(scr_ref[...], cd_ref[...])
    twr = jnp.concatenate([twr_ref[...]] * (w // LANES), axis=1)
    twi = jnp.concatenate([twi_ref[...]] * (w // LANES), axis=1)
    zr, zi = z[:, :w], z[:, w:]
    br = zr * twr - zi * twi
    bi = zr * twi + zi * twr
    for j in range(FFT_CHUNK):
        o_ref[0, j, :, 0:w] = br[j * n:(j + 1) * n]
        o_ref[0, j, :, w:2 * w] = bi[j * n:(j + 1) * n]


def _fft_s2_kernel(b_ref, cs_ref, o_ref, *, scale):
    n, w = FFT_N, FNET_W
    for j in range(FFT_CHUNK):
        r = _dot(cs_ref[...], b_ref[0, :, j, :].astype(bf16))
        o_ref[0, :, j, :] = (r[0:n, 0:w] + r[n:2 * n, w:2 * w]) * scale


def _fft_ct_mats():
    n, w = FFT_N, FNET_W
    k = np.arange(n)
    ang = 2.0 * np.pi * ((k[:, None] * k[None, :]) % n) / n
    c, s = np.cos(ang), np.sin(ang)
    g = FNET_GROUP_DIM
    kg = np.arange(g)
    ang_g = 2.0 * np.pi * ((kg[:, None] * kg[None, :]) % g) / g
    eye = np.eye(w // g)
    cbd, sbd = np.kron(eye, np.cos(ang_g)), np.kron(eye, np.sin(ang_g))
    cs1 = np.concatenate([c, -s], axis=0)
    cd = np.block([[cbd, -sbd], [sbd, cbd]])
    cs2 = np.concatenate([c, s], axis=0)
    t2 = np.arange(n)[:, None]
    k1 = np.arange(n)[None, :]
    tw = 2.0 * np.pi * ((t2 * k1) % (n * n)) / (n * n)
    twr = np.broadcast_to(np.cos(tw).reshape(n * n, 1), (n * n, LANES))
    twi = np.broadcast_to(-np.sin(tw).reshape(n * n, 1), (n * n, LANES))
    as_bf = lambda m: jnp.asarray(m, f32).astype(bf16)
    return as_bf(cs1), as_bf(cd), as_bf(cs2), jnp.asarray(twr, f32), jnp.asarray(twi, f32)


def _fft_ct_call(pf, mats):
    B, T, w = pf.shape
    n, ch = FFT_N, FFT_CHUNK
    cs1, cd, cs2, twr, twi = mats
    full = lambda a: pl.BlockSpec(a.shape, lambda b, i: (0,) * a.ndim)
    stage1 = pl.pallas_call(
        _fft_s1_kernel,
        grid=(B, n // ch),
        in_specs=[pl.BlockSpec((1, n, ch, w), lambda b, i: (b, 0, i, 0)), full(cs1), full(cd),
                  pl.BlockSpec((ch * n, LANES), lambda b, i: (i, 0)),
                  pl.BlockSpec((ch * n, LANES), lambda b, i: (i, 0))],
        out_specs=pl.BlockSpec((1, ch, n, 2 * w), lambda b, i: (b, i, 0, 0)),
        out_shape=jax.ShapeDtypeStruct((B, n, n, 2 * w), f32),
        scratch_shapes=[pltpu.VMEM((ch * n, 2 * w), bf16)],
        compiler_params=_cparams(("parallel", "parallel")),
        name="fourier_stage1",
    )(pf.reshape(B, n, n, w), cs1, cd, twr, twi)
    kern2 = functools.partial(_fft_s2_kernel, scale=1.0 / math.sqrt(T * FNET_GROUP_DIM))
    out = pl.pallas_call(
        kern2,
        grid=(B, n // ch),
        in_specs=[pl.BlockSpec((1, n, ch, 2 * w), lambda b, i: (b, 0, i, 0)), full(cs2)],
        out_specs=pl.BlockSpec((1, n, ch, w), lambda b, i: (b, 0, i, 0)),
        out_shape=jax.ShapeDtypeStruct((B, n, n, w), f32),
        compiler_params=_cparams(("parallel", "parallel")),
        name="fourier_stage2",
    )(stage1, cs2)
    return out.reshape(B, T, w)


def _lane_is_low(shape):
    return lax.broadcasted_iota(jnp.int32, shape, len(shape) - 1) < HEAD_DIM


def _softmax_pv(logits, values, extra=None):
    def fold(x, op, acc):
        for c in range(x.shape[1] // LANES):
            blk = x[:, c * LANES:(c + 1) * LANES]
            acc = blk if acc is None else op(acc, blk)
        return acc

    m_el = None
    for s in logits:
        m_el = fold(s, jnp.maximum, m_el)
    m = m_el.max(axis=-1, keepdims=True)
    if extra is not None:
        m = jnp.maximum(m, extra)
    d_el = None
    out = None
    for s, v in zip(logits, values):
        p = jnp.exp(s - m)
        d_el = fold(p, jnp.add, d_el)
        o = _dot(p.astype(bf16), v)
        out = o if out is None else out + o
    den = d_el.sum(axis=-1, keepdims=True)
    if extra is not None:
        den = den + jnp.exp(extra - m)
    return out * (1.0 / den)


NAT_QROWS = 4
NAT_GROUPS = 2
NAT_BAND = NAT_QROWS + NAT_KH
NAT_TQ = NAT_QROWS * GRID_W
NAT_TK = NAT_BAND * GRID_W
NAT_ROWS = 64
NAT_EDGE_GROUPS = -(-(NAT_KH // 2) // NAT_QROWS)
NQ_COL, NK_COL, NV_COL = 2, 3, 4


def _nat_band_start(group):
    return jnp.clip(group * NAT_QROWS - NAT_KH // 2, 0, NAT_ROWS - NAT_BAND)


def _nat_kernel(q_ref, k_ref, v_ref, kc_ref, vc_ref, bias_ref, o_ref):
    j = pl.program_id(1)
    ngroups = NAT_ROWS // NAT_QROWS
    low = _lane_is_low((NAT_TQ, LANES))
    for g in range(NAT_GROUPS):
        group = j * NAT_GROUPS + g
        start = pl.multiple_of(_nat_band_start(group) * GRID_W, NAT_QROWS * GRID_W)
        var = jnp.where(group < NAT_EDGE_GROUPS, group + 1,
                        jnp.where(group >= ngroups - NAT_EDGE_GROUPS,
                                  group - (ngroups - NAT_EDGE_GROUPS) + NAT_EDGE_GROUPS + 1, 0))
        rows = slice(g * NAT_TQ, (g + 1) * NAT_TQ)
        for c in range(NAT_HEADS // 2):
            cols = slice(c * LANES, (c + 1) * LANES)
            q2 = q_ref[0, rows, cols] * 0.125
            k2 = k_ref[0, pl.ds(start, NAT_TK), cols]
            v2 = v_ref[0, pl.ds(start, NAT_TK), cols]
            kc2 = kc_ref[0, :, cols]
            vc2 = vc_ref[0, :, cols]
            halves = []
            for e in range(2):
                qm = jnp.where(low if e == 0 else jnp.logical_not(low), q2, jnp.zeros_like(q2))
                s = _dot_nt(qm, k2) + bias_ref[var, 2 * c + e]
                sc = _dot_nt(qm, kc2)
                halves.append(_softmax_pv([s, sc], [v2, vc2]))
            o_ref[0, rows, cols] = jnp.where(low, halves[0], halves[1]).astype(o_ref.dtype)


def _nat_bias_tables(rpb):
    rows = NAT_ROWS
    ngroups = rows // NAT_QROWS
    reps = [NAT_EDGE_GROUPS] + list(range(NAT_EDGE_GROUPS)) + list(range(ngroups - NAT_EDGE_GROUPS, ngroups))
    nv = len(reps)
    a = np.arange(NAT_QROWS)[:, None]
    m = np.arange(NAT_BAND)[None, :]
    sel_r = np.zeros((nv, NAT_QROWS, NAT_BAND, 2 * NAT_KH - 1), np.float32)
    ok_r = np.zeros((nv, NAT_QROWS, NAT_BAND), bool)
    for v, jj in enumerate(reps):
        band0 = int(np.clip(jj * NAT_QROWS - NAT_KH // 2, 0, rows - NAT_BAND))
        qr = jj * NAT_QROWS + a
        kr = band0 + m
        rstart = np.clip(qr - NAT_KH // 2, 0, rows - NAT_KH)
        ok = (kr >= rstart) & (kr < rstart + NAT_KH)
        dr = np.clip(kr - qr + NAT_KH - 1, 0, 2 * NAT_KH - 2)
        ok_r[v] = ok
        sel_r[v] = np.eye(2 * NAT_KH - 1, dtype=np.float32)[dr] * ok[..., None]
    qc = np.arange(GRID_W)[:, None]
    kc = np.arange(GRID_W)[None, :]
    cstart = np.clip(qc - NAT_KW // 2, 0, GRID_W - NAT_KW)
    ok_c = (kc >= cstart) & (kc < cstart + NAT_KW)
    dc = np.clip(kc - qc + NAT_KW - 1, 0, 2 * NAT_KW - 2)
    sel_c = np.eye(2 * NAT_KW - 1, dtype=np.float32)[dc] * ok_c[..., None]
    valid = ok_r[:, :, None, :, None] & ok_c[None, None, :, None, :]
    valid = valid.reshape(nv, 1, NAT_TQ, NAT_TK)
    t = jnp.einsum('vamr,hrc,qkc->vhaqmk', jnp.asarray(sel_r), rpb.astype(f32), jnp.asarray(sel_c),
                   precision=lax.Precision.HIGHEST)
    t = t.reshape(nv, NAT_HEADS, NAT_TQ, NAT_TK)
    return jnp.where(jnp.asarray(valid), t, NEG_INF)


def _nat_call(p, pc, bias):
    B, T, _ = p.shape
    assert T == NAT_ROWS * GRID_W
    L = pc.shape[1]
    W = NAT_HEADS * HEAD_DIM
    tq = NAT_GROUPS * NAT_TQ
    return pl.pallas_call(
        _nat_kernel,
        grid=(B, T // tq),
        in_specs=[pl.BlockSpec((1, tq, W), lambda b, j: (b, j, NQ_COL)),
                  pl.BlockSpec((1, T, W), lambda b, j: (b, 0, NK_COL)),
                  pl.BlockSpec((1, T, W), lambda b, j: (b, 0, NV_COL)),
                  pl.BlockSpec((1, L, W), lambda b, j: (b, 0, NK_COL)),
                  pl.BlockSpec((1, L, W), lambda b, j: (b, 0, NV_COL)),
                  pl.BlockSpec(bias.shape, lambda b, j: (0, 0, 0, 0), pipeline_mode=pl.Buffered(1))],
        out_specs=pl.BlockSpec((1, tq, W), lambda b, j: (b, j, 0)),
        out_shape=jax.ShapeDtypeStruct((B, T, W), bf16),
        compiler_params=_cparams(("parallel", "arbitrary")),
        name="nat_attention",
    )(p, p, p, pc, pc, bias)


GQ_COL = 1536 // 512
GK_COL = 2048 // LANES
GV_COL = 2176 // LANES
GQA_R = GQA_Q_HEADS // GQA_KV_HEADS


def _rope(x, cos, sin):
    lane = lax.broadcasted_iota(jnp.int32, x.shape, 1)
    first = (lane % (HEAD_DIM // 2)) < (HEAD_DIM // 4)
    swapped = jnp.where(first, pltpu.roll(x, LANES - HEAD_DIM // 4, 1), pltpu.roll(x, HEAD_DIM // 4, 1))
    return x * cos + swapped * sin


def _rope_tables(S):
    t = np.arange(S)
    pos = np.stack([t // GRID_W, t % GRID_W], axis=1).astype(np.float64)
    quarter = HEAD_DIM // 4
    freqs = ROPE_BASE ** (-np.arange(quarter, dtype=np.float64) / quarter)
    lane = np.arange(LANES)
    which = (lane % HEAD_DIM) // (HEAD_DIM // 2)
    ang = pos[:, which] * freqs[lane % quarter][None, :]
    sign = np.where((lane % (HEAD_DIM // 2)) < quarter, -1.0, 1.0)[None, :]
    return jnp.asarray(np.cos(ang), f32), jnp.asarray(np.sin(ang) * sign, f32)


def _gqa_kernel(q_ref, kp_ref, kc_ref, kn_ref, vp_ref, vc_ref, vn_ref, kx_ref, vx_ref,
                cp_ref, cc_ref, cn_ref, sp_ref, sc_ref, sn_ref, sink_ref, band_ref, o_ref, *, nb):
    n = pl.program_id(1)
    blk = GQA_BLOCK
    rows = GQA_R * blk
    cos_q, sin_q = cc_ref[...], sc_ref[...]
    qs = [(_rope(q_ref[0, :, r * LANES:(r + 1) * LANES].astype(f32), cos_q, sin_q) * 0.125) for r in range(GQA_R)]
    q_all = jnp.concatenate(qs, axis=0)
    low = _lane_is_low((rows, LANES))
    kp = _rope(kp_ref[0].astype(f32), cp_ref[...], sp_ref[...]).astype(bf16)
    kc = _rope(kc_ref[0].astype(f32), cos_q, sin_q).astype(bf16)
    kn = _rope(kn_ref[0].astype(f32), cn_ref[...], sn_ref[...]).astype(bf16)
    bias_p = jnp.minimum(band_ref[0], jnp.where(n > 0, 0.0, NEG_INF))
    bias_n = jnp.minimum(band_ref[1], jnp.where(n < nb - 1, 0.0, NEG_INF))
    halves = []
    for g in range(GQA_KV_HEADS):
        qm = jnp.where(low if g == 0 else jnp.logical_not(low), q_all, 0.0).astype(bf16)
        logits = [_dot_nt(qm, kp) + bias_p, _dot_nt(qm, kc), _dot_nt(qm, kn) + bias_n, _dot_nt(qm, kx_ref[0])]
        vals = [vp_ref[0], vc_ref[0], vn_ref[0], vx_ref[0]]
        halves.append(_softmax_pv(logits, vals, extra=sink_ref[g][:, 0:1]))
    out = jnp.where(low, halves[0], halves[1])
    for r in range(GQA_R):
        o_ref[0, :, r * LANES:(r + 1) * LANES] = out[r * blk:(r + 1) * blk].astype(o_ref.dtype)


def _sink_cols(sink, blk):
    s = sink.astype(f32).reshape(GQA_KV_HEADS, GQA_R, 1, 1)
    return jnp.broadcast_to(s, (GQA_KV_HEADS, GQA_R, blk, LANES)).reshape(GQA_KV_HEADS, GQA_R * blk, LANES)


def _gqa_band_masks():
    qi = np.arange(GQA_R * GQA_BLOCK)[:, None] % GQA_BLOCK
    kj = np.arange(GQA_BLOCK)[None, :]
    return jnp.asarray(np.stack([np.where(kj >= qi, 0.0, NEG_INF), np.where(kj <= qi, 0.0, NEG_INF)]), f32)


def _gqa_call(p, pc, cos, sin, sink):
    B, T, _ = p.shape
    L = pc.shape[1]
    blk = GQA_BLOCK
    nb = T // blk
    QW = GQA_Q_HEADS * HEAD_DIM
    kern = functools.partial(_gqa_kernel, nb=nb)
    prev = lambda n: jnp.maximum(n - 1, 0)
    nxt = lambda n: jnp.minimum(n + 1, nb - 1)
    kv = lambda col, f: pl.BlockSpec((1, blk, LANES), lambda b, n: (b, f(n), col))
    tab = lambda f: pl.BlockSpec((blk, LANES), lambda b, n: (f(n), 0))
    same = lambda n: n
    return pl.pallas_call(
        kern,
        grid=(B, nb),
        in_specs=[pl.BlockSpec((1, blk, QW), lambda b, n: (b, n, GQ_COL)),
                  kv(GK_COL, prev), kv(GK_COL, same), kv(GK_COL, nxt),
                  kv(GV_COL, prev), kv(GV_COL, same), kv(GV_COL, nxt),
                  pl.BlockSpec((1, L, LANES), lambda b, n: (b, 0, GK_COL)),
                  pl.BlockSpec((1, L, LANES), lambda b, n: (b, 0, GV_COL)),
                  tab(prev), tab(same), tab(nxt), tab(prev), tab(same), tab(nxt),
                  pl.BlockSpec((GQA_KV_HEADS, GQA_R * blk, LANES), lambda b, n: (0, 0, 0)),
                  pl.BlockSpec((2, GQA_R * blk, blk), lambda b, n: (0, 0, 0))],
        out_specs=pl.BlockSpec((1, blk, QW), lambda b, n: (b, n, 0)),
        out_shape=jax.ShapeDtypeStruct((B, T, QW), bf16),
        compiler_params=_cparams(("parallel", "parallel")),
        name="window_gqa",
    )(p, p, p, p, p, p, p, pc, pc, cos, cos, cos, sin, sin, sin, _sink_cols(sink, blk), _gqa_band_masks())


def _ctx_attn_kernel(nq_ref, nk_ref, nv_ref, gq_ref, gk_ref, gv_ref, sink_ref, ob_ref, od_ref, *, L):
    low = _lane_is_low((L, LANES))
    for c in range(NAT_HEADS // 2):
        cols = slice(c * LANES, (c + 1) * LANES)
        q2 = nq_ref[0, :, cols] * 0.125
        halves = []
        for e in range(2):
            qm = jnp.where(low if e == 0 else jnp.logical_not(low), q2, jnp.zeros_like(q2))
            halves.append(_softmax_pv([_dot_nt(qm, nk_ref[0, :, cols])], [nv_ref[0, :, cols]]))
        ob_ref[0, :, cols] = jnp.where(low, halves[0], halves[1]).astype(ob_ref.dtype)
    q_all = jnp.concatenate([gq_ref[0, :, r * LANES:(r + 1) * LANES] for r in range(GQA_R)], axis=0) * 0.125
    low4 = _lane_is_low((GQA_R * L, LANES))
    halves = []
    for g in range(GQA_KV_HEADS):
        qm = jnp.where(low4 if g == 0 else jnp.logical_not(low4), q_all, jnp.zeros_like(q_all))
        halves.append(_softmax_pv([_dot_nt(qm, gk_ref[0])], [gv_ref[0]], extra=sink_ref[g][:, 0:1]))
    out = jnp.where(low4, halves[0], halves[1])
    for r in range(GQA_R):
        od_ref[0, :, r * LANES:(r + 1) * LANES] = out[r * L:(r + 1) * L].astype(od_ref.dtype)


def _ctx_attn_call(pc, sink):
    B, L, _ = pc.shape
    W = NAT_HEADS * HEAD_DIM
    QW = GQA_Q_HEADS * HEAD_DIM
    kern = functools.partial(_ctx_attn_kernel, L=L)
    return pl.pallas_call(
        kern,
        grid=(B,),
        in_specs=[pl.BlockSpec((1, L, W), lambda b: (b, 0, NQ_COL)),
                  pl.BlockSpec((1, L, W), lambda b: (b, 0, NK_COL)),
                  pl.BlockSpec((1, L, W), lambda b: (b, 0, NV_COL)),
                  pl.BlockSpec((1, L, QW), lambda b: (b, 0, GQ_COL)),
                  pl.BlockSpec((1, L, LANES), lambda b: (b, 0, GK_COL)),
                  pl.BlockSpec((1, L, LANES), lambda b: (b, 0, GV_COL)),
                  pl.BlockSpec((GQA_KV_HEADS, GQA_R * L, LANES), lambda b: (0, 0, 0))],
        out_specs=[pl.BlockSpec((1, L, W), lambda b: (b, 0, 0)),
                   pl.BlockSpec((1, L, QW), lambda b: (b, 0, 0))],
        out_shape=[jax.ShapeDtypeStruct((B, L, W), bf16), jax.ShapeDtypeStruct((B, L, QW), bf16)],
        compiler_params=_cparams(("parallel",)),
        name="context_attention",
    )(pc, pc, pc, pc, pc, pc, _sink_cols(sink, L))


def _merge_kernel(x_ref, sh_ref, sc_ref, g1_ref, ya_ref, yb_ref, yc_ref, yd_ref, wg_ref, bg_ref,
                  wa_ref, wb_ref, wc_ref, wd_ref, wo_ref, lg_ref, lb_ref, o_ref):
    D = D_MODEL
    x = x_ref[0]
    xm = (_ln(x) * (1.0 + sc_ref[0]) + sh_ref[0]).astype(bf16)
    m = None
    for j, (y_ref, w_ref) in enumerate(((ya_ref, wa_ref), (yb_ref, wb_ref), (yc_ref, wc_ref), (yd_ref, wd_ref))):
        gate = jax.nn.sigmoid(_dot(xm, wg_ref[:, j * D:(j + 1) * D]) + bg_ref[:, j * D:(j + 1) * D])
        t = gate * _dot(y_ref[0].astype(bf16), w_ref[...])
        m = t if m is None else m + t
    mix = _dot(m.astype(bf16), wo_ref[...])
    z = DEEPNORM_ALPHA * x + g1_ref[0] * mix
    o_ref[0] = _ln(z) * lg_ref[...] + lb_ref[...]


def _merge_call(x, sh, sc, g1, ya, yb, yc, yd, wg, bg, wa, wb, wc, wd, wo, lg, lb, tm):
    B, T, D = x.shape
    tok = lambda w: pl.BlockSpec((1, tm, w), lambda b, i: (b, i, 0))
    mod = pl.BlockSpec((1, 1, D), lambda b, i: (b, 0, 0))
    full = lambda a: pl.BlockSpec(a.shape, lambda b, i: (0,) * a.ndim)
    return pl.pallas_call(
        _merge_kernel,
        grid=(B, T // tm),
        in_specs=[tok(D), mod, mod, mod, tok(ya.shape[-1]), tok(yb.shape[-1]), tok(yc.shape[-1]), tok(yd.shape[-1]),
                  full(wg), full(bg), full(wa), full(wb), full(wc), full(wd), full(wo), full(lg), full(lb)],
        out_specs=tok(D),
        out_shape=jax.ShapeDtypeStruct((B, T, D), f32),
        compiler_params=_cparams(("parallel", "parallel")),
        name="merge_ln",
    )(x, sh, sc, g1, ya, yb, yc, yd, wg, bg, wa, wb, wc, wd, wo, lg, lb)


PEER_TM = 512
PEER_EBLK = 8
PEER_DENSE_J = 4
BIG_NEG = -3.0e38
SQRT_HALF = 0.7071067811865476


def _top_values(s, count, rank_below=None):
    vals = []
    cur = s
    rank = None if rank_below is None else jnp.full(s.shape, float(rank_below), f32)
    for r in range(count):
        m = jnp.max(cur, axis=0, keepdims=True)
        vals.append(m)
        hit = cur == m
        if rank is not None and r < rank_below:
            rank = jnp.where(hit, float(r), rank)
        if r + 1 < count:
            cur = jnp.where(hit, BIG_NEG, cur)
        yield
    return vals, rank


def _peer_select_chunk(h, s1t, s2t, ln, n1_ref, e1_ref, r2_ref, e2_ref):
    k = PEER_TOPK
    pairs = [(i, j) for i in range(k) for j in range(k) if (i + 1) * (j + 1) <= k]
    pad = (-len(pairs)) % SUBLANES
    top_ranks = [a for a in range(k) if (a + 1) * (PEER_DENSE_J + 1) <= k]
    a, _ = yield from _top_values(s1t, k)
    b, rank2 = yield from _top_values(s2t, k, rank_below=k)
    r2_ref[h, :, ln] = rank2.astype(bf16)
    e2_ref[h, :, ln] = jnp.exp(s2t - b[0]).astype(bf16)
    yield
    cand = jnp.concatenate([a[i] + b[j] for i, j in pairs]
                           + [jnp.full((pad, LANES), BIG_NEG, f32)], axis=0)
    top, _ = yield from _top_values(cand, k)
    tau = top[k - 1]
    z = jnp.sum(jnp.where(cand >= tau, jnp.exp(cand - (a[0] + b[0])), 0.0), axis=0, keepdims=True)
    yield
    n1 = jnp.zeros_like(s1t)
    for j in range(PEER_DENSE_J):
        n1 = n1 + jnp.where(s1t + b[j] >= tau, 1.0, 0.0)
        yield
    for r in top_ranks:
        extra = jnp.zeros_like(tau)
        for j in range(PEER_DENSE_J, k // (r + 1)):
            extra = extra + jnp.where(a[r] + b[j] >= tau, 1.0, 0.0)
        n1 = n1 + jnp.where(s1t == a[r], extra, 0.0)
        yield
    n1_ref[h, :, ln] = n1
    e1_ref[h, :, ln] = jnp.exp(s1t - a[0]) / z
    yield


def _peer_select_lanes(h, off, nlanes, ut, wq_ref, k1_ref, k2_ref, n1_ref, e1_ref, r2_ref, e2_ref):
    nk = PEER_N_KEYS
    r0 = pl.multiple_of(h * 2 * nk, 2 * nk)
    u_cols = ut[:, pl.ds(off, nlanes)]
    q1 = _dot(wq_ref[pl.ds(r0, nk), :], u_cols).astype(bf16)
    yield
    q2 = _dot(wq_ref[pl.ds(r0 + nk, nk), :], u_cols).astype(bf16)
    yield
    s1 = _dot(k1_ref[h], q1)
    s2 = _dot(k2_ref[h], q2)
    yield
    chunks = [_peer_select_chunk(h, s1[:, t * LANES:(t + 1) * LANES], s2[:, t * LANES:(t + 1) * LANES],
                                 pl.ds(off + t * LANES, LANES), n1_ref, e1_ref, r2_ref, e2_ref)
              for t in range(nlanes // LANES)]
    while chunks:
        for g in list(chunks):
            try:
                next(g)
                yield
            except StopIteration:
                chunks.remove(g)


def _advance(gen, steps=None):
    n = 0
    for _ in gen:
        n += 1
        if steps is not None and n >= steps:
            break


def _peer_kernel(x_ref, sh_ref, sc_ref, g2_ref, wq_ref, k1_ref, k2_ref, eu_ref, ev_ref, lg_ref, lb_ref,
                 o_ref, ut_ref, n1_ref, e1_ref, r2_ref, e2_ref, acc_ref, *, tm, nsteps):
    step = pl.program_id(2)
    nk = PEER_N_KEYS

    @pl.when(step == 0)
    def _select():
        u = _ln(x_ref[0]) * (1.0 + sc_ref[0]) + sh_ref[0]
        ut_ref[...] = u.T.astype(bf16)
        acc_ref[...] = jnp.zeros_like(acc_ref)

        def head(h, carry):
            _advance(_peer_select_lanes(h, 0, tm, ut_ref, wq_ref, k1_ref, k2_ref, n1_ref, e1_ref, r2_ref, e2_ref))
            return carry

        lax.fori_loop(0, PEER_HEADS, head, 0)

    gs = []
    for blk in range(PEER_EBLK):
        i1 = step * PEER_EBLK + blk
        ht = _dot(eu_ref[blk * nk:(blk + 1) * nk, :], ut_ref[...])
        w = jnp.zeros((nk, tm), bf16)
        for h in range(PEER_HEADS):
            n1 = n1_ref[h, pl.ds(i1, 1), :].astype(bf16)
            e1 = e1_ref[h, pl.ds(i1, 1), :].astype(bf16)
            w = w + jnp.where(r2_ref[h] < n1, e2_ref[h] * e1, jnp.zeros((), bf16))
        gelu = 0.5 * ht * (1.0 + lax.erf(ht * SQRT_HALF))
        gs.append(w * gelu.astype(bf16))
    acc_ref[...] += _dot(ev_ref[...], jnp.concatenate(gs, axis=0))

    @pl.when(step == nsteps - 1)
    def _finish():
        z = DEEPNORM_ALPHA * x_ref[0] + g2_ref[0] * acc_ref[...].T
        o_ref[0] = _ln(z) * lg_ref[...] + lb_ref[...]


def _peer_call(x, sh, sc, g2, wq_t, k1, k2, eu, ev_t, lg, lb):
    B, T, D = x.shape
    tm = PEER_TM
    nk = PEER_N_KEYS
    ne = eu.shape[0]
    eb = PEER_EBLK * nk
    nsteps = ne // eb
    kern = functools.partial(_peer_kernel, tm=tm, nsteps=nsteps)
    mod = pl.BlockSpec((1, 1, D), lambda b, i, s: (b, 0, 0))
    full = lambda a, **kw: pl.BlockSpec(a.shape, lambda b, i, s: (0,) * a.ndim, **kw)
    sel = lambda dt: pltpu.VMEM((PEER_HEADS, nk, tm), dt)
    return pl.pallas_call(
        kern,
        grid=(B, T // tm, nsteps),
        in_specs=[pl.BlockSpec((1, tm, D), lambda b, i, s: (b, i, 0)), mod, mod, mod,
                  full(wq_t, pipeline_mode=pl.Buffered(1)), full(k1), full(k2),
                  pl.BlockSpec((eb, D), lambda b, i, s: (s, 0)),
                  pl.BlockSpec((D, eb), lambda b, i, s: (0, s)),
                  full(lg), full(lb)],
        out_specs=pl.BlockSpec((1, tm, D), lambda b, i, s: (b, i, 0)),
        out_shape=jax.ShapeDtypeStruct((B, T, D), f32),
        scratch_shapes=[pltpu.VMEM((D, tm), bf16), sel(f32), sel(f32), sel(bf16), sel(bf16),
                        pltpu.VMEM((D, tm), f32)],
        compiler_params=_cparams(("parallel", "parallel", "arbitrary")),
        name="peer_ffn",
    )(x, sh, sc, g2, wq_t, k1, k2, eu, ev_t, lg, lb)


def _gqa_head_perm():
    cols = []
    for c in range(GQA_R):
        cols += list(range(c * HEAD_DIM, (c + 1) * HEAD_DIM))
        cols += list(range((GQA_R + c) * HEAD_DIM, (GQA_R + c + 1) * HEAD_DIM))
    return np.asarray(cols)


def kernel(x, c, ctx, c_ctx, w_ada, b_ada, w_in, b_in, conv_w, conv_b, conv_ln_g, conv_ln_b, nat_rpb, gqa_sink,
           w_branch_a, w_branch_b, w_branch_c, w_branch_d, w_out, ln1_g, ln1_b, peer_wq, peer_k1, peer_k2,
           peer_u, peer_v, ln2_g, ln2_b):
    B, S, D = x.shape
    L = ctx.shape[1]
    gq0, gq1 = 1536, 2048
    perm = _gqa_head_perm()

    cvec = jnp.concatenate([c, c_ctx[None, :], jnp.zeros((8 - B - 1, D), f32)], axis=0)
    mod = _ada_call(cvec, w_ada, b_ada)
    cos, sin = _rope_tables(S)
    assert S == FFT_N * FFT_N
    ct_mats = _fft_ct_mats()
    cs_ctx = _dft_time_mats(L)
    bd = _dft_chan_mats()
    row = lambda v: v.reshape(1, -1)

    xc = ctx.reshape(1, B * L, D)
    for i in range(DEPTH):
        lat = [mod[i, :B, k * D:(k + 1) * D].reshape(B, 1, D) for k in range(6)]
        con = [mod[i, B:B + 1, k * D:(k + 1) * D].reshape(1, 1, D) for k in range(6)]
        wi, bi = w_in[i], b_in[i]
        w_small = jnp.concatenate([wi[:, :gq0], wi[:, gq0:gq1][:, perm], wi[:, gq1:N_SMALL]], axis=1).astype(bf16)
        b_small = row(jnp.concatenate([bi[:gq0], bi[gq0:gq1][perm], bi[gq1:N_SMALL]]))
        w_gate = wi[:, N_SMALL:].astype(bf16)
        b_gate = row(bi[N_SMALL:])
        wa, wb, wc = w_branch_a[i].astype(bf16), w_branch_b[i].astype(bf16), w_branch_c[i].astype(bf16)
        wd = w_branch_d[i][perm].astype(bf16)
        wo = w_out[i].astype(bf16)
        merge_w = (w_gate, b_gate, wa, wb, wc, wd, wo, row(ln1_g[i]), row(ln1_b[i]))
        peer_w = (peer_wq[i].T.astype(bf16), peer_k1[i].astype(bf16), peer_k2[i].astype(bf16),
                  peer_u[i].astype(bf16), peer_v[i].T.astype(bf16), row(ln2_g[i]), row(ln2_b[i]))
        conv_p = (conv_w[i], conv_b[i], conv_ln_g[i], conv_ln_b[i])

        pc, pcf = _inproj_call(xc, con[0], con[1], w_small, b_small, 512)
        pc, pcf = pc.reshape(B, L, N_SMALL), pcf.reshape(B, L, FNET_W)
        p, pf = _inproj_call(x, lat[0], lat[1], w_small, b_small, 512)
        y_a = _conv_call(p, *conv_p)
        y_b = _nat_call(p, pc, _nat_bias_tables(nat_rpb[i]))
        y_c = _fft_ct_call(pf, ct_mats)
        y_d = _gqa_call(p, pc, cos, sin, gqa_sink[i])
        x = _merge_call(x, lat[0], lat[1], lat[2], y_a, y_b, y_c, y_d, *merge_w, 256)
        x = _peer_call(x, lat[3], lat[4], lat[5], *peer_w)

        if i < DEPTH - 1:
            yc_a = _conv_call(pc, *conv_p)
            yc_b, yc_d = _ctx_attn_call(pc, gqa_sink[i])
            yc_c = _fft_call(pcf, cs_ctx, bd)
            flat = lambda a: a.reshape(1, B * L, a.shape[-1])
            xc = _merge_call(xc, con[0], con[1], con[2], flat(yc_a), flat(yc_b), flat(yc_c), flat(yc_d),
                             *merge_w, 256)
            xc = _peer_call(xc, con[3], con[4], con[5], *peer_w)
    return x
```

```python
import functools
import math

import numpy as np
import jax
import jax.numpy as jnp
from jax import lax
from jax.experimental import pallas as pl
from jax.experimental.pallas import tpu as pltpu

f32 = jnp.float32
bf16 = jnp.bfloat16

D_MODEL = 1024
DEPTH = 2
GRID_W = 64
CONV_CH = 256
CONV_WIDTH = 31
NAT_HEADS = 4
NAT_KH = 8
NAT_KW = 16
GQA_Q_HEADS = 8
GQA_KV_HEADS = 2
GQA_BLOCK = 128
HEAD_DIM = 64
ROPE_BASE = 10000.0
PEER_HEADS = 8
PEER_N_KEYS = 128
PEER_TOPK = 16
LN_EPS = 1e-5
NEG_INF = -1e30
DEEPNORM_ALPHA = (2 * DEPTH) ** 0.25

LANES = 128
SUBLANES = 8
N_SMALL = 2304
VMEM_LIMIT = 56 * 1024 * 1024


def _cparams(sem, vmem=VMEM_LIMIT, flags=None):
    return pltpu.CompilerParams(dimension_semantics=sem, vmem_limit_bytes=vmem, flags=flags)


def _ln(x):
    mu = jnp.mean(x, axis=-1, keepdims=True)
    xc = x - mu
    var = jnp.mean(xc * xc, axis=-1, keepdims=True)
    return xc * lax.rsqrt(var + LN_EPS)


def _dot(a, b):
    return jnp.dot(a, b, preferred_element_type=f32)


def _dot_nt(a, b):
    return lax.dot_general(a, b, (((1,), (1,)), ((), ())), preferred_element_type=f32)


def _ada_kernel(c_ref, w_ref, b_ref, o_ref):
    c = c_ref[...]
    h = c * jax.nn.sigmoid(c)
    o_ref[0] = jnp.dot(h, w_ref[0], preferred_element_type=f32,
                       precision=lax.Precision.HIGHEST) + b_ref[0]


def _ada_call(cvec, w_ada, b_ada):
    L, D, N = w_ada.shape
    tn = 1536
    return pl.pallas_call(
        _ada_kernel,
        grid=(L, N // tn),
        in_specs=[pl.BlockSpec((8, D), lambda l, j: (0, 0)),
                  pl.BlockSpec((1, D, tn), lambda l, j: (l, 0, j)),
                  pl.BlockSpec((1, 1, tn), lambda l, j: (l, 0, j))],
        out_specs=pl.BlockSpec((1, 8, tn), lambda l, j: (l, 0, j)),
        out_shape=jax.ShapeDtypeStruct((L, 8, N), f32),
        compiler_params=_cparams(("parallel", "parallel")),
        name="ada_mod",
    )(cvec, w_ada, b_ada.reshape(L, 1, N))


F_OFF = 1280
FNET_W = 256


def _inproj_kernel(x_ref, sh_ref, sc_ref, w_ref, b_ref, o_ref, of_ref):
    xm = _ln(x_ref[0]) * (1.0 + sc_ref[0]) + sh_ref[0]
    y = _dot(xm.astype(bf16), w_ref[...]) + b_ref[...]
    o_ref[0] = y.astype(o_ref.dtype)
    of_ref[0] = y[:, F_OFF:F_OFF + FNET_W]


def _inproj_call(x, sh, sc, w, b, tm):
    B, T, D = x.shape
    N = w.shape[1]
    return pl.pallas_call(
        _inproj_kernel,
        grid=(B, T // tm),
        in_specs=[pl.BlockSpec((1, tm, D), lambda b, i: (b, i, 0)),
                  pl.BlockSpec((1, 1, D), lambda b, i: (b, 0, 0)),
                  pl.BlockSpec((1, 1, D), lambda b, i: (b, 0, 0)),
                  pl.BlockSpec((D, N), lambda b, i: (0, 0)),
                  pl.BlockSpec((1, N), lambda b, i: (0, 0))],
        out_specs=[pl.BlockSpec((1, tm, N), lambda b, i: (b, i, 0)),
                   pl.BlockSpec((1, tm, FNET_W), lambda b, i: (b, i, 0))],
        out_shape=[jax.ShapeDtypeStruct((B, T, N), bf16), jax.ShapeDtypeStruct((B, T, FNET_W), f32)],
        compiler_params=_cparams(("parallel", "parallel")),
        name="in_proj",
    )(x, sh, sc, w, b)


CONV_HALO = 16
CONV_SUB = 128


def _conv_kernel(prev_ref, cur_ref, next_ref, w_ref, cb_ref, g_ref, b_ref, o_ref, hs_ref, rot_ref, *, tc, nchunks):
    i = pl.program_id(1)

    def glu(v):
        v = v.astype(f32)
        return v[:, :CONV_CH] * jax.nn.sigmoid(v[:, CONV_CH:])

    hs_ref[0:CONV_HALO, :] = jnp.where(i > 0, glu(prev_ref[0]), 0.0)
    hs_ref[CONV_HALO:CONV_HALO + tc, :] = glu(cur_ref[0])
    hs_ref[CONV_HALO + tc:2 * CONV_HALO + tc, :] = jnp.where(i < nchunks - 1, glu(next_ref[0]), 0.0)
    base = CONV_HALO - CONV_WIDTH // 2
    span = tc + 2 * CONV_HALO - SUBLANES
    for r in range(1, SUBLANES):
        rot_ref[r] = hs_ref[pl.ds(r, span), :]
    for s in range(tc // CONV_SUB):
        acc = jnp.zeros((CONV_SUB, CONV_CH), f32)
        for j in range(CONV_WIDTH):
            off = base + j
            r, q = off % SUBLANES, off - off % SUBLANES
            src = hs_ref[pl.ds(s * CONV_SUB + q, CONV_SUB), :] if r == 0 else rot_ref[r, pl.ds(s * CONV_SUB + q, CONV_SUB), :]
            acc = acc + src * w_ref[j:j + 1, :]
        y = _ln(acc + cb_ref[...]) * g_ref[...] + b_ref[...]
        y = y * jax.nn.sigmoid(y)
        o_ref[0, s * CONV_SUB:(s + 1) * CONV_SUB, :] = y.astype(o_ref.dtype)


def _conv_call(p, conv_w, conv_b, ln_g, ln_b):
    B, T, _ = p.shape
    tc = min(512, T)
    nchunks = T // tc
    hb = tc // CONV_HALO
    nhb = T // CONV_HALO
    width = 2 * CONV_CH
    kern = functools.partial(_conv_kernel, tc=tc, nchunks=nchunks)
    vec = lambda v: v.reshape(1, CONV_CH)
    return pl.pallas_call(
        kern,
        grid=(B, nchunks),
        in_specs=[pl.BlockSpec((1, CONV_HALO, width), lambda b, i: (b, jnp.maximum(i * hb - 1, 0), 0)),
                  pl.BlockSpec((1, tc, width), lambda b, i: (b, i, 0)),
                  pl.BlockSpec((1, CONV_HALO, width), lambda b, i: (b, jnp.minimum((i + 1) * hb, nhb - 1), 0)),
                  pl.BlockSpec((CONV_WIDTH, CONV_CH), lambda b, i: (0, 0)),
                  pl.BlockSpec((1, CONV_CH), lambda b, i: (0, 0)),
                  pl.BlockSpec((1, CONV_CH), lambda b, i: (0, 0)),
                  pl.BlockSpec((1, CONV_CH), lambda b, i: (0, 0))],
        out_specs=pl.BlockSpec((1, tc, CONV_CH), lambda b, i: (b, i, 0)),
        out_shape=jax.ShapeDtypeStruct((B, T, CONV_CH), bf16),
        scratch_shapes=[pltpu.VMEM((tc + 2 * CONV_HALO, CONV_CH), f32),
                        pltpu.VMEM((SUBLANES, tc + 2 * CONV_HALO - SUBLANES, CONV_CH), f32)],
        compiler_params=_cparams(("parallel", "parallel")),
        name="conv_branch",
    )(p, p, p, conv_w, vec(conv_b), vec(ln_g), vec(ln_b))


FNET_GROUP_DIM = 64


def _fft_kernel(f_ref, cs_ref, bd_ref, o_ref, rhs_ref, *, T, scale):
    i = pl.program_id(0)
    b = pl.program_id(1)

    @pl.when(i == 0)
    def _():
        rows = min(512, T)
        for r in range(T // rows):
            z = _dot(f_ref[0, r * rows:(r + 1) * rows, :].astype(bf16), bd_ref[...])
            rhs_ref[b, r * rows:(r + 1) * rows, :] = z[:, :FNET_W].astype(bf16)
            rhs_ref[b, T + r * rows:T + (r + 1) * rows, :] = (-z[:, FNET_W:]).astype(bf16)

    o_ref[0] = (_dot(cs_ref[...], rhs_ref[b]) * scale).astype(o_ref.dtype)


def _dft_time_mats(T):
    t = jnp.arange(T, dtype=jnp.int32)
    ang = ((t[:, None] * t[None, :]) % T).astype(f32) * f32(2.0 * math.pi / T)
    return jnp.concatenate([jnp.cos(ang), jnp.sin(ang)], axis=1).astype(bf16)


def _dft_chan_mats():
    n = FNET_GROUP_DIM
    k = np.arange(n)
    ang = 2.0 * np.pi * ((k[:, None] * k[None, :]) % n) / n
    eye = np.eye(FNET_W // n)
    return jnp.asarray(np.concatenate([np.kron(eye, np.cos(ang)), np.kron(eye, np.sin(ang))], axis=1), f32).astype(bf16)


def _fft_call(pf, cs, bd):
    B, T, _ = pf.shape
    p = pf
    tm = min(512, T)
    kern = functools.partial(_fft_kernel, T=T, scale=1.0 / math.sqrt(T * FNET_GROUP_DIM))
    return pl.pallas_call(
        kern,
        grid=(T // tm, B),
        in_specs=[pl.BlockSpec((1, T, FNET_W), lambda i, b: (b, 0, 0)),
                  pl.BlockSpec((tm, 2 * T), lambda i, b: (i, 0)),
                  pl.BlockSpec((FNET_W, 2 * FNET_W), lambda i, b: (0, 0))],
        out_specs=pl.BlockSpec((1, tm, FNET_W), lambda i, b: (b, i, 0)),
        out_shape=jax.ShapeDtypeStruct((B, T, FNET_W), bf16),
        scratch_shapes=[pltpu.VMEM((B, 2 * T, FNET_W), bf16)],
        compiler_params=_cparams(("arbitrary", "arbitrary")),
        name="fourier_branch",
    )(p, cs, bd)


FFT_N = 64
FFT_CHUNK = 16


def _fft_s1_kernel(x_ref, cs_ref, cd_ref, twr_ref, twi_ref, o_ref, scr_ref):
    n, w = FFT_N, FNET_W
    for j in range(FFT_CHUNK):
        a = _dot(cs_ref[...], x_ref[0, :, j, :].astype(bf16))
        scr_ref[j * n:(j + 1) * n, 0:w] = a[0:n].astype(bf16)
        scr_ref[j * n:(j + 1) * n, w:2 * w] = a[n:2 * n].astype(bf16)
    z = _dot(scr_ref[...], cd_ref[...])
    twr = jnp.concatenate([twr_ref[...]] * (w // LANES), axis=1)
    twi = jnp.concatenate([twi_ref[...]] * (w // LANES), axis=1)
    zr, zi = z[:, :w], z[:, w:]
    br = zr * twr - zi * twi
    bi = zr * twi + zi * twr
    for j in range(FFT_CHUNK):
        o_ref[0, j, :, 0:w] = br[j * n:(j + 1) * n]
        o_ref[0, j, :, w:2 * w] = bi[j * n:(j + 1) * n]


def _fft_s2_kernel(b_ref, cs_ref, o_ref, *, scale):
    n, w = FFT_N, FNET_W
    for j in range(FFT_CHUNK):
        r = _dot(cs_ref[...], b_ref[0, :, j, :].astype(bf16))
        o_ref[0, :, j, :] = (r[0:n, 0:w] + r[n:2 * n, w:2 * w]) * scale


def _fft_ct_mats():
    n, w = FFT_N, FNET_W
    k = np.arange(n)
    ang = 2.0 * np.pi * ((k[:, None] * k[None, :]) % n) / n
    c, s = np.cos(ang), np.sin(ang)
    g = FNET_GROUP_DIM
    kg = np.arange(g)
    ang_g = 2.0 * np.pi * ((kg[:, None] * kg[None, :]) % g) / g
    eye = np.eye(w // g)
    cbd, sbd = np.kron(eye, np.cos(ang_g)), np.kron(eye, np.sin(ang_g))
    cs1 = np.concatenate([c, -s], axis=0)
    cd = np.block([[cbd, -sbd], [sbd, cbd]])
    cs2 = np.concatenate([c, s], axis=0)
    t2 = np.arange(n)[:, None]
    k1 = np.arange(n)[None, :]
    tw = 2.0 * np.pi * ((t2 * k1) % (n * n)) / (n * n)
    twr = np.broadcast_to(np.cos(tw).reshape(n * n, 1), (n * n, LANES))
    twi = np.broadcast_to(-np.sin(tw).reshape(n * n, 1), (n * n, LANES))
    as_bf = lambda m: jnp.asarray(m, f32).astype(bf16)
    return as_bf(cs1), as_bf(cd), as_bf(cs2), jnp.asarray(twr, f32), jnp.asarray(twi, f32)


def _fft_ct_call(pf, mats):
    B, T, w = pf.shape
    n, ch = FFT_N, FFT_CHUNK
    cs1, cd, cs2, twr, twi = mats
    full = lambda a: pl.BlockSpec(a.shape, lambda b, i: (0,) * a.ndim)
    stage1 = pl.pallas_call(
        _fft_s1_kernel,
        grid=(B, n // ch),
        in_specs=[pl.BlockSpec((1, n, ch, w), lambda b, i: (b, 0, i, 0)), full(cs1), full(cd),
                  pl.BlockSpec((ch * n, LANES), lambda b, i: (i, 0)),
                  pl.BlockSpec((ch * n, LANES), lambda b, i: (i, 0))],
        out_specs=pl.BlockSpec((1, ch, n, 2 * w), lambda b, i: (b, i, 0, 0)),
        out_shape=jax.ShapeDtypeStruct((B, n, n, 2 * w), f32),
        scratch_shapes=[pltpu.VMEM((ch * n, 2 * w), bf16)],
        compiler_params=_cparams(("parallel", "parallel")),
        name="fourier_stage1",
    )(pf.reshape(B, n, n, w), cs1, cd, twr, twi)
    kern2 = functools.partial(_fft_s2_kernel, scale=1.0 / math.sqrt(T * FNET_GROUP_DIM))
    out = pl.pallas_call(
        kern2,
        grid=(B, n // ch),
        in_specs=[pl.BlockSpec((1, n, ch, 2 * w), lambda b, i: (b, 0, i, 0)), full(cs2)],
        out_specs=pl.BlockSpec((1, n, ch, w), lambda b, i: (b, 0, i, 0)),
        out_shape=jax.ShapeDtypeStruct((B, n, n, w), f32),
        compiler_params=_cparams(("parallel", "parallel")),
        name="fourier_stage2",
    )(stage1, cs2)
    return out.reshape(B, T, w)


def _lane_is_low(shape):
    return lax.broadcasted_iota(jnp.int32, shape, len(shape) - 1) < HEAD_DIM


def _softmax_pv_steps(make_logits, values, extra=None):
    def fold(x, op, acc):
        for c in range(x.shape[1] // LANES):
            blk = x[:, c * LANES:(c + 1) * LANES]
            acc = blk if acc is None else op(acc, blk)
        return acc

    logits = make_logits()
    yield
    m_el = None
    for s in logits:
        m_el = fold(s, jnp.maximum, m_el)
    m = m_el.max(axis=-1, keepdims=True)
    if extra is not None:
        m = jnp.maximum(m, extra)
    yield
    d_el = None
    out = None
    for s, v in zip(logits, values):
        p = jnp.exp(s - m)
        d_el = fold(p, jnp.add, d_el)
        o = _dot(p.astype(bf16), v)
        out = o if out is None else out + o
        yield
    den = d_el.sum(axis=-1, keepdims=True)
    if extra is not None:
        den = den + jnp.exp(extra - m)
    return out * (1.0 / den)


def _run_together(gens):
    results = [None] * len(gens)
    live = list(enumerate(gens))
    while live:
        for item in list(live):
            i, g = item
            try:
                next(g)
            except StopIteration as stop:
                results[i] = stop.value
                live.remove(item)
    return results


def _softmax_pv(logits, values, extra=None):
    return _run_together([_softmax_pv_steps(lambda: logits, values, extra)])[0]


NAT_QROWS = 4
NAT_GROUPS = 2
NAT_BAND = NAT_QROWS + NAT_KH
NAT_TQ = NAT_QROWS * GRID_W
NAT_TK = NAT_BAND * GRID_W
NAT_ROWS = 64
NAT_EDGE_GROUPS = -(-(NAT_KH // 2) // NAT_QROWS)
NQ_COL, NK_COL, NV_COL = 2, 3, 4


def _nat_band_start(group):
    return jnp.clip(group * NAT_QROWS - NAT_KH // 2, 0, NAT_ROWS - NAT_BAND)


def _nat_kernel(q_ref, k_ref, v_ref, kc_ref, vc_ref, bias_ref, o_ref):
    j = pl.program_id(1)
    ngroups = NAT_ROWS // NAT_QROWS
    low = _lane_is_low((NAT_TQ, LANES))
    for g in range(NAT_GROUPS):
        group = j * NAT_GROUPS + g
        start = pl.multiple_of(_nat_band_start(group) * GRID_W, NAT_QROWS * GRID_W)
        var = jnp.where(group < NAT_EDGE_GROUPS, group + 1,
                        jnp.where(group >= ngroups - NAT_EDGE_GROUPS,
                                  group - (ngroups - NAT_EDGE_GROUPS) + NAT_EDGE_GROUPS + 1, 0))
        rows = slice(g * NAT_TQ, (g + 1) * NAT_TQ)
        heads = []
        for c in range(NAT_HEADS // 2):
            cols = slice(c * LANES, (c + 1) * LANES)
            q2 = q_ref[0, rows, cols] * 0.125
            k2 = k_ref[0, pl.ds(start, NAT_TK), cols]
            v2 = v_ref[0, pl.ds(start, NAT_TK), cols]
            kc2 = kc_ref[0, :, cols]
            vc2 = vc_ref[0, :, cols]

            def logits_of(e, q2=q2, k2=k2, kc2=kc2, c=c):
                qm = jnp.where(low if e == 0 else jnp.logical_not(low), q2, jnp.zeros_like(q2))
                return [_dot_nt(qm, k2) + bias_ref[var, 2 * c + e], _dot_nt(qm, kc2)]

            heads += [_softmax_pv_steps(functools.partial(logits_of, e), [v2, vc2]) for e in range(2)]
        outs = _run_together(heads)
        for c in range(NAT_HEADS // 2):
            cols = slice(c * LANES, (c + 1) * LANES)
            o_ref[0, rows, cols] = jnp.where(low, outs[2 * c], outs[2 * c + 1]).astype(o_ref.dtype)


def _nat_bias_tables(rpb):
    rows = NAT_ROWS
    ngroups = rows // NAT_QROWS
    reps = [NAT_EDGE_GROUPS] + list(range(NAT_EDGE_GROUPS)) + list(range(ngroups - NAT_EDGE_GROUPS, ngroups))
    nv = len(reps)
    a = np.arange(NAT_QROWS)[:, None]
    m = np.arange(NAT_BAND)[None, :]
    sel_r = np.zeros((nv, NAT_QROWS, NAT_BAND, 2 * NAT_KH - 1), np.float32)
    ok_r = np.zeros((nv, NAT_QROWS, NAT_BAND), bool)
    for v, jj in enumerate(reps):
        band0 = int(np.clip(jj * NAT_QROWS - NAT_KH // 2, 0, rows - NAT_BAND))
        qr = jj * NAT_QROWS + a
        kr = band0 + m
        rstart = np.clip(qr - NAT_KH // 2, 0, rows - NAT_KH)
        ok = (kr >= rstart) & (kr < rstart + NAT_KH)
        dr = np.clip(kr - qr + NAT_KH - 1, 0, 2 * NAT_KH - 2)
        ok_r[v] = ok
        sel_r[v] = np.eye(2 * NAT_KH - 1, dtype=np.float32)[dr] * ok[..., None]
    qc = np.arange(GRID_W)[:, None]
    kc = np.arange(GRID_W)[None, :]
    cstart = np.clip(qc - NAT_KW // 2, 0, GRID_W - NAT_KW)
    ok_c = (kc >= cstart) & (kc < cstart + NAT_KW)
    dc = np.clip(kc - qc + NAT_KW - 1, 0, 2 * NAT_KW - 2)
    sel_c = np.eye(2 * NAT_KW - 1, dtype=np.float32)[dc] * ok_c[..., None]
    valid = ok_r[:, :, None, :, None] & ok_c[None, None, :, None, :]
    valid = valid.reshape(nv, 1, NAT_TQ, NAT_TK)
    t = jnp.einsum('vamr,hrc,qkc->vhaqmk', jnp.asarray(sel_r), rpb.astype(f32), jnp.asarray(sel_c),
                   precision=lax.Precision.HIGHEST)
    t = t.reshape(nv, NAT_HEADS, NAT_TQ, NAT_TK)
    return jnp.where(jnp.asarray(valid), t, NEG_INF)


def _nat_call(p, pc, bias):
    B, T, _ = p.shape
    assert T == NAT_ROWS * GRID_W
    L = pc.shape[1]
    W = NAT_HEADS * HEAD_DIM
    tq = NAT_GROUPS * NAT_TQ
    return pl.pallas_call(
        _nat_kernel,
        grid=(B, T // tq),
        in_specs=[pl.BlockSpec((1, tq, W), lambda b, j: (b, j, NQ_COL)),
                  pl.BlockSpec((1, T, W), lambda b, j: (b, 0, NK_COL)),
                  pl.BlockSpec((1, T, W), lambda b, j: (b, 0, NV_COL)),
                  pl.BlockSpec((1, L, W), lambda b, j: (b, 0, NK_COL)),
                  pl.BlockSpec((1, L, W), lambda b, j: (b, 0, NV_COL)),
                  pl.BlockSpec(bias.shape, lambda b, j: (0, 0, 0, 0), pipeline_mode=pl.Buffered(1))],
        out_specs=pl.BlockSpec((1, tq, W), lambda b, j: (b, j, 0)),
        out_shape=jax.ShapeDtypeStruct((B, T, W), bf16),
        compiler_params=_cparams(("parallel", "arbitrary")),
        name="nat_attention",
    )(p, p, p, pc, pc, bias)


GQ_COL = 1536 // 512
GK_COL = 2048 // LANES
GV_COL = 2176 // LANES
GQA_R = GQA_Q_HEADS // GQA_KV_HEADS


def _rope(x, cos, sin):
    lane = lax.broadcasted_iota(jnp.int32, x.shape, 1)
    first = (lane % (HEAD_DIM // 2)) < (HEAD_DIM // 4)
    swapped = jnp.where(first, pltpu.roll(x, LANES - HEAD_DIM // 4, 1), pltpu.roll(x, HEAD_DIM // 4, 1))
    return x * cos + swapped * sin


def _rope_tables(S):
    t = np.arange(S)
    pos = np.stack([t // GRID_W, t % GRID_W], axis=1).astype(np.float64)
    quarter = HEAD_DIM // 4
    freqs = ROPE_BASE ** (-np.arange(quarter, dtype=np.float64) / quarter)
    lane = np.arange(LANES)
    which = (lane % HEAD_DIM) // (HEAD_DIM // 2)
    ang = pos[:, which] * freqs[lane % quarter][None, :]
    sign = np.where((lane % (HEAD_DIM // 2)) < quarter, -1.0, 1.0)[None, :]
    return jnp.asarray(np.cos(ang), f32), jnp.asarray(np.sin(ang) * sign, f32)


def _gqa_kernel(q_ref, kp_ref, kc_ref, kn_ref, vp_ref, vc_ref, vn_ref, kx_ref, vx_ref,
                cp_ref, cc_ref, cn_ref, sp_ref, sc_ref, sn_ref, sink_ref, band_ref, o_ref, *, nb):
    n = pl.program_id(1)
    blk = GQA_BLOCK
    rows = GQA_R * blk
    cos_q, sin_q = cc_ref[...], sc_ref[...]
    qs = [(_rope(q_ref[0, :, r * LANES:(r + 1) * LANES].astype(f32), cos_q, sin_q) * 0.125) for r in range(GQA_R)]
    q_all = jnp.concatenate(qs, axis=0)
    low = _lane_is_low((rows, LANES))
    kp = _rope(kp_ref[0].astype(f32), cp_ref[...], sp_ref[...]).astype(bf16)
    kc = _rope(kc_ref[0].astype(f32), cos_q, sin_q).astype(bf16)
    kn = _rope(kn_ref[0].astype(f32), cn_ref[...], sn_ref[...]).astype(bf16)
    bias_p = jnp.minimum(band_ref[0], jnp.where(n > 0, 0.0, NEG_INF))
    bias_n = jnp.minimum(band_ref[1], jnp.where(n < nb - 1, 0.0, NEG_INF))
    halves = []
    for g in range(GQA_KV_HEADS):
        qm = jnp.where(low if g == 0 else jnp.logical_not(low), q_all, 0.0).astype(bf16)
        logits = [_dot_nt(qm, kp) + bias_p, _dot_nt(qm, kc), _dot_nt(qm, kn) + bias_n, _dot_nt(qm, kx_ref[0])]
        vals = [vp_ref[0], vc_ref[0], vn_ref[0], vx_ref[0]]
        halves.append(_softmax_pv(logits, vals, extra=sink_ref[g][:, 0:1]))
    out = jnp.where(low, halves[0], halves[1])
    for r in range(GQA_R):
        o_ref[0, :, r * LANES:(r + 1) * LANES] = out[r * blk:(r + 1) * blk].astype(o_ref.dtype)


def _sink_cols(sink, blk):
    s = sink.astype(f32).reshape(GQA_KV_HEADS, GQA_R, 1, 1)
    return jnp.broadcast_to(s, (GQA_KV_HEADS, GQA_R, blk, LANES)).reshape(GQA_KV_HEADS, GQA_R * blk, LANES)


def _gqa_band_masks():
    qi = np.arange(GQA_R * GQA_BLOCK)[:, None] % GQA_BLOCK
    kj = np.arange(GQA_BLOCK)[None, :]
    return jnp.asarray(np.stack([np.where(kj >= qi, 0.0, NEG_INF), np.where(kj <= qi, 0.0, NEG_INF)]), f32)


def _gqa_call(p, pc, cos, sin, sink):
    B, T, _ = p.shape
    L = pc.shape[1]
    blk = GQA_BLOCK
    nb = T // blk
    QW = GQA_Q_HEADS * HEAD_DIM
    kern = functools.partial(_gqa_kernel, nb=nb)
    prev = lambda n: jnp.maximum(n - 1, 0)
    nxt = lambda n: jnp.minimum(n + 1, nb - 1)
    kv = lambda col, f: pl.BlockSpec((1, blk, LANES), lambda b, n: (b, f(n), col))
    tab = lambda f: pl.BlockSpec((blk, LANES), lambda b, n: (f(n), 0))
    same = lambda n: n
    return pl.pallas_call(
        kern,
        grid=(B, nb),
        in_specs=[pl.BlockSpec((1, blk, QW), lambda b, n: (b, n, GQ_COL)),
                  kv(GK_COL, prev), kv(GK_COL, same), kv(GK_COL, nxt),
                  kv(GV_COL, prev), kv(GV_COL, same), kv(GV_COL, nxt),
                  pl.BlockSpec((1, L, LANES), lambda b, n: (b, 0, GK_COL)),
                  pl.BlockSpec((1, L, LANES), lambda b, n: (b, 0, GV_COL)),
                  tab(prev), tab(same), tab(nxt), tab(prev), tab(same), tab(nxt),
                  pl.BlockSpec((GQA_KV_HEADS, GQA_R * blk, LANES), lambda b, n: (0, 0, 0)),
                  pl.BlockSpec((2, GQA_R * blk, blk), lambda b, n: (0, 0, 0))],
        out_specs=pl.BlockSpec((1, blk, QW), lambda b, n: (b, n, 0)),
        out_shape=jax.ShapeDtypeStruct((B, T, QW), bf16),
        compiler_params=_cparams(("parallel", "parallel")),
        name="window_gqa",
    )(p, p, p, p, p, p, p, pc, pc, cos, cos, cos, sin, sin, sin, _sink_cols(sink, blk), _gqa_band_masks())


def _ctx_attn_kernel(nq_ref, nk_ref, nv_ref, gq_ref, gk_ref, gv_ref, sink_ref, ob_ref, od_ref, *, L):
    low = _lane_is_low((L, LANES))
    for c in range(NAT_HEADS // 2):
        cols = slice(c * LANES, (c + 1) * LANES)
        q2 = nq_ref[0, :, cols] * 0.125
        halves = []
        for e in range(2):
            qm = jnp.where(low if e == 0 else jnp.logical_not(low), q2, jnp.zeros_like(q2))
            halves.append(_softmax_pv([_dot_nt(qm, nk_ref[0, :, cols])], [nv_ref[0, :, cols]]))
        ob_ref[0, :, cols] = jnp.where(low, halves[0], halves[1]).astype(ob_ref.dtype)
    q_all = jnp.concatenate([gq_ref[0, :, r * LANES:(r + 1) * LANES] for r in range(GQA_R)], axis=0) * 0.125
    low4 = _lane_is_low((GQA_R * L, LANES))
    halves = []
    for g in range(GQA_KV_HEADS):
        qm = jnp.where(low4 if g == 0 else jnp.logical_not(low4), q_all, jnp.zeros_like(q_all))
        halves.append(_softmax_pv([_dot_nt(qm, gk_ref[0])], [gv_ref[0]], extra=sink_ref[g][:, 0:1]))
    out = jnp.where(low4, halves[0], halves[1])
    for r in range(GQA_R):
        od_ref[0, :, r * LANES:(r + 1) * LANES] = out[r * L:(r + 1) * L].astype(od_ref.dtype)


def _ctx_attn_call(pc, sink):
    B, L, _ = pc.shape
    W = NAT_HEADS * HEAD_DIM
    QW = GQA_Q_HEADS * HEAD_DIM
    kern = functools.partial(_ctx_attn_kernel, L=L)
    return pl.pallas_call(
        kern,
        grid=(B,),
        in_specs=[pl.BlockSpec((1, L, W), lambda b: (b, 0, NQ_COL)),
                  pl.BlockSpec((1, L, W), lambda b: (b, 0, NK_COL)),
                  pl.BlockSpec((1, L, W), lambda b: (b, 0, NV_COL)),
                  pl.BlockSpec((1, L, QW), lambda b: (b, 0, GQ_COL)),
                  pl.BlockSpec((1, L, LANES), lambda b: (b, 0, GK_COL)),
                  pl.BlockSpec((1, L, LANES), lambda b: (b, 0, GV_COL)),
                  pl.BlockSpec((GQA_KV_HEADS, GQA_R * L, LANES), lambda b: (0, 0, 0))],
        out_specs=[pl.BlockSpec((1, L, W), lambda b: (b, 0, 0)),
                   pl.BlockSpec((1, L, QW), lambda b: (b, 0, 0))],
        out_shape=[jax.ShapeDtypeStruct((B, L, W), bf16), jax.ShapeDtypeStruct((B, L, QW), bf16)],
        compiler_params=_cparams(("parallel",)),
        name="context_attention",
    )(pc, pc, pc, pc, pc, pc, _sink_cols(sink, L))


def _merge_kernel(x_ref, sh_ref, sc_ref, g1_ref, ya_ref, yb_ref, yc_ref, yd_ref, wg_ref, bg_ref,
                  wa_ref, wb_ref, wc_ref, wd_ref, wo_ref, lg_ref, lb_ref, o_ref):
    D = D_MODEL
    x = x_ref[0]
    xm = (_ln(x) * (1.0 + sc_ref[0]) + sh_ref[0]).astype(bf16)
    m = None
    for j, (y_ref, w_ref) in enumerate(((ya_ref, wa_ref), (yb_ref, wb_ref), (yc_ref, wc_ref), (yd_ref, wd_ref))):
        gate = jax.nn.sigmoid(_dot(xm, wg_ref[:, j * D:(j + 1) * D]) + bg_ref[:, j * D:(j + 1) * D])
        t = gate * _dot(y_ref[0].astype(bf16), w_ref[...])
        m = t if m is None else m + t
    mix = _dot(m.astype(bf16), wo_ref[...])
    z = DEEPNORM_ALPHA * x + g1_ref[0] * mix
    o_ref[0] = _ln(z) * lg_ref[...] + lb_ref[...]


def _merge_call(x, sh, sc, g1, ya, yb, yc, yd, wg, bg, wa, wb, wc, wd, wo, lg, lb, tm):
    B, T, D = x.shape
    tok = lambda w: pl.BlockSpec((1, tm, w), lambda b, i: (b, i, 0))
    mod = pl.BlockSpec((1, 1, D), lambda b, i: (b, 0, 0))
    full = lambda a: pl.BlockSpec(a.shape, lambda b, i: (0,) * a.ndim)
    return pl.pallas_call(
        _merge_kernel,
        grid=(B, T // tm),
        in_specs=[tok(D), mod, mod, mod, tok(ya.shape[-1]), tok(yb.shape[-1]), tok(yc.shape[-1]), tok(yd.shape[-1]),
                  full(wg), full(bg), full(wa), full(wb), full(wc), full(wd), full(wo), full(lg), full(lb)],
        out_specs=tok(D),
        out_shape=jax.ShapeDtypeStruct((B, T, D), f32),
        compiler_params=_cparams(("parallel", "parallel")),
        name="merge_ln",
    )(x, sh, sc, g1, ya, yb, yc, yd, wg, bg, wa, wb, wc, wd, wo, lg, lb)


PEER_TM = 512
PEER_EBLK = 16
PEER_HGROUP = 1
PEER_DENSE_J = 4
BIG_NEG = -3.0e38
SQRT_HALF = 0.7071067811865476


def _top_values(s, count, rank_below=None):
    vals = []
    cur = s
    rank = None if rank_below is None else jnp.full(s.shape, float(rank_below), f32)
    for r in range(count):
        m = jnp.max(cur, axis=0, keepdims=True)
        vals.append(m)
        hit = cur == m
        if rank is not None and r < rank_below:
            rank = jnp.where(hit, float(r), rank)
        if r + 1 < count:
            cur = jnp.where(hit, BIG_NEG, cur)
        yield
    return vals, rank


def _peer_select_chunk(h, s1t, s2t, ln, n1_ref, e1_ref, r2_ref, e2_ref):
    k = PEER_TOPK
    pairs = [(i, j) for i in range(k) for j in range(k) if (i + 1) * (j + 1) <= k]
    pad = (-len(pairs)) % SUBLANES
    top_ranks = [a for a in range(k) if (a + 1) * (PEER_DENSE_J + 1) <= k]
    a, _ = yield from _top_values(s1t, k)
    b, rank2 = yield from _top_values(s2t, k, rank_below=k)
    r2_ref[h, :, ln] = rank2.astype(bf16)
    e2_ref[h, :, ln] = jnp.exp(s2t - b[0]).astype(bf16)
    yield
    cand = jnp.concatenate([a[i] + b[j] for i, j in pairs]
                           + [jnp.full((pad, LANES), BIG_NEG, f32)], axis=0)
    top, _ = yield from _top_values(cand, k)
    tau = top[k - 1]
    z = jnp.sum(jnp.where(cand >= tau, jnp.exp(cand - (a[0] + b[0])), 0.0), axis=0, keepdims=True)
    yield
    n1 = jnp.zeros_like(s1t)
    for j in range(PEER_DENSE_J):
        n1 = n1 + jnp.where(s1t + b[j] >= tau, 1.0, 0.0)
        yield
    for r in top_ranks:
        extra = jnp.zeros_like(tau)
        for j in range(PEER_DENSE_J, k // (r + 1)):
            extra = extra + jnp.where(a[r] + b[j] >= tau, 1.0, 0.0)
        n1 = n1 + jnp.where(s1t == a[r], extra, 0.0)
        yield
    n1_ref[h, :, ln] = n1
    e1_ref[h, :, ln] = jnp.exp(s1t - a[0]) * (0.5 / z)
    yield


def _peer_select_lanes(h, off, nlanes, ut, wq_ref, k1_ref, k2_ref, n1_ref, e1_ref, r2_ref, e2_ref):
    nk = PEER_N_KEYS
    r0 = pl.multiple_of(h * 2 * nk, 2 * nk)
    u_cols = ut[:, pl.ds(off, nlanes)]
    q1 = _dot(wq_ref[pl.ds(r0, nk), :], u_cols).astype(bf16)
    yield
    q2 = _dot(wq_ref[pl.ds(r0 + nk, nk), :], u_cols).astype(bf16)
    yield
    s1 = _dot(k1_ref[h], q1)
    s2 = _dot(k2_ref[h], q2)
    yield
    chunks = [_peer_select_chunk(h, s1[:, t * LANES:(t + 1) * LANES], s2[:, t * LANES:(t + 1) * LANES],
                                 pl.ds(off + t * LANES, LANES), n1_ref, e1_ref, r2_ref, e2_ref)
              for t in range(nlanes // LANES)]
    while chunks:
        for g in list(chunks):
            try:
                next(g)
                yield
            except StopIteration:
                chunks.remove(g)


def _advance(gen, steps=None):
    n = 0
    for _ in gen:
        n += 1
        if steps is not None and n >= steps:
            break


def _peer_kernel(x_ref, sh_ref, sc_ref, g2_ref, wq_ref, k1_ref, k2_ref, eu_ref, ev_ref, lg_ref, lb_ref,
                 o_ref, ut_ref, n1_ref, e1_ref, r2_ref, e2_ref, acc_ref, *, tm, nsteps):
    step = pl.program_id(2)
    nk = PEER_N_KEYS

    @pl.when(step == 0)
    def _select():
        u = _ln(x_ref[0]) * (1.0 + sc_ref[0]) + sh_ref[0]
        ut_ref[...] = u.T.astype(bf16)
        acc_ref[...] = jnp.zeros_like(acc_ref)

        def head(h, carry):
            _advance(_peer_select_lanes(h, 0, tm, ut_ref, wq_ref, k1_ref, k2_ref, n1_ref, e1_ref, r2_ref, e2_ref))
            return carry

        lax.fori_loop(0, PEER_HEADS, head, 0)

    gs = []
    for b0 in range(0, PEER_EBLK, PEER_HGROUP):
        hts = _dot(eu_ref[b0 * nk:(b0 + PEER_HGROUP) * nk, :], ut_ref[...])
        for b in range(PEER_HGROUP):
            i1 = step * PEER_EBLK + b0 + b
            ht = hts[b * nk:(b + 1) * nk]
            w = jnp.zeros((nk, tm), bf16)
            for h in range(PEER_HEADS):
                n1 = n1_ref[h, pl.ds(i1, 1), :].astype(bf16)
                e1 = e1_ref[h, pl.ds(i1, 1), :].astype(bf16)
                w = w + jnp.where(r2_ref[h] < n1, e2_ref[h] * e1, jnp.zeros((), bf16))
            act = ht * (1.0 + lax.erf(ht * SQRT_HALF))
            gs.append(w * act.astype(bf16))
    acc_ref[...] += _dot(ev_ref[...], jnp.concatenate(gs, axis=0))

    @pl.when(step == nsteps - 1)
    def _finish():
        z = DEEPNORM_ALPHA * x_ref[0] + g2_ref[0] * acc_ref[...].T
        o_ref[0] = _ln(z) * lg_ref[...] + lb_ref[...]


def _peer_call(x, sh, sc, g2, wq_t, k1, k2, eu, ev_t, lg, lb):
    B, T, D = x.shape
    tm = PEER_TM
    nk = PEER_N_KEYS
    ne = eu.shape[0]
    eb = PEER_EBLK * nk
    nsteps = ne // eb
    kern = functools.partial(_peer_kernel, tm=tm, nsteps=nsteps)
    mod = pl.BlockSpec((1, 1, D), lambda b, i, s: (b, 0, 0))
    full = lambda a, **kw: pl.BlockSpec(a.shape, lambda b, i, s: (0,) * a.ndim, **kw)
    sel = lambda dt: pltpu.VMEM((PEER_HEADS, nk, tm), dt)
    return pl.pallas_call(
        kern,
        grid=(B, T // tm, nsteps),
        in_specs=[pl.BlockSpec((1, tm, D), lambda b, i, s: (b, i, 0)), mod, mod, mod,
                  full(wq_t, pipeline_mode=pl.Buffered(1)), full(k1), full(k2),
                  pl.BlockSpec((eb, D), lambda b, i, s: (s, 0)),
                  pl.BlockSpec((D, eb), lambda b, i, s: (0, s)),
                  full(lg), full(lb)],
        out_specs=pl.BlockSpec((1, tm, D), lambda b, i, s: (b, i, 0)),
        out_shape=jax.ShapeDtypeStruct((B, T, D), f32),
        scratch_shapes=[pltpu.VMEM((D, tm), bf16), sel(f32), sel(f32), sel(bf16), sel(bf16),
                        pltpu.VMEM((D, tm), f32)],
        compiler_params=_cparams(("parallel", "parallel", "arbitrary")),
        name="peer_ffn",
    )(x, sh, sc, g2, wq_t, k1, k2, eu, ev_t, lg, lb)


def _gqa_head_perm():
    cols = []
    for c in range(GQA_R):
        cols += list(range(c * HEAD_DIM, (c + 1) * HEAD_DIM))
        cols += list(range((GQA_R + c) * HEAD_DIM, (GQA_R + c + 1) * HEAD_DIM))
    return np.asarray(cols)


def kernel(x, c, ctx, c_ctx, w_ada, b_ada, w_in, b_in, conv_w, conv_b, conv_ln_g, conv_ln_b, nat_rpb, gqa_sink,
           w_branch_a, w_branch_b, w_branch_c, w_branch_d, w_out, ln1_g, ln1_b, peer_wq, peer_k1, peer_k2,
           peer_u, peer_v, ln2_g, ln2_b):
    B, S, D = x.shape
    L = ctx.shape[1]
    gq0, gq1 = 1536, 2048
    perm = _gqa_head_perm()

    cvec = jnp.concatenate([c, c_ctx[None, :], jnp.zeros((8 - B - 1, D), f32)], axis=0)
    mod = _ada_call(cvec, w_ada, b_ada)
    cos, sin = _rope_tables(S)
    assert S == FFT_N * FFT_N
    ct_mats = _fft_ct_mats()
    cs_ctx = _dft_time_mats(L)
    bd = _dft_chan_mats()
    row = lambda v: v.reshape(1, -1)

    xc = ctx.reshape(1, B * L, D)
    for i in range(DEPTH):
        lat = [mod[i, :B, k * D:(k + 1) * D].reshape(B, 1, D) for k in range(6)]
        con = [mod[i, B:B + 1, k * D:(k + 1) * D].reshape(1, 1, D) for k in range(6)]
        wi, bi = w_in[i], b_in[i]
        w_small = jnp.concatenate([wi[:, :gq0], wi[:, gq0:gq1][:, perm], wi[:, gq1:N_SMALL]], axis=1).astype(bf16)
        b_small = row(jnp.concatenate([bi[:gq0], bi[gq0:gq1][perm], bi[gq1:N_SMALL]]))
        w_gate = wi[:, N_SMALL:].astype(bf16)
        b_gate = row(bi[N_SMALL:])
        wa, wb, wc = w_branch_a[i].astype(bf16), w_branch_b[i].astype(bf16), w_branch_c[i].astype(bf16)
        wd = w_branch_d[i][perm].astype(bf16)
        wo = w_out[i].astype(bf16)
        merge_w = (w_gate, b_gate, wa, wb, wc, wd, wo, row(ln1_g[i]), row(ln1_b[i]))
        peer_w = (peer_wq[i].T.astype(bf16), peer_k1[i].astype(bf16), peer_k2[i].astype(bf16),
                  peer_u[i].astype(bf16), peer_v[i].T.astype(bf16), row(ln2_g[i]), row(ln2_b[i]))
        conv_p = (conv_w[i], conv_b[i], conv_ln_g[i], conv_ln_b[i])

        pc, pcf = _inproj_call(xc, con[0], con[1], w_small, b_small, 512)
        pc, pcf = pc.reshape(B, L, N_SMALL), pcf.reshape(B, L, FNET_W)
        p, pf = _inproj_call(x, lat[0], lat[1], w_small, b_small, 512)
        y_a = _conv_call(p, *conv_p)
        y_b = _nat_call(p, pc, _nat_bias_tables(nat_rpb[i]))
        y_c = _fft_ct_call(pf, ct_mats)
        y_d = _gqa_call(p, pc, cos, sin, gqa_sink[i])
        x = _merge_call(x, lat[0], lat[1], lat[2], y_a, y_b, y_c, y_d, *merge_w, 256)
        x = _peer_call(x, lat[3], lat[4], lat[5], *peer_w)

        if i < DEPTH - 1:
            yc_a = _conv_call(pc, *conv_p)
            yc_b, yc_d = _ctx_attn_call(pc, gqa_sink[i])
            yc_c = _fft_call(pcf, cs_ctx, bd)
            flat = lambda a: a.reshape(1, B * L, a.shape[-1])
            xc = _merge_call(xc, con[0], con[1], con[2], flat(yc_a), flat(yc_b), flat(yc_c), flat(yc_d),
                             *merge_w, 256)
            xc = _peer_call(xc, con[3], con[4], con[5], *peer_w)
    return x
```

```python
import functools
import math

import numpy as np
import jax
import jax.numpy as jnp
from jax import lax
from jax.experimental import pallas as pl
from jax.experimental.pallas import tpu as pltpu

f32 = jnp.float32
bf16 = jnp.bfloat16

D_MODEL = 1024
DEPTH = 2
GRID_W = 64
CONV_CH = 256
CONV_WIDTH = 31
NAT_HEADS = 4
NAT_KH = 8
NAT_KW = 16
GQA_Q_HEADS = 8
GQA_KV_HEADS = 2
GQA_BLOCK = 128
HEAD_DIM = 64
ROPE_BASE = 10000.0
PEER_HEADS = 8
PEER_N_KEYS = 128
PEER_TOPK = 16
LN_EPS = 1e-5
NEG_INF = -1e30
DEEPNORM_ALPHA = (2 * DEPTH) ** 0.25

LANES = 128
SUBLANES = 8
N_SMALL = 2304
VMEM_LIMIT = 56 * 1024 * 1024


def _cparams(sem, vmem=VMEM_LIMIT, flags=None):
    return pltpu.CompilerParams(dimension_semantics=sem, vmem_limit_bytes=vmem, flags=flags)


def _ln(x):
    mu = jnp.mean(x, axis=-1, keepdims=True)
    xc = x - mu
    var = jnp.mean(xc * xc, axis=-1, keepdims=True)
    return xc * lax.rsqrt(var + LN_EPS)


def _dot(a, b):
    return jnp.dot(a, b, preferred_element_type=f32)


def _dot_nt(a, b):
    return lax.dot_general(a, b, (((1,), (1,)), ((), ())), preferred_element_type=f32)


def _ada_kernel(c_ref, w_ref, b_ref, o_ref):
    c = c_ref[...]
    h = c * jax.nn.sigmoid(c)
    o_ref[0] = jnp.dot(h, w_ref[0], preferred_element_type=f32,
                       precision=lax.Precision.HIGHEST) + b_ref[0]


def _ada_call(cvec, w_ada, b_ada):
    L, D, N = w_ada.shape
    tn = 1536
    return pl.pallas_call(
        _ada_kernel,
        grid=(L, N // tn),
        in_specs=[pl.BlockSpec((8, D), lambda l, j: (0, 0)),
                  pl.BlockSpec((1, D, tn), lambda l, j: (l, 0, j)),
                  pl.BlockSpec((1, 1, tn), lambda l, j: (l, 0, j))],
        out_specs=pl.BlockSpec((1, 8, tn), lambda l, j: (l, 0, j)),
        out_shape=jax.ShapeDtypeStruct((L, 8, N), f32),
        compiler_params=_cparams(("parallel", "parallel")),
        name="ada_mod",
    )(cvec, w_ada, b_ada.reshape(L, 1, N))


F_OFF = 1280
FNET_W = 256


def _inproj_kernel(x_ref, sh_ref, sc_ref, w_ref, b_ref, o_ref, of_ref):
    xm = _ln(x_ref[0]) * (1.0 + sc_ref[0]) + sh_ref[0]
    y = _dot(xm.astype(bf16), w_ref[...]) + b_ref[...]
    o_ref[0] = y.astype(o_ref.dtype)
    of_ref[0] = y[:, F_OFF:F_OFF + FNET_W]


def _inproj_call(x, sh, sc, w, b, tm):
    B, T, D = x.shape
    N = w.shape[1]
    return pl.pallas_call(
        _inproj_kernel,
        grid=(B, T // tm),
        in_specs=[pl.BlockSpec((1, tm, D), lambda b, i: (b, i, 0)),
                  pl.BlockSpec((1, 1, D), lambda b, i: (b, 0, 0)),
                  pl.BlockSpec((1, 1, D), lambda b, i: (b, 0, 0)),
                  pl.BlockSpec((D, N), lambda b, i: (0, 0)),
                  pl.BlockSpec((1, N), lambda b, i: (0, 0))],
        out_specs=[pl.BlockSpec((1, tm, N), lambda b, i: (b, i, 0)),
                   pl.BlockSpec((1, tm, FNET_W), lambda b, i: (b, i, 0))],
        out_shape=[jax.ShapeDtypeStruct((B, T, N), bf16), jax.ShapeDtypeStruct((B, T, FNET_W), f32)],
        compiler_params=_cparams(("parallel", "parallel")),
        name="in_proj",
    )(x, sh, sc, w, b)


CONV_HALO = 16
CONV_SUB = 128


def _conv_kernel(prev_ref, cur_ref, next_ref, w_ref, cb_ref, g_ref, b_ref, o_ref, hs_ref, rot_ref, *, tc, nchunks):
    i = pl.program_id(1)

    def glu(v):
        v = v.astype(f32)
        return v[:, :CONV_CH] * jax.nn.sigmoid(v[:, CONV_CH:])

    hs_ref[0:CONV_HALO, :] = jnp.where(i > 0, glu(prev_ref[0]), 0.0)
    hs_ref[CONV_HALO:CONV_HALO + tc, :] = glu(cur_ref[0])
    hs_ref[CONV_HALO + tc:2 * CONV_HALO + tc, :] = jnp.where(i < nchunks - 1, glu(next_ref[0]), 0.0)
    base = CONV_HALO - CONV_WIDTH // 2
    span = tc + 2 * CONV_HALO - SUBLANES
    for r in range(1, SUBLANES):
        rot_ref[r] = hs_ref[pl.ds(r, span), :]
    for s in range(tc // CONV_SUB):
        acc = jnp.zeros((CONV_SUB, CONV_CH), f32)
        for j in range(CONV_WIDTH):
            off = base + j
            r, q = off % SUBLANES, off - off % SUBLANES
            src = hs_ref[pl.ds(s * CONV_SUB + q, CONV_SUB), :] if r == 0 else rot_ref[r, pl.ds(s * CONV_SUB + q, CONV_SUB), :]
            acc = acc + src * w_ref[j:j + 1, :]
        y = _ln(acc + cb_ref[...]) * g_ref[...] + b_ref[...]
        y = y * jax.nn.sigmoid(y)
        o_ref[0, s * CONV_SUB:(s + 1) * CONV_SUB, :] = y.astype(o_ref.dtype)


def _conv_call(p, conv_w, conv_b, ln_g, ln_b):
    B, T, _ = p.shape
    tc = min(512, T)
    nchunks = T // tc
    hb = tc // CONV_HALO
    nhb = T // CONV_HALO
    width = 2 * CONV_CH
    kern = functools.partial(_conv_kernel, tc=tc, nchunks=nchunks)
    vec = lambda v: v.reshape(1, CONV_CH)
    return pl.pallas_call(
        kern,
        grid=(B, nchunks),
        in_specs=[pl.BlockSpec((1, CONV_HALO, width), lambda b, i: (b, jnp.maximum(i * hb - 1, 0), 0)),
                  pl.BlockSpec((1, tc, width), lambda b, i: (b, i, 0)),
                  pl.BlockSpec((1, CONV_HALO, width), lambda b, i: (b, jnp.minimum((i + 1) * hb, nhb - 1), 0)),
                  pl.BlockSpec((CONV_WIDTH, CONV_CH), lambda b, i: (0, 0)),
                  pl.BlockSpec((1, CONV_CH), lambda b, i: (0, 0)),
                  pl.BlockSpec((1, CONV_CH), lambda b, i: (0, 0)),
                  pl.BlockSpec((1, CONV_CH), lambda b, i: (0, 0))],
        out_specs=pl.BlockSpec((1, tc, CONV_CH), lambda b, i: (b, i, 0)),
        out_shape=jax.ShapeDtypeStruct((B, T, CONV_CH), bf16),
        scratch_shapes=[pltpu.VMEM((tc + 2 * CONV_HALO, CONV_CH), f32),
                        pltpu.VMEM((SUBLANES, tc + 2 * CONV_HALO - SUBLANES, CONV_CH), f32)],
        compiler_params=_cparams(("parallel", "parallel")),
        name="conv_branch",
    )(p, p, p, conv_w, vec(conv_b), vec(ln_g), vec(ln_b))


FNET_GROUP_DIM = 64


def _fft_kernel(f_ref, cs_ref, bd_ref, o_ref, rhs_ref, *, T, scale):
    i = pl.program_id(0)
    b = pl.program_id(1)

    @pl.when(i == 0)
    def _():
        rows = min(512, T)
        for r in range(T // rows):
            z = _dot(f_ref[0, r * rows:(r + 1) * rows, :].astype(bf16), bd_ref[...])
            rhs_ref[b, r * rows:(r + 1) * rows, :] = z[:, :FNET_W].astype(bf16)
            rhs_ref[b, T + r * rows:T + (r + 1) * rows, :] = (-z[:, FNET_W:]).astype(bf16)

    o_ref[0] = (_dot(cs_ref[...], rhs_ref[b]) * scale).astype(o_ref.dtype)


def _dft_time_mats(T):
    t = jnp.arange(T, dtype=jnp.int32)
    ang = ((t[:, None] * t[None, :]) % T).astype(f32) * f32(2.0 * math.pi / T)
    return jnp.concatenate([jnp.cos(ang), jnp.sin(ang)], axis=1).astype(bf16)


def _dft_chan_mats():
    n = FNET_GROUP_DIM
    k = np.arange(n)
    ang = 2.0 * np.pi * ((k[:, None] * k[None, :]) % n) / n
    eye = np.eye(FNET_W // n)
    return jnp.asarray(np.concatenate([np.kron(eye, np.cos(ang)), np.kron(eye, np.sin(ang))], axis=1), f32).astype(bf16)


def _fft_call(pf, cs, bd):
    B, T, _ = pf.shape
    p = pf
    tm = min(512, T)
    kern = functools.partial(_fft_kernel, T=T, scale=1.0 / math.sqrt(T * FNET_GROUP_DIM))
    return pl.pallas_call(
        kern,
        grid=(T // tm, B),
        in_specs=[pl.BlockSpec((1, T, FNET_W), lambda i, b: (b, 0, 0)),
                  pl.BlockSpec((tm, 2 * T), lambda i, b: (i, 0)),
                  pl.BlockSpec((FNET_W, 2 * FNET_W), lambda i, b: (0, 0))],
        out_specs=pl.BlockSpec((1, tm, FNET_W), lambda i, b: (b, i, 0)),
        out_shape=jax.ShapeDtypeStruct((B, T, FNET_W), bf16),
        scratch_shapes=[pltpu.VMEM((B, 2 * T, FNET_W), bf16)],
        compiler_params=_cparams(("arbitrary", "arbitrary")),
        name="fourier_branch",
    )(p, cs, bd)


FFT_N = 64
FFT_CHUNK = 16


def _fft_s1_kernel(x_ref, cs_ref, cd_ref, twr_ref, twi_ref, o_ref, scr_ref):
    n, w = FFT_N, FNET_W
    for j in range(FFT_CHUNK):
        a = _dot(cs_ref[...], x_ref[0, :, j, :].astype(bf16))
        scr_ref[j * n:(j + 1) * n, 0:w] = a[0:n].astype(bf16)
        scr_ref[j * n:(j + 1) * n, w:2 * w] = a[n:2 * n].astype(bf16)
    z = _dot(scr_ref[...], cd_ref[...])
    twr = jnp.concatenate([twr_ref[...]] * (w // LANES), axis=1)
    twi = jnp.concatenate([twi_ref[...]] * (w // LANES), axis=1)
    zr, zi = z[:, :w], z[:, w:]
    br = zr * twr - zi * twi
    bi = zr * twi + zi * twr
    for j in range(FFT_CHUNK):
        o_ref[0, j, :, 0:w] = br[j * n:(j + 1) * n]
        o_ref[0, j, :, w:2 * w] = bi[j * n:(j + 1) * n]


def _fft_s2_kernel(b_ref, cs_ref, o_ref, *, scale):
    n, w = FFT_N, FNET_W
    for j in range(FFT_CHUNK):
        r = _dot(cs_ref[...], b_ref[0, :, j, :].astype(bf16))
        o_ref[0, :, j, :] = (r[0:n, 0:w] + r[n:2 * n, w:2 * w]) * scale


def _fft_ct_mats():
    n, w = FFT_N, FNET_W
    k = np.arange(n)
    ang = 2.0 * np.pi * ((k[:, None] * k[None, :]) % n) / n
    c, s = np.cos(ang), np.sin(ang)
    g = FNET_GROUP_DIM
    kg = np.arange(g)
    ang_g = 2.0 * np.pi * ((kg[:, None] * kg[None, :]) % g) / g
    eye = np.eye(w // g)
    cbd, sbd = np.kron(eye, np.cos(ang_g)), np.kron(eye, np.sin(ang_g))
    cs1 = np.concatenate([c, -s], axis=0)
    cd = np.block([[cbd, -sbd], [sbd, cbd]])
    cs2 = np.concatenate([c, s], axis=0)
    t2 = np.arange(n)[:, None]
    k1 = np.arange(n)[None, :]
    tw = 2.0 * np.pi * ((t2 * k1) % (n * n)) / (n * n)
    twr = np.broadcast_to(np.cos(tw).reshape(n * n, 1), (n * n, LANES))
    twi = np.broadcast_to(-np.sin(tw).reshape(n * n, 1), (n * n, LANES))
    as_bf = lambda m: jnp.asarray(m, f32).astype(bf16)
    return as_bf(cs1), as_bf(cd), as_bf(cs2), jnp.asarray(twr, f32), jnp.asarray(twi, f32)


def _fft_ct_call(pf, mats):
    B, T, w = pf.shape
    n, ch = FFT_N, FFT_CHUNK
    cs1, cd, cs2, twr, twi = mats
    full = lambda a: pl.BlockSpec(a.shape, lambda b, i: (0,) * a.ndim)
    stage1 = pl.pallas_call(
        _fft_s1_kernel,
        grid=(B, n // ch),
        in_specs=[pl.BlockSpec((1, n, ch, w), lambda b, i: (b, 0, i, 0)), full(cs1), full(cd),
                  pl.BlockSpec((ch * n, LANES), lambda b, i: (i, 0)),
                  pl.BlockSpec((ch * n, LANES), lambda b, i: (i, 0))],
        out_specs=pl.BlockSpec((1, ch, n, 2 * w), lambda b, i: (b, i, 0, 0)),
        out_shape=jax.ShapeDtypeStruct((B, n, n, 2 * w), f32),
        scratch_shapes=[pltpu.VMEM((ch * n, 2 * w), bf16)],
        compiler_params=_cparams(("parallel", "parallel")),
        name="fourier_stage1",
    )(pf.reshape(B, n, n, w), cs1, cd, twr, twi)
    kern2 = functools.partial(_fft_s2_kernel, scale=1.0 / math.sqrt(T * FNET_GROUP_DIM))
    out = pl.pallas_call(
        kern2,
        grid=(B, n // ch),
        in_specs=[pl.BlockSpec((1, n, ch, 2 * w), lambda b, i: (b, 0, i, 0)), full(cs2)],
        out_specs=pl.BlockSpec((1, n, ch, w), lambda b, i: (b, 0, i, 0)),
        out_shape=jax.ShapeDtypeStruct((B, n, n, w), f32),
        compiler_params=_cparams(("parallel", "parallel")),
        name="fourier_stage2",
    )(stage1, cs2)
    return out.reshape(B, T, w)


def _lane_is_low(shape):
    return lax.broadcasted_iota(jnp.int32, shape, len(shape) - 1) < HEAD_DIM


def _softmax_pv_steps(make_logits, values, extra=None):
    def fold(x, op, acc):
        for c in range(x.shape[1] // LANES):
            blk = x[:, c * LANES:(c + 1) * LANES]
            acc = blk if acc is None else op(acc, blk)
        return acc

    logits = make_logits()
    yield
    m_el = None
    for s in logits:
        m_el = fold(s, jnp.maximum, m_el)
    m = m_el.max(axis=-1, keepdims=True)
    if extra is not None:
        m = jnp.maximum(m, extra)
    yield
    d_el = None
    out = None
    for s, v in zip(logits, values):
        p = jnp.exp(s - m)
        d_el = fold(p, jnp.add, d_el)
        o = _dot(p.astype(bf16), v)
        out = o if out is None else out + o
        yield
    den = d_el.sum(axis=-1, keepdims=True)
    if extra is not None:
        den = den + jnp.exp(extra - m)
    return out * (1.0 / den)


def _run_together(gens):
    results = [None] * len(gens)
    live = list(enumerate(gens))
    while live:
        for item in list(live):
            i, g = item
            try:
                next(g)
            except StopIteration as stop:
                results[i] = stop.value
                live.remove(item)
    return results


def _softmax_pv(logits, values, extra=None):
    return _run_together([_softmax_pv_steps(lambda: logits, values, extra)])[0]


NAT_QROWS = 4
NAT_GROUPS = 4
NAT_BAND = NAT_QROWS + NAT_KH
NAT_TQ = NAT_QROWS * GRID_W
NAT_TK = NAT_BAND * GRID_W
NAT_ROWS = 64
NAT_EDGE_GROUPS = -(-(NAT_KH // 2) // NAT_QROWS)
NQ_COL, NK_COL, NV_COL = 2, 3, 4


def _nat_band_start(group):
    return jnp.clip(group * NAT_QROWS - NAT_KH // 2, 0, NAT_ROWS - NAT_BAND)


def _nat_kernel(q_ref, k_ref, v_ref, kc_ref, vc_ref, bias_ref, o_ref):
    j = pl.program_id(1)
    ngroups = NAT_ROWS // NAT_QROWS
    low = _lane_is_low((NAT_TQ, LANES))
    for g in range(NAT_GROUPS):
        group = j * NAT_GROUPS + g
        start = pl.multiple_of(_nat_band_start(group) * GRID_W, NAT_QROWS * GRID_W)
        var = jnp.where(group < NAT_EDGE_GROUPS, group + 1,
                        jnp.where(group >= ngroups - NAT_EDGE_GROUPS,
                                  group - (ngroups - NAT_EDGE_GROUPS) + NAT_EDGE_GROUPS + 1, 0))
        rows = slice(g * NAT_TQ, (g + 1) * NAT_TQ)
        heads = []
        for c in range(NAT_HEADS // 2):
            cols = slice(c * LANES, (c + 1) * LANES)
            q2 = q_ref[0, rows, cols] * 0.125
            k2 = k_ref[0, pl.ds(start, NAT_TK), cols]
            v2 = v_ref[0, pl.ds(start, NAT_TK), cols]
            kc2 = kc_ref[0, :, cols]
            vc2 = vc_ref[0, :, cols]

            def logits_of(e, q2=q2, k2=k2, kc2=kc2, c=c):
                qm = jnp.where(low if e == 0 else jnp.logical_not(low), q2, jnp.zeros_like(q2))
                return [_dot_nt(qm, k2) + bias_ref[var, 2 * c + e], _dot_nt(qm, kc2)]

            heads += [_softmax_pv_steps(functools.partial(logits_of, e), [v2, vc2]) for e in range(2)]
        outs = _run_together(heads)
        for c in range(NAT_HEADS // 2):
            cols = slice(c * LANES, (c + 1) * LANES)
            o_ref[0, rows, cols] = jnp.where(low, outs[2 * c], outs[2 * c + 1]).astype(o_ref.dtype)


def _nat_bias_tables(rpb):
    rows = NAT_ROWS
    ngroups = rows // NAT_QROWS
    reps = [NAT_EDGE_GROUPS] + list(range(NAT_EDGE_GROUPS)) + list(range(ngroups - NAT_EDGE_GROUPS, ngroups))
    nv = len(reps)
    a = np.arange(NAT_QROWS)[:, None]
    m = np.arange(NAT_BAND)[None, :]
    sel_r = np.zeros((nv, NAT_QROWS, NAT_BAND, 2 * NAT_KH - 1), np.float32)
    ok_r = np.zeros((nv, NAT_QROWS, NAT_BAND), bool)
    for v, jj in enumerate(reps):
        band0 = int(np.clip(jj * NAT_QROWS - NAT_KH // 2, 0, rows - NAT_BAND))
        qr = jj * NAT_QROWS + a
        kr = band0 + m
        rstart = np.clip(qr - NAT_KH // 2, 0, rows - NAT_KH)
        ok = (kr >= rstart) & (kr < rstart + NAT_KH)
        dr = np.clip(kr - qr + NAT_KH - 1, 0, 2 * NAT_KH - 2)
        ok_r[v] = ok
        sel_r[v] = np.eye(2 * NAT_KH - 1, dtype=np.float32)[dr] * ok[..., None]
    qc = np.arange(GRID_W)[:, None]
    kc = np.arange(GRID_W)[None, :]
    cstart = np.clip(qc - NAT_KW // 2, 0, GRID_W - NAT_KW)
    ok_c = (kc >= cstart) & (kc < cstart + NAT_KW)
    dc = np.clip(kc - qc + NAT_KW - 1, 0, 2 * NAT_KW - 2)
    sel_c = np.eye(2 * NAT_KW - 1, dtype=np.float32)[dc] * ok_c[..., None]
    valid = ok_r[:, :, None, :, None] & ok_c[None, None, :, None, :]
    valid = valid.reshape(nv, 1, NAT_TQ, NAT_TK)
    t = jnp.einsum('vamr,hrc,qkc->vhaqmk', jnp.asarray(sel_r), rpb.astype(f32), jnp.asarray(sel_c),
                   precision=lax.Precision.HIGHEST)
    t = t.reshape(nv, NAT_HEADS, NAT_TQ, NAT_TK)
    return jnp.where(jnp.asarray(valid), t, NEG_INF)


def _nat_call(p, pc, bias):
    B, T, _ = p.shape
    assert T == NAT_ROWS * GRID_W
    L = pc.shape[1]
    W = NAT_HEADS * HEAD_DIM
    tq = NAT_GROUPS * NAT_TQ
    return pl.pallas_call(
        _nat_kernel,
        grid=(B, T // tq),
        in_specs=[pl.BlockSpec((1, tq, W), lambda b, j: (b, j, NQ_COL)),
                  pl.BlockSpec((1, T, W), lambda b, j: (b, 0, NK_COL)),
                  pl.BlockSpec((1, T, W), lambda b, j: (b, 0, NV_COL)),
                  pl.BlockSpec((1, L, W), lambda b, j: (b, 0, NK_COL)),
                  pl.BlockSpec((1, L, W), lambda b, j: (b, 0, NV_COL)),
                  pl.BlockSpec(bias.shape, lambda b, j: (0, 0, 0, 0), pipeline_mode=pl.Buffered(1))],
        out_specs=pl.BlockSpec((1, tq, W), lambda b, j: (b, j, 0)),
        out_shape=jax.ShapeDtypeStruct((B, T, W), bf16),
        compiler_params=_cparams(("parallel", "arbitrary")),
        name="nat_attention",
    )(p, p, p, pc, pc, bias)


GQ_COL = 1536 // 512
GK_COL = 2048 // LANES
GV_COL = 2176 // LANES
GQA_R = GQA_Q_HEADS // GQA_KV_HEADS


def _rope(x, cos, sin):
    lane = lax.broadcasted_iota(jnp.int32, x.shape, 1)
    first = (lane % (HEAD_DIM // 2)) < (HEAD_DIM // 4)
    swapped = jnp.where(first, pltpu.roll(x, LANES - HEAD_DIM // 4, 1), pltpu.roll(x, HEAD_DIM // 4, 1))
    return x * cos + swapped * sin


def _rope_tables(S):
    t = np.arange(S)
    pos = np.stack([t // GRID_W, t % GRID_W], axis=1).astype(np.float64)
    quarter = HEAD_DIM // 4
    freqs = ROPE_BASE ** (-np.arange(quarter, dtype=np.float64) / quarter)
    lane = np.arange(LANES)
    which = (lane % HEAD_DIM) // (HEAD_DIM // 2)
    ang = pos[:, which] * freqs[lane % quarter][None, :]
    sign = np.where((lane % (HEAD_DIM // 2)) < quarter, -1.0, 1.0)[None, :]
    return jnp.asarray(np.cos(ang), f32), jnp.asarray(np.sin(ang) * sign, f32)


def _gqa_kernel(q_ref, kp_ref, kc_ref, kn_ref, vp_ref, vc_ref, vn_ref, kx_ref, vx_ref,
                cp_ref, cc_ref, cn_ref, sp_ref, sc_ref, sn_ref, sink_ref, band_ref, o_ref, *, nb):
    n = pl.program_id(1)
    blk = GQA_BLOCK
    rows = GQA_R * blk
    cos_q, sin_q = cc_ref[...], sc_ref[...]
    qs = [(_rope(q_ref[0, :, r * LANES:(r + 1) * LANES].astype(f32), cos_q, sin_q) * 0.125) for r in range(GQA_R)]
    q_all = jnp.concatenate(qs, axis=0)
    low = _lane_is_low((rows, LANES))
    kp = _rope(kp_ref[0].astype(f32), cp_ref[...], sp_ref[...]).astype(bf16)
    kc = _rope(kc_ref[0].astype(f32), cos_q, sin_q).astype(bf16)
    kn = _rope(kn_ref[0].astype(f32), cn_ref[...], sn_ref[...]).astype(bf16)
    bias_p = jnp.minimum(band_ref[0], jnp.where(n > 0, 0.0, NEG_INF))
    bias_n = jnp.minimum(band_ref[1], jnp.where(n < nb - 1, 0.0, NEG_INF))
    halves = []
    for g in range(GQA_KV_HEADS):
        qm = jnp.where(low if g == 0 else jnp.logical_not(low), q_all, 0.0).astype(bf16)
        logits = [_dot_nt(qm, kp) + bias_p, _dot_nt(qm, kc), _dot_nt(qm, kn) + bias_n, _dot_nt(qm, kx_ref[0])]
        vals = [vp_ref[0], vc_ref[0], vn_ref[0], vx_ref[0]]
        halves.append(_softmax_pv(logits, vals, extra=sink_ref[g][:, 0:1]))
    out = jnp.where(low, halves[0], halves[1])
    for r in range(GQA_R):
        o_ref[0, :, r * LANES:(r + 1) * LANES] = out[r * blk:(r + 1) * blk].astype(o_ref.dtype)


def _sink_cols(sink, blk):
    s = sink.astype(f32).reshape(GQA_KV_HEADS, GQA_R, 1, 1)
    return jnp.broadcast_to(s, (GQA_KV_HEADS, GQA_R, blk, LANES)).reshape(GQA_KV_HEADS, GQA_R * blk, LANES)


def _gqa_band_masks():
    qi = np.arange(GQA_R * GQA_BLOCK)[:, None] % GQA_BLOCK
    kj = np.arange(GQA_BLOCK)[None, :]
    return jnp.asarray(np.stack([np.where(kj >= qi, 0.0, NEG_INF), np.where(kj <= qi, 0.0, NEG_INF)]), f32)


def _gqa_call(p, pc, cos, sin, sink):
    B, T, _ = p.shape
    L = pc.shape[1]
    blk = GQA_BLOCK
    nb = T // blk
    QW = GQA_Q_HEADS * HEAD_DIM
    kern = functools.partial(_gqa_kernel, nb=nb)
    prev = lambda n: jnp.maximum(n - 1, 0)
    nxt = lambda n: jnp.minimum(n + 1, nb - 1)
    kv = lambda col, f: pl.BlockSpec((1, blk, LANES), lambda b, n: (b, f(n), col))
    tab = lambda f: pl.BlockSpec((blk, LANES), lambda b, n: (f(n), 0))
    same = lambda n: n
    return pl.pallas_call(
        kern,
        grid=(B, nb),
        in_specs=[pl.BlockSpec((1, blk, QW), lambda b, n: (b, n, GQ_COL)),
                  kv(GK_COL, prev), kv(GK_COL, same), kv(GK_COL, nxt),
                  kv(GV_COL, prev), kv(GV_COL, same), kv(GV_COL, nxt),
                  pl.BlockSpec((1, L, LANES), lambda b, n: (b, 0, GK_COL)),
                  pl.BlockSpec((1, L, LANES), lambda b, n: (b, 0, GV_COL)),
                  tab(prev), tab(same), tab(nxt), tab(prev), tab(same), tab(nxt),
                  pl.BlockSpec((GQA_KV_HEADS, GQA_R * blk, LANES), lambda b, n: (0, 0, 0)),
                  pl.BlockSpec((2, GQA_R * blk, blk), lambda b, n: (0, 0, 0))],
        out_specs=pl.BlockSpec((1, blk, QW), lambda b, n: (b, n, 0)),
        out_shape=jax.ShapeDtypeStruct((B, T, QW), bf16),
        compiler_params=_cparams(("parallel", "parallel")),
        name="window_gqa",
    )(p, p, p, p, p, p, p, pc, pc, cos, cos, cos, sin, sin, sin, _sink_cols(sink, blk), _gqa_band_masks())


def _ctx_attn_kernel(nq_ref, nk_ref, nv_ref, gq_ref, gk_ref, gv_ref, sink_ref, ob_ref, od_ref, *, L):
    low = _lane_is_low((L, LANES))
    for c in range(NAT_HEADS // 2):
        cols = slice(c * LANES, (c + 1) * LANES)
        q2 = nq_ref[0, :, cols] * 0.125
        halves = []
        for e in range(2):
            qm = jnp.where(low if e == 0 else jnp.logical_not(low), q2, jnp.zeros_like(q2))
            halves.append(_softmax_pv([_dot_nt(qm, nk_ref[0, :, cols])], [nv_ref[0, :, cols]]))
        ob_ref[0, :, cols] = jnp.where(low, halves[0], halves[1]).astype(ob_ref.dtype)
    q_all = jnp.concatenate([gq_ref[0, :, r * LANES:(r + 1) * LANES] for r in range(GQA_R)], axis=0) * 0.125
    low4 = _lane_is_low((GQA_R * L, LANES))
    halves = []
    for g in range(GQA_KV_HEADS):
        qm = jnp.where(low4 if g == 0 else jnp.logical_not(low4), q_all, jnp.zeros_like(q_all))
        halves.append(_softmax_pv([_dot_nt(qm, gk_ref[0])], [gv_ref[0]], extra=sink_ref[g][:, 0:1]))
    out = jnp.where(low4, halves[0], halves[1])
    for r in range(GQA_R):
        od_ref[0, :, r * LANES:(r + 1) * LANES] = out[r * L:(r + 1) * L].astype(od_ref.dtype)


def _ctx_attn_call(pc, sink):
    B, L, _ = pc.shape
    W = NAT_HEADS * HEAD_DIM
    QW = GQA_Q_HEADS * HEAD_DIM
    kern = functools.partial(_ctx_attn_kernel, L=L)
    return pl.pallas_call(
        kern,
        grid=(B,),
        in_specs=[pl.BlockSpec((1, L, W), lambda b: (b, 0, NQ_COL)),
                  pl.BlockSpec((1, L, W), lambda b: (b, 0, NK_COL)),
                  pl.BlockSpec((1, L, W), lambda b: (b, 0, NV_COL)),
                  pl.BlockSpec((1, L, QW), lambda b: (b, 0, GQ_COL)),
                  pl.BlockSpec((1, L, LANES), lambda b: (b, 0, GK_COL)),
                  pl.BlockSpec((1, L, LANES), lambda b: (b, 0, GV_COL)),
                  pl.BlockSpec((GQA_KV_HEADS, GQA_R * L, LANES), lambda b: (0, 0, 0))],
        out_specs=[pl.BlockSpec((1, L, W), lambda b: (b, 0, 0)),
                   pl.BlockSpec((1, L, QW), lambda b: (b, 0, 0))],
        out_shape=[jax.ShapeDtypeStruct((B, L, W), bf16), jax.ShapeDtypeStruct((B, L, QW), bf16)],
        compiler_params=_cparams(("parallel",)),
        name="context_attention",
    )(pc, pc, pc, pc, pc, pc, _sink_cols(sink, L))


def _merge_kernel(x_ref, sh_ref, sc_ref, g1_ref, ya_ref, yb_ref, yc_ref, yd_ref, wg_ref, bg_ref,
                  wa_ref, wb_ref, wc_ref, wd_ref, wo_ref, lg_ref, lb_ref, o_ref):
    D = D_MODEL
    x = x_ref[0]
    xm = (_ln(x) * (1.0 + sc_ref[0]) + sh_ref[0]).astype(bf16)
    m = None
    for j, (y_ref, w_ref) in enumerate(((ya_ref, wa_ref), (yb_ref, wb_ref), (yc_ref, wc_ref), (yd_ref, wd_ref))):
        gate = jax.nn.sigmoid(_dot(xm, wg_ref[:, j * D:(j + 1) * D]) + bg_ref[:, j * D:(j + 1) * D])
        t = gate * _dot(y_ref[0].astype(bf16), w_ref[...])
        m = t if m is None else m + t
    mix = _dot(m.astype(bf16), wo_ref[...])
    z = DEEPNORM_ALPHA * x + g1_ref[0] * mix
    o_ref[0] = _ln(z) * lg_ref[...] + lb_ref[...]


MERGE_TM = 512


def _merge_call(x, sh, sc, g1, ya, yb, yc, yd, wg, bg, wa, wb, wc, wd, wo, lg, lb, tm):
    B, T, D = x.shape
    tok = lambda w: pl.BlockSpec((1, tm, w), lambda b, i: (b, i, 0))
    mod = pl.BlockSpec((1, 1, D), lambda b, i: (b, 0, 0))
    full = lambda a: pl.BlockSpec(a.shape, lambda b, i: (0,) * a.ndim, pipeline_mode=pl.Buffered(1))
    return pl.pallas_call(
        _merge_kernel,
        grid=(B, T // tm),
        in_specs=[tok(D), mod, mod, mod, tok(ya.shape[-1]), tok(yb.shape[-1]), tok(yc.shape[-1]), tok(yd.shape[-1]),
                  full(wg), full(bg), full(wa), full(wb), full(wc), full(wd), full(wo), full(lg), full(lb)],
        out_specs=tok(D),
        out_shape=jax.ShapeDtypeStruct((B, T, D), f32),
        compiler_params=_cparams(("parallel", "parallel")),
        name="merge_ln",
    )(x, sh, sc, g1, ya, yb, yc, yd, wg, bg, wa, wb, wc, wd, wo, lg, lb)


PEER_TM = 512
PEER_EBLK = 16
PEER_HGROUP = 1
PEER_SEL_HEADS = 4
PEER_DENSE_J = 4
BIG_NEG = -3.0e38
SQRT_HALF = 0.7071067811865476


def _top_values(s, count, rank_below=None):
    vals = []
    cur = s
    rank = None if rank_below is None else jnp.full(s.shape, float(rank_below), f32)
    for r in range(count):
        m = jnp.max(cur, axis=0, keepdims=True)
        vals.append(m)
        hit = cur == m
        if rank is not None and r < rank_below:
            rank = jnp.where(hit, float(r), rank)
        if r + 1 < count:
            cur = jnp.where(hit, BIG_NEG, cur)
        yield
    return vals, rank


def _peer_select_chunk(h, s1t, s2t, ln, n1_ref, e1_ref, r2_ref, e2_ref):
    k = PEER_TOPK
    pairs = [(i, j) for i in range(k) for j in range(k) if (i + 1) * (j + 1) <= k]
    pad = (-len(pairs)) % SUBLANES
    top_ranks = [a for a in range(k) if (a + 1) * (PEER_DENSE_J + 1) <= k]
    a, _ = yield from _top_values(s1t, k)
    b, rank2 = yield from _top_values(s2t, k, rank_below=k)
    r2_ref[h, :, ln] = rank2.astype(bf16)
    e2_ref[h, :, ln] = jnp.exp(s2t - b[0]).astype(bf16)
    yield
    cand = jnp.concatenate([a[i] + b[j] for i, j in pairs]
                           + [jnp.full((pad, LANES), BIG_NEG, f32)], axis=0)
    top, _ = yield from _top_values(cand, k)
    tau = top[k - 1]
    z = jnp.sum(jnp.where(cand >= tau, jnp.exp(cand - (a[0] + b[0])), 0.0), axis=0, keepdims=True)
    yield
    n1 = jnp.zeros_like(s1t)
    for j in range(PEER_DENSE_J):
        n1 = n1 + jnp.where(s1t + b[j] >= tau, 1.0, 0.0)
        yield
    for r in top_ranks:
        extra = jnp.zeros_like(tau)
        for j in range(PEER_DENSE_J, k // (r + 1)):
            extra = extra + jnp.where(a[r] + b[j] >= tau, 1.0, 0.0)
        n1 = n1 + jnp.where(s1t == a[r], extra, 0.0)
        yield
    n1_ref[h, :, ln] = n1
    e1_ref[h, :, ln] = jnp.exp(s1t - a[0]) * (0.5 / z)
    yield


def _peer_select_lanes(h, off, nlanes, ut, wq_ref, k1_ref, k2_ref, n1_ref, e1_ref, r2_ref, e2_ref):
    nk = PEER_N_KEYS
    r0 = pl.multiple_of(h * 2 * nk, 2 * nk)
    u_cols = ut[:, pl.ds(off, nlanes)]
    q1 = _dot(wq_ref[pl.ds(r0, nk), :], u_cols).astype(bf16)
    yield
    q2 = _dot(wq_ref[pl.ds(r0 + nk, nk), :], u_cols).astype(bf16)
    yield
    s1 = _dot(k1_ref[h], q1)
    s2 = _dot(k2_ref[h], q2)
    yield
    chunks = [_peer_select_chunk(h, s1[:, t * LANES:(t + 1) * LANES], s2[:, t * LANES:(t + 1) * LANES],
                                 pl.ds(off + t * LANES, LANES), n1_ref, e1_ref, r2_ref, e2_ref)
              for t in range(nlanes // LANES)]
    while chunks:
        for g in list(chunks):
            try:
                next(g)
                yield
            except StopIteration:
                chunks.remove(g)


def _advance(gen, steps=None):
    n = 0
    for _ in gen:
        n += 1
        if steps is not None and n >= steps:
            break


def _peer_kernel(x_ref, sh_ref, sc_ref, g2_ref, wq_ref, k1_ref, k2_ref, eu_ref, ev_ref, lg_ref, lb_ref,
                 o_ref, ut_ref, n1_ref, e1_ref, r2_ref, e2_ref, acc_ref, *, tm, nsteps):
    step = pl.program_id(2)
    nk = PEER_N_KEYS

    @pl.when(step == 0)
    def _select():
        u = _ln(x_ref[0]) * (1.0 + sc_ref[0]) + sh_ref[0]
        ut_ref[...] = u.T.astype(bf16)
        acc_ref[...] = jnp.zeros_like(acc_ref)

        def heads(i, carry):
            _run_together([_peer_select_lanes(i * PEER_SEL_HEADS + d, 0, tm, ut_ref, wq_ref, k1_ref, k2_ref,
                                              n1_ref, e1_ref, r2_ref, e2_ref) for d in range(PEER_SEL_HEADS)])
            return carry

        lax.fori_loop(0, PEER_HEADS // PEER_SEL_HEADS, heads, 0)

    gs = []
    for b0 in range(0, PEER_EBLK, PEER_HGROUP):
        hts = _dot(eu_ref[b0 * nk:(b0 + PEER_HGROUP) * nk, :], ut_ref[...])
        for b in range(PEER_HGROUP):
            i1 = step * PEER_EBLK + b0 + b
            ht = hts[b * nk:(b + 1) * nk]
            w = jnp.zeros((nk, tm), bf16)
            for h in range(PEER_HEADS):
                n1 = n1_ref[h, pl.ds(i1, 1), :].astype(bf16)
                e1 = e1_ref[h, pl.ds(i1, 1), :].astype(bf16)
                w = w + jnp.where(r2_ref[h] < n1, e2_ref[h] * e1, jnp.zeros((), bf16))
            act = ht * (1.0 + lax.erf(ht * SQRT_HALF))
            gs.append(w * act.astype(bf16))
    acc_ref[...] += _dot(ev_ref[...], jnp.concatenate(gs, axis=0))

    @pl.when(step == nsteps - 1)
    def _finish():
        z = DEEPNORM_ALPHA * x_ref[0] + g2_ref[0] * acc_ref[...].T
        o_ref[0] = _ln(z) * lg_ref[...] + lb_ref[...]


def _peer_call(x, sh, sc, g2, wq_t, k1, k2, eu, ev_t, lg, lb):
    B, T, D = x.shape
    tm = PEER_TM
    nk = PEER_N_KEYS
    ne = eu.shape[0]
    eb = PEER_EBLK * nk
    nsteps = ne // eb
    kern = functools.partial(_peer_kernel, tm=tm, nsteps=nsteps)
    mod = pl.BlockSpec((1, 1, D), lambda b, i, s: (b, 0, 0))
    full = lambda a, **kw: pl.BlockSpec(a.shape, lambda b, i, s: (0,) * a.ndim, **kw)
    sel = lambda dt: pltpu.VMEM((PEER_HEADS, nk, tm), dt)
    return pl.pallas_call(
        kern,
        grid=(B, T // tm, nsteps),
        in_specs=[pl.BlockSpec((1, tm, D), lambda b, i, s: (b, i, 0)), mod, mod, mod,
                  full(wq_t, pipeline_mode=pl.Buffered(1)), full(k1), full(k2),
                  pl.BlockSpec((eb, D), lambda b, i, s: (s, 0)),
                  pl.BlockSpec((D, eb), lambda b, i, s: (0, s)),
                  full(lg), full(lb)],
        out_specs=pl.BlockSpec((1, tm, D), lambda b, i, s: (b, i, 0)),
        out_shape=jax.ShapeDtypeStruct((B, T, D), f32),
        scratch_shapes=[pltpu.VMEM((D, tm), bf16), sel(f32), sel(f32), sel(bf16), sel(bf16),
                        pltpu.VMEM((D, tm), f32)],
        compiler_params=_cparams(("parallel", "parallel", "arbitrary")),
        name="peer_ffn",
    )(x, sh, sc, g2, wq_t, k1, k2, eu, ev_t, lg, lb)


def _gqa_head_perm():
    cols = []
    for c in range(GQA_R):
        cols += list(range(c * HEAD_DIM, (c + 1) * HEAD_DIM))
        cols += list(range((GQA_R + c) * HEAD_DIM, (GQA_R + c + 1) * HEAD_DIM))
    return np.asarray(cols)


def kernel(x, c, ctx, c_ctx, w_ada, b_ada, w_in, b_in, conv_w, conv_b, conv_ln_g, conv_ln_b, nat_rpb, gqa_sink,
           w_branch_a, w_branch_b, w_branch_c, w_branch_d, w_out, ln1_g, ln1_b, peer_wq, peer_k1, peer_k2,
           peer_u, peer_v, ln2_g, ln2_b):
    B, S, D = x.shape
    L = ctx.shape[1]
    gq0, gq1 = 1536, 2048
    perm = _gqa_head_perm()

    cvec = jnp.concatenate([c, c_ctx[None, :], jnp.zeros((8 - B - 1, D), f32)], axis=0)
    mod = _ada_call(cvec, w_ada, b_ada)
    cos, sin = _rope_tables(S)
    assert S == FFT_N * FFT_N
    ct_mats = _fft_ct_mats()
    cs_ctx = _dft_time_mats(L)
    bd = _dft_chan_mats()
    row = lambda v: v.reshape(1, -1)

    xc = ctx.reshape(1, B * L, D)
    for i in range(DEPTH):
        lat = [mod[i, :B, k * D:(k + 1) * D].reshape(B, 1, D) for k in range(6)]
        con = [mod[i, B:B + 1, k * D:(k + 1) * D].reshape(1, 1, D) for k in range(6)]
        wi, bi = w_in[i], b_in[i]
        w_small = jnp.concatenate([wi[:, :gq0], wi[:, gq0:gq1][:, perm], wi[:, gq1:N_SMALL]], axis=1).astype(bf16)
        b_small = row(jnp.concatenate([bi[:gq0], bi[gq0:gq1][perm], bi[gq1:N_SMALL]]))
        w_gate = wi[:, N_SMALL:].astype(bf16)
        b_gate = row(bi[N_SMALL:])
        wa, wb, wc = w_branch_a[i].astype(bf16), w_branch_b[i].astype(bf16), w_branch_c[i].astype(bf16)
        wd = w_branch_d[i][perm].astype(bf16)
        wo = w_out[i].astype(bf16)
        merge_w = (w_gate, b_gate, wa, wb, wc, wd, wo, row(ln1_g[i]), row(ln1_b[i]))
        peer_w = (peer_wq[i].T.astype(bf16), peer_k1[i].astype(bf16), peer_k2[i].astype(bf16),
                  peer_u[i].astype(bf16), peer_v[i].T.astype(bf16), row(ln2_g[i]), row(ln2_b[i]))
        conv_p = (conv_w[i], conv_b[i], conv_ln_g[i], conv_ln_b[i])

        pc, pcf = _inproj_call(xc, con[0], con[1], w_small, b_small, 512)
        pc, pcf = pc.reshape(B, L, N_SMALL), pcf.reshape(B, L, FNET_W)
        p, pf = _inproj_call(x, lat[0], lat[1], w_small, b_small, 512)
        y_a = _conv_call(p, *conv_p)
        y_b = _nat_call(p, pc, _nat_bias_tables(nat_rpb[i]))
        y_c = _fft_ct_call(pf, ct_mats)
        y_d = _gqa_call(p, pc, cos, sin, gqa_sink[i])
        x = _merge_call(x, lat[0], lat[1], lat[2], y_a, y_b, y_c, y_d, *merge_w, MERGE_TM)
        x = _peer_call(x, lat[3], lat[4], lat[5], *peer_w)

        if i < DEPTH - 1:
            yc_a = _conv_call(pc, *conv_p)
            yc_b, yc_d = _ctx_attn_call(pc, gqa_sink[i])
            yc_c = _fft_call(pcf, cs_ctx, bd)
            flat = lambda a: a.reshape(1, B * L, a.shape[-1])
            xc = _merge_call(xc, con[0], con[1], con[2], flat(yc_a), flat(yc_b), flat(yc_c), flat(yc_d),
                             *merge_w, MERGE_TM)
            xc = _peer_call(xc, con[3], con[4], con[5], *peer_w)
    return x
```

```python
import functools
import math

import numpy as np
import jax
import jax.numpy as jnp
from jax import lax
from jax.experimental import pallas as pl
from jax.experimental.pallas import tpu as pltpu

f32 = jnp.float32
bf16 = jnp.bfloat16

D_MODEL = 1024
DEPTH = 2
GRID_W = 64
CONV_CH = 256
CONV_WIDTH = 31
NAT_HEADS = 4
NAT_KH = 8
NAT_KW = 16
GQA_Q_HEADS = 8
GQA_KV_HEADS = 2
GQA_BLOCK = 128
HEAD_DIM = 64
ROPE_BASE = 10000.0
PEER_HEADS = 8
PEER_N_KEYS = 128
PEER_TOPK = 16
LN_EPS = 1e-5
NEG_INF = -1e30
DEEPNORM_ALPHA = (2 * DEPTH) ** 0.25

LANES = 128
SUBLANES = 8
N_SMALL = 2304
VMEM_LIMIT = 56 * 1024 * 1024


def _cparams(sem, vmem=VMEM_LIMIT, flags=None):
    return pltpu.CompilerParams(dimension_semantics=sem, vmem_limit_bytes=vmem, flags=flags)


def _ln(x):
    mu = jnp.mean(x, axis=-1, keepdims=True)
    xc = x - mu
    var = jnp.mean(xc * xc, axis=-1, keepdims=True)
    return xc * lax.rsqrt(var + LN_EPS)


def _dot(a, b):
    return jnp.dot(a, b, preferred_element_type=f32)


def _dot_nt(a, b):
    return lax.dot_general(a, b, (((1,), (1,)), ((), ())), preferred_element_type=f32)


def _ada_kernel(c_ref, w_ref, b_ref, o_ref):
    c = c_ref[...]
    h = c * jax.nn.sigmoid(c)
    o_ref[0] = jnp.dot(h, w_ref[0], preferred_element_type=f32,
                       precision=lax.Precision.HIGHEST) + b_ref[0]


def _ada_call(cvec, w_ada, b_ada):
    L, D, N = w_ada.shape
    tn = 1536
    return pl.pallas_call(
        _ada_kernel,
        grid=(L, N // tn),
        in_specs=[pl.BlockSpec((8, D), lambda l, j: (0, 0)),
                  pl.BlockSpec((1, D, tn), lambda l, j: (l, 0, j)),
                  pl.BlockSpec((1, 1, tn), lambda l, j: (l, 0, j))],
        out_specs=pl.BlockSpec((1, 8, tn), lambda l, j: (l, 0, j)),
        out_shape=jax.ShapeDtypeStruct((L, 8, N), f32),
        compiler_params=_cparams(("parallel", "parallel")),
        name="ada_mod",
    )(cvec, w_ada, b_ada.reshape(L, 1, N))


F_OFF = 1280
FNET_W = 256


def _inproj_kernel(x_ref, sh_ref, sc_ref, w_ref, b_ref, o_ref, of_ref):
    xm = _ln(x_ref[0]) * (1.0 + sc_ref[0]) + sh_ref[0]
    y = _dot(xm.astype(bf16), w_ref[...]) + b_ref[...]
    o_ref[0] = y.astype(o_ref.dtype)
    of_ref[0] = y[:, F_OFF:F_OFF + FNET_W]


def _inproj_call(x, sh, sc, w, b, tm):
    B, T, D = x.shape
    N = w.shape[1]
    return pl.pallas_call(
        _inproj_kernel,
        grid=(B, T // tm),
        in_specs=[pl.BlockSpec((1, tm, D), lambda b, i: (b, i, 0)),
                  pl.BlockSpec((1, 1, D), lambda b, i: (b, 0, 0)),
                  pl.BlockSpec((1, 1, D), lambda b, i: (b, 0, 0)),
                  pl.BlockSpec((D, N), lambda b, i: (0, 0)),
                  pl.BlockSpec((1, N), lambda b, i: (0, 0))],
        out_specs=[pl.BlockSpec((1, tm, N), lambda b, i: (b, i, 0)),
                   pl.BlockSpec((1, tm, FNET_W), lambda b, i: (b, i, 0))],
        out_shape=[jax.ShapeDtypeStruct((B, T, N), bf16), jax.ShapeDtypeStruct((B, T, FNET_W), f32)],
        compiler_params=_cparams(("parallel", "parallel")),
        name="in_proj",
    )(x, sh, sc, w, b)


CONV_HALO = 16
CONV_SUB = 128


def _conv_kernel(prev_ref, cur_ref, next_ref, w_ref, cb_ref, g_ref, b_ref, o_ref, hs_ref, rot_ref, *, tc, nchunks):
    i = pl.program_id(1)

    def glu(v):
        v = v.astype(f32)
        return v[:, :CONV_CH] * jax.nn.sigmoid(v[:, CONV_CH:])

    hs_ref[0:CONV_HALO, :] = jnp.where(i > 0, glu(prev_ref[0]), 0.0)
    hs_ref[CONV_HALO:CONV_HALO + tc, :] = glu(cur_ref[0])
    hs_ref[CONV_HALO + tc:2 * CONV_HALO + tc, :] = jnp.where(i < nchunks - 1, glu(next_ref[0]), 0.0)
    base = CONV_HALO - CONV_WIDTH // 2
    span = tc + 2 * CONV_HALO - SUBLANES
    for r in range(1, SUBLANES):
        rot_ref[r] = hs_ref[pl.ds(r, span), :]
    for s in range(tc // CONV_SUB):
        acc = jnp.zeros((CONV_SUB, CONV_CH), f32)
        for j in range(CONV_WIDTH):
            off = base + j
            r, q = off % SUBLANES, off - off % SUBLANES
            src = hs_ref[pl.ds(s * CONV_SUB + q, CONV_SUB), :] if r == 0 else rot_ref[r, pl.ds(s * CONV_SUB + q, CONV_SUB), :]
            acc = acc + src * w_ref[j:j + 1, :]
        y = _ln(acc + cb_ref[...]) * g_ref[...] + b_ref[...]
        y = y * jax.nn.sigmoid(y)
        o_ref[0, s * CONV_SUB:(s + 1) * CONV_SUB, :] = y.astype(o_ref.dtype)


def _conv_call(p, conv_w, conv_b, ln_g, ln_b):
    B, T, _ = p.shape
    tc = min(512, T)
    nchunks = T // tc
    hb = tc // CONV_HALO
    nhb = T // CONV_HALO
    width = 2 * CONV_CH
    kern = functools.partial(_conv_kernel, tc=tc, nchunks=nchunks)
    vec = lambda v: v.reshape(1, CONV_CH)
    return pl.pallas_call(
        kern,
        grid=(B, nchunks),
        in_specs=[pl.BlockSpec((1, CONV_HALO, width), lambda b, i: (b, jnp.maximum(i * hb - 1, 0), 0)),
                  pl.BlockSpec((1, tc, width), lambda b, i: (b, i, 0)),
                  pl.BlockSpec((1, CONV_HALO, width), lambda b, i: (b, jnp.minimum((i + 1) * hb, nhb - 1), 0)),
                  pl.BlockSpec((CONV_WIDTH, CONV_CH), lambda b, i: (0, 0)),
                  pl.BlockSpec((1, CONV_CH), lambda b, i: (0, 0)),
                  pl.BlockSpec((1, CONV_CH), lambda b, i: (0, 0)),
                  pl.BlockSpec((1, CONV_CH), lambda b, i: (0, 0))],
        out_specs=pl.BlockSpec((1, tc, CONV_CH), lambda b, i: (b, i, 0)),
        out_shape=jax.ShapeDtypeStruct((B, T, CONV_CH), bf16),
        scratch_shapes=[pltpu.VMEM((tc + 2 * CONV_HALO, CONV_CH), f32),
                        pltpu.VMEM((SUBLANES, tc + 2 * CONV_HALO - SUBLANES, CONV_CH), f32)],
        compiler_params=_cparams(("parallel", "parallel")),
        name="conv_branch",
    )(p, p, p, conv_w, vec(conv_b), vec(ln_g), vec(ln_b))


FNET_GROUP_DIM = 64


def _fft_kernel(f_ref, cs_ref, bd_ref, o_ref, rhs_ref, *, T, scale):
    i = pl.program_id(0)
    b = pl.program_id(1)

    @pl.when(i == 0)
    def _():
        rows = min(512, T)
        for r in range(T // rows):
            z = _dot(f_ref[0, r * rows:(r + 1) * rows, :].astype(bf16), bd_ref[...])
            rhs_ref[b, r * rows:(r + 1) * rows, :] = z[:, :FNET_W].astype(bf16)
            rhs_ref[b, T + r * rows:T + (r + 1) * rows, :] = (-z[:, FNET_W:]).astype(bf16)

    o_ref[0] = (_dot(cs_ref[...], rhs_ref[b]) * scale).astype(o_ref.dtype)


def _dft_time_mats(T):
    t = jnp.arange(T, dtype=jnp.int32)
    ang = ((t[:, None] * t[None, :]) % T).astype(f32) * f32(2.0 * math.pi / T)
    return jnp.concatenate([jnp.cos(ang), jnp.sin(ang)], axis=1).astype(bf16)


def _dft_chan_mats():
    n = FNET_GROUP_DIM
    k = np.arange(n)
    ang = 2.0 * np.pi * ((k[:, None] * k[None, :]) % n) / n
    eye = np.eye(FNET_W // n)
    return jnp.asarray(np.concatenate([np.kron(eye, np.cos(ang)), np.kron(eye, np.sin(ang))], axis=1), f32).astype(bf16)


def _fft_call(pf, cs, bd):
    B, T, _ = pf.shape
    p = pf
    tm = min(512, T)
    kern = functools.partial(_fft_kernel, T=T, scale=1.0 / math.sqrt(T * FNET_GROUP_DIM))
    return pl.pallas_call(
        kern,
        grid=(T // tm, B),
        in_specs=[pl.BlockSpec((1, T, FNET_W), lambda i, b: (b, 0, 0)),
                  pl.BlockSpec((tm, 2 * T), lambda i, b: (i, 0)),
                  pl.BlockSpec((FNET_W, 2 * FNET_W), lambda i, b: (0, 0))],
        out_specs=pl.BlockSpec((1, tm, FNET_W), lambda i, b: (b, i, 0)),
        out_shape=jax.ShapeDtypeStruct((B, T, FNET_W), bf16),
        scratch_shapes=[pltpu.VMEM((B, 2 * T, FNET_W), bf16)],
        compiler_params=_cparams(("arbitrary", "arbitrary")),
        name="fourier_branch",
    )(p, cs, bd)


FFT_N = 64
FFT_CHUNK = 16


def _fft_s1_kernel(x_ref, cs_ref, cd_ref, twr_ref, twi_ref, o_ref, scr_ref):
    n, w = FFT_N, FNET_W
    for j in range(FFT_CHUNK):
        a = _dot(cs_ref[...], x_ref[0, :, j, :].astype(bf16))
        scr_ref[j * n:(j + 1) * n, 0:w] = a[0:n].astype(bf16)
        scr_ref[j * n:(j + 1) * n, w:2 * w] = a[n:2 * n].astype(bf16)
    z = _dot(scr_ref[...], cd_ref[...])
    twr = jnp.concatenate([twr_ref[...]] * (w // LANES), axis=1)
    twi = jnp.concatenate([twi_ref[...]] * (w // LANES), axis=1)
    zr, zi = z[:, :w], z[:, w:]
    br = zr * twr - zi * twi
    bi = zr * twi + zi * twr
    for j in range(FFT_CHUNK):
        o_ref[0, j, :, 0:w] = br[j * n:(j + 1) * n]
        o_ref[0, j, :, w:2 * w] = bi[j * n:(j + 1) * n]


def _fft_s2_kernel(b_ref, cs_ref, o_ref, *, scale):
    n, w = FFT_N, FNET_W
    for j in range(FFT_CHUNK):
        r = _dot(cs_ref[...], b_ref[0, :, j, :].astype(bf16))
        o_ref[0, :, j, :] = (r[0:n, 0:w] + r[n:2 * n, w:2 * w]) * scale


def _fft_ct_mats():
    n, w = FFT_N, FNET_W
    k = np.arange(n)
    ang = 2.0 * np.pi * ((k[:, None] * k[None, :]) % n) / n
    c, s = np.cos(ang), np.sin(ang)
    g = FNET_GROUP_DIM
    kg = np.arange(g)
    ang_g = 2.0 * np.pi * ((kg[:, None] * kg[None, :]) % g) / g
    eye = np.eye(w // g)
    cbd, sbd = np.kron(eye, np.cos(ang_g)), np.kron(eye, np.sin(ang_g))
    cs1 = np.concatenate([c, -s], axis=0)
    cd = np.block([[cbd, -sbd], [sbd, cbd]])
    cs2 = np.concatenate([c, s], axis=0)
    t2 = np.arange(n)[:, None]
    k1 = np.arange(n)[None, :]
    tw = 2.0 * np.pi * ((t2 * k1) % (n * n)) / (n * n)
    twr = np.broadcast_to(np.cos(tw).reshape(n * n, 1), (n * n, LANES))
    twi = np.broadcast_to(-np.sin(tw).reshape(n * n, 1), (n * n, LANES))
    as_bf = lambda m: jnp.asarray(m, f32).astype(bf16)
    return as_bf(cs1), as_bf(cd), as_bf(cs2), jnp.asarray(twr, f32), jnp.asarray(twi, f32)


def _fft_ct_call(pf, mats):
    B, T, w = pf.shape
    n, ch = FFT_N, FFT_CHUNK
    cs1, cd, cs2, twr, twi = mats
    full = lambda a: pl.BlockSpec(a.shape, lambda b, i: (0,) * a.ndim)
    stage1 = pl.pallas_call(
        _fft_s1_kernel,
        grid=(B, n // ch),
        in_specs=[pl.BlockSpec((1, n, ch, w), lambda b, i: (b, 0, i, 0)), full(cs1), full(cd),
                  pl.BlockSpec((ch * n, LANES), lambda b, i: (i, 0)),
                  pl.BlockSpec((ch * n, LANES), lambda b, i: (i, 0))],
        out_specs=pl.BlockSpec((1, ch, n, 2 * w), lambda b, i: (b, i, 0, 0)),
        out_shape=jax.ShapeDtypeStruct((B, n, n, 2 * w), f32),
        scratch_shapes=[pltpu.VMEM((ch * n, 2 * w), bf16)],
        compiler_params=_cparams(("parallel", "parallel")),
        name="fourier_stage1",
    )(pf.reshape(B, n, n, w), cs1, cd, twr, twi)
    kern2 = functools.partial(_fft_s2_kernel, scale=1.0 / math.sqrt(T * FNET_GROUP_DIM))
    out = pl.pallas_call(
        kern2,
        grid=(B, n // ch),
        in_specs=[pl.BlockSpec((1, n, ch, 2 * w), lambda b, i: (b, 0, i, 0)), full(cs2)],
        out_specs=pl.BlockSpec((1, n, ch, w), lambda b, i: (b, 0, i, 0)),
        out_shape=jax.ShapeDtypeStruct((B, n, n, w), f32),
        compiler_params=_cparams(("parallel", "parallel")),
        name="fourier_stage2",
    )(stage1, cs2)
    return out.reshape(B, T, w)


def _lane_is_low(shape):
    return lax.broadcasted_iota(jnp.int32, shape, len(shape) - 1) < HEAD_DIM


def _softmax_pv_steps(make_logits, values, extra=None):
    def fold(x, op, acc):
        for c in range(x.shape[1] // LANES):
            blk = x[:, c * LANES:(c + 1) * LANES]
            acc = blk if acc is None else op(acc, blk)
        return acc

    logits = make_logits()
    yield
    m_el = None
    for s in logits:
        m_el = fold(s, jnp.maximum, m_el)
    m = m_el.max(axis=-1, keepdims=True)
    if extra is not None:
        m = jnp.maximum(m, extra)
    yield
    d_el = None
    out = None
    for s, v in zip(logits, values):
        p = jnp.exp(s - m)
        d_el = fold(p, jnp.add, d_el)
        o = _dot(p.astype(bf16), v)
        out = o if out is None else out + o
        yield
    den = d_el.sum(axis=-1, keepdims=True)
    if extra is not None:
        den = den + jnp.exp(extra - m)
    return out * (1.0 / den)


def _run_together(gens):
    results = [None] * len(gens)
    live = list(enumerate(gens))
    while live:
        for item in list(live):
            i, g = item
            try:
                next(g)
            except StopIteration as stop:
                results[i] = stop.value
                live.remove(item)
    return results


def _softmax_pv(logits, values, extra=None):
    return _run_together([_softmax_pv_steps(lambda: logits, values, extra)])[0]


NAT_QROWS = 4
NAT_GROUPS = 8
NAT_BAND = NAT_QROWS + NAT_KH
NAT_TQ = NAT_QROWS * GRID_W
NAT_TK = NAT_BAND * GRID_W
NAT_ROWS = 64
NAT_EDGE_GROUPS = -(-(NAT_KH // 2) // NAT_QROWS)
NQ_COL, NK_COL, NV_COL = 2, 3, 4


def _nat_band_start(group):
    return jnp.clip(group * NAT_QROWS - NAT_KH // 2, 0, NAT_ROWS - NAT_BAND)


def _nat_kernel(q_ref, k_ref, v_ref, kc_ref, vc_ref, bias_ref, o_ref):
    j = pl.program_id(1)
    ngroups = NAT_ROWS // NAT_QROWS
    low = _lane_is_low((NAT_TQ, LANES))
    for g in range(NAT_GROUPS):
        group = j * NAT_GROUPS + g
        start = pl.multiple_of(_nat_band_start(group) * GRID_W, NAT_QROWS * GRID_W)
        var = jnp.where(group < NAT_EDGE_GROUPS, group + 1,
                        jnp.where(group >= ngroups - NAT_EDGE_GROUPS,
                                  group - (ngroups - NAT_EDGE_GROUPS) + NAT_EDGE_GROUPS + 1, 0))
        rows = slice(g * NAT_TQ, (g + 1) * NAT_TQ)
        heads = []
        for c in range(NAT_HEADS // 2):
            cols = slice(c * LANES, (c + 1) * LANES)
            q2 = q_ref[0, rows, cols] * 0.125
            k2 = k_ref[0, pl.ds(start, NAT_TK), cols]
            v2 = v_ref[0, pl.ds(start, NAT_TK), cols]
            kc2 = kc_ref[0, :, cols]
            vc2 = vc_ref[0, :, cols]

            def logits_of(e, q2=q2, k2=k2, kc2=kc2, c=c):
                qm = jnp.where(low if e == 0 else jnp.logical_not(low), q2, jnp.zeros_like(q2))
                return [_dot_nt(qm, k2) + bias_ref[var, 2 * c + e], _dot_nt(qm, kc2)]

            heads += [_softmax_pv_steps(functools.partial(logits_of, e), [v2, vc2]) for e in range(2)]
        outs = _run_together(heads)
        for c in range(NAT_HEADS // 2):
            cols = slice(c * LANES, (c + 1) * LANES)
            o_ref[0, rows, cols] = jnp.where(low, outs[2 * c], outs[2 * c + 1]).astype(o_ref.dtype)


def _nat_bias_tables(rpb):
    rows = NAT_ROWS
    ngroups = rows // NAT_QROWS
    reps = [NAT_EDGE_GROUPS] + list(range(NAT_EDGE_GROUPS)) + list(range(ngroups - NAT_EDGE_GROUPS, ngroups))
    nv = len(reps)
    a = np.arange(NAT_QROWS)[:, None]
    m = np.arange(NAT_BAND)[None, :]
    sel_r = np.zeros((nv, NAT_QROWS, NAT_BAND, 2 * NAT_KH - 1), np.float32)
    ok_r = np.zeros((nv, NAT_QROWS, NAT_BAND), bool)
    for v, jj in enumerate(reps):
        band0 = int(np.clip(jj * NAT_QROWS - NAT_KH // 2, 0, rows - NAT_BAND))
        qr = jj * NAT_QROWS + a
        kr = band0 + m
        rstart = np.clip(qr - NAT_KH // 2, 0, rows - NAT_KH)
        ok = (kr >= rstart) & (kr < rstart + NAT_KH)
        dr = np.clip(kr - qr + NAT_KH - 1, 0, 2 * NAT_KH - 2)
        ok_r[v] = ok
        sel_r[v] = np.eye(2 * NAT_KH - 1, dtype=np.float32)[dr] * ok[..., None]
    qc = np.arange(GRID_W)[:, None]
    kc = np.arange(GRID_W)[None, :]
    cstart = np.clip(qc - NAT_KW // 2, 0, GRID_W - NAT_KW)
    ok_c = (kc >= cstart) & (kc < cstart + NAT_KW)
    dc = np.clip(kc - qc + NAT_KW - 1, 0, 2 * NAT_KW - 2)
    sel_c = np.eye(2 * NAT_KW - 1, dtype=np.float32)[dc] * ok_c[..., None]
    valid = ok_r[:, :, None, :, None] & ok_c[None, None, :, None, :]
    valid = valid.reshape(nv, 1, NAT_TQ, NAT_TK)
    t = jnp.einsum('vamr,hrc,qkc->vhaqmk', jnp.asarray(sel_r), rpb.astype(f32), jnp.asarray(sel_c),
                   precision=lax.Precision.HIGHEST)
    t = t.reshape(nv, NAT_HEADS, NAT_TQ, NAT_TK)
    return jnp.where(jnp.asarray(valid), t, NEG_INF)


def _nat_call(p, pc, bias):
    B, T, _ = p.shape
    assert T == NAT_ROWS * GRID_W
    L = pc.shape[1]
    W = NAT_HEADS * HEAD_DIM
    tq = NAT_GROUPS * NAT_TQ
    return pl.pallas_call(
        _nat_kernel,
        grid=(B, T // tq),
        in_specs=[pl.BlockSpec((1, tq, W), lambda b, j: (b, j, NQ_COL)),
                  pl.BlockSpec((1, T, W), lambda b, j: (b, 0, NK_COL)),
                  pl.BlockSpec((1, T, W), lambda b, j: (b, 0, NV_COL)),
                  pl.BlockSpec((1, L, W), lambda b, j: (b, 0, NK_COL)),
                  pl.BlockSpec((1, L, W), lambda b, j: (b, 0, NV_COL)),
                  pl.BlockSpec(bias.shape, lambda b, j: (0, 0, 0, 0), pipeline_mode=pl.Buffered(1))],
        out_specs=pl.BlockSpec((1, tq, W), lambda b, j: (b, j, 0)),
        out_shape=jax.ShapeDtypeStruct((B, T, W), bf16),
        compiler_params=_cparams(("parallel", "arbitrary")),
        name="nat_attention",
    )(p, p, p, pc, pc, bias)


GQ_COL = 1536 // 512
GK_COL = 2048 // LANES
GV_COL = 2176 // LANES
GQA_R = GQA_Q_HEADS // GQA_KV_HEADS


def _rope(x, cos, sin):
    lane = lax.broadcasted_iota(jnp.int32, x.shape, 1)
    first = (lane % (HEAD_DIM // 2)) < (HEAD_DIM // 4)
    swapped = jnp.where(first, pltpu.roll(x, LANES - HEAD_DIM // 4, 1), pltpu.roll(x, HEAD_DIM // 4, 1))
    return x * cos + swapped * sin


def _rope_tables(S):
    t = np.arange(S)
    pos = np.stack([t // GRID_W, t % GRID_W], axis=1).astype(np.float64)
    quarter = HEAD_DIM // 4
    freqs = ROPE_BASE ** (-np.arange(quarter, dtype=np.float64) / quarter)
    lane = np.arange(LANES)
    which = (lane % HEAD_DIM) // (HEAD_DIM // 2)
    ang = pos[:, which] * freqs[lane % quarter][None, :]
    sign = np.where((lane % (HEAD_DIM // 2)) < quarter, -1.0, 1.0)[None, :]
    return jnp.asarray(np.cos(ang), f32), jnp.asarray(np.sin(ang) * sign, f32)


GQA_QB = 8


def _gqa_kernel(q_ref, kvp_ref, kvm_ref, kvn_ref, kvx_ref, tp_ref, tm_ref, tn_ref, sink_ref, band_ref, o_ref, *, nsteps):
    n = pl.program_id(1)
    blk = GQA_BLOCK
    rows = GQA_R * blk
    k_of = lambda ref, r0=None: ref[0, :, 0:LANES] if r0 is None else ref[0, r0:r0 + blk, 0:LANES]
    v_of = lambda ref, r0=None: ref[0, :, LANES:2 * LANES] if r0 is None else ref[0, r0:r0 + blk, LANES:2 * LANES]
    rope_k = lambda k, t: _rope(k.astype(f32), t[:, 0:LANES], t[:, LANES:2 * LANES]).astype(bf16)
    tmid = [tm_ref[i * blk:(i + 1) * blk, :] for i in range(GQA_QB)]
    kb = [rope_k(k_of(kvp_ref), tp_ref[...])] + [rope_k(k_of(kvm_ref, i * blk), tmid[i]) for i in range(GQA_QB)] \
        + [rope_k(k_of(kvn_ref), tn_ref[...])]
    vb = [v_of(kvp_ref)] + [v_of(kvm_ref, i * blk) for i in range(GQA_QB)] + [v_of(kvn_ref)]
    kx, vx = kvx_ref[0, :, 0:LANES], kvx_ref[0, :, LANES:2 * LANES]
    low = _lane_is_low((rows, LANES))
    for i in range(GQA_QB):
        cos_q, sin_q = tmid[i][:, 0:LANES], tmid[i][:, LANES:2 * LANES]
        qs = [(_rope(q_ref[0, i * blk:(i + 1) * blk, r * LANES:(r + 1) * LANES].astype(f32), cos_q, sin_q) * 0.125)
              for r in range(GQA_R)]
        q_all = jnp.concatenate(qs, axis=0)
        has_prev = (n > 0) if i == 0 else True
        has_next = (n < nsteps - 1) if i == GQA_QB - 1 else True
        bias_p = band_ref[0] if has_prev is True else jnp.minimum(band_ref[0], jnp.where(has_prev, 0.0, NEG_INF))
        bias_n = band_ref[1] if has_next is True else jnp.minimum(band_ref[1], jnp.where(has_next, 0.0, NEG_INF))
        halves = []
        for g in range(GQA_KV_HEADS):
            qm = jnp.where(low if g == 0 else jnp.logical_not(low), q_all, 0.0).astype(bf16)
            logits = [_dot_nt(qm, kb[i]) + bias_p, _dot_nt(qm, kb[i + 1]), _dot_nt(qm, kb[i + 2]) + bias_n,
                      _dot_nt(qm, kx)]
            vals = [vb[i], vb[i + 1], vb[i + 2], vx]
            halves.append(_softmax_pv(logits, vals, extra=sink_ref[g][:, 0:1]))
        out = jnp.where(low, halves[0], halves[1])
        for r in range(GQA_R):
            o_ref[0, i * blk:(i + 1) * blk, r * LANES:(r + 1) * LANES] = out[r * blk:(r + 1) * blk].astype(o_ref.dtype)


def _sink_cols(sink, blk):
    s = sink.astype(f32).reshape(GQA_KV_HEADS, GQA_R, 1, 1)
    return jnp.broadcast_to(s, (GQA_KV_HEADS, GQA_R, blk, LANES)).reshape(GQA_KV_HEADS, GQA_R * blk, LANES)


def _gqa_band_masks():
    qi = np.arange(GQA_R * GQA_BLOCK)[:, None] % GQA_BLOCK
    kj = np.arange(GQA_BLOCK)[None, :]
    return jnp.asarray(np.stack([np.where(kj >= qi, 0.0, NEG_INF), np.where(kj <= qi, 0.0, NEG_INF)]), f32)


def _gqa_call(p, pc, cos, sin, sink):
    B, T, _ = p.shape
    L = pc.shape[1]
    blk = GQA_BLOCK
    nb = T // blk
    nsteps = nb // GQA_QB
    QW = GQA_Q_HEADS * HEAD_DIM
    kvw = 2 * LANES
    kv_col = (GK_COL * LANES) // kvw
    assert GV_COL == GK_COL + 1 and GK_COL % 2 == 0
    tabs = jnp.concatenate([cos, sin], axis=1)
    kern = functools.partial(_gqa_kernel, nsteps=nsteps)
    prev = lambda n: jnp.maximum(n * GQA_QB - 1, 0)
    nxt = lambda n: jnp.minimum((n + 1) * GQA_QB, nb - 1)
    once = dict(pipeline_mode=pl.Buffered(1))
    return pl.pallas_call(
        kern,
        grid=(B, nsteps),
        in_specs=[pl.BlockSpec((1, GQA_QB * blk, QW), lambda b, n: (b, n, GQ_COL)),
                  pl.BlockSpec((1, blk, kvw), lambda b, n: (b, prev(n), kv_col)),
                  pl.BlockSpec((1, GQA_QB * blk, kvw), lambda b, n: (b, n, kv_col)),
                  pl.BlockSpec((1, blk, kvw), lambda b, n: (b, nxt(n), kv_col)),
                  pl.BlockSpec((1, L, kvw), lambda b, n: (b, 0, kv_col)),
                  pl.BlockSpec((blk, kvw), lambda b, n: (prev(n), 0)),
                  pl.BlockSpec((GQA_QB * blk, kvw), lambda b, n: (n, 0)),
                  pl.BlockSpec((blk, kvw), lambda b, n: (nxt(n), 0)),
                  pl.BlockSpec((GQA_KV_HEADS, GQA_R * blk, LANES), lambda b, n: (0, 0, 0), **once),
                  pl.BlockSpec((2, GQA_R * blk, blk), lambda b, n: (0, 0, 0), **once)],
        out_specs=pl.BlockSpec((1, GQA_QB * blk, QW), lambda b, n: (b, n, 0)),
        out_shape=jax.ShapeDtypeStruct((B, T, QW), bf16),
        compiler_params=_cparams(("parallel", "parallel")),
        name="window_gqa",
    )(p, p, p, p, pc, tabs, tabs, tabs, _sink_cols(sink, blk), _gqa_band_masks())


def _ctx_attn_kernel(nq_ref, nk_ref, nv_ref, gq_ref, gk_ref, gv_ref, sink_ref, ob_ref, od_ref, *, L):
    low = _lane_is_low((L, LANES))
    for c in range(NAT_HEADS // 2):
        cols = slice(c * LANES, (c + 1) * LANES)
        q2 = nq_ref[0, :, cols] * 0.125
        halves = []
        for e in range(2):
            qm = jnp.where(low if e == 0 else jnp.logical_not(low), q2, jnp.zeros_like(q2))
            halves.append(_softmax_pv([_dot_nt(qm, nk_ref[0, :, cols])], [nv_ref[0, :, cols]]))
        ob_ref[0, :, cols] = jnp.where(low, halves[0], halves[1]).astype(ob_ref.dtype)
    q_all = jnp.concatenate([gq_ref[0, :, r * LANES:(r + 1) * LANES] for r in range(GQA_R)], axis=0) * 0.125
    low4 = _lane_is_low((GQA_R * L, LANES))
    halves = []
    for g in range(GQA_KV_HEADS):
        qm = jnp.where(low4 if g == 0 else jnp.logical_not(low4), q_all, jnp.zeros_like(q_all))
        halves.append(_softmax_pv([_dot_nt(qm, gk_ref[0])], [gv_ref[0]], extra=sink_ref[g][:, 0:1]))
    out = jnp.where(low4, halves[0], halves[1])
    for r in range(GQA_R):
        od_ref[0, :, r * LANES:(r + 1) * LANES] = out[r * L:(r + 1) * L].astype(od_ref.dtype)


def _ctx_attn_call(pc, sink):
    B, L, _ = pc.shape
    W = NAT_HEADS * HEAD_DIM
    QW = GQA_Q_HEADS * HEAD_DIM
    kern = functools.partial(_ctx_attn_kernel, L=L)
    return pl.pallas_call(
        kern,
        grid=(B,),
        in_specs=[pl.BlockSpec((1, L, W), lambda b: (b, 0, NQ_COL)),
                  pl.BlockSpec((1, L, W), lambda b: (b, 0, NK_COL)),
                  pl.BlockSpec((1, L, W), lambda b: (b, 0, NV_COL)),
                  pl.BlockSpec((1, L, QW), lambda b: (b, 0, GQ_COL)),
                  pl.BlockSpec((1, L, LANES), lambda b: (b, 0, GK_COL)),
                  pl.BlockSpec((1, L, LANES), lambda b: (b, 0, GV_COL)),
                  pl.BlockSpec((GQA_KV_HEADS, GQA_R * L, LANES), lambda b: (0, 0, 0))],
        out_specs=[pl.BlockSpec((1, L, W), lambda b: (b, 0, 0)),
                   pl.BlockSpec((1, L, QW), lambda b: (b, 0, 0))],
        out_shape=[jax.ShapeDtypeStruct((B, L, W), bf16), jax.ShapeDtypeStruct((B, L, QW), bf16)],
        compiler_params=_cparams(("parallel",)),
        name="context_attention",
    )(pc, pc, pc, pc, pc, pc, _sink_cols(sink, L))


def _merge_kernel(x_ref, sh_ref, sc_ref, g1_ref, ya_ref, yb_ref, yc_ref, yd_ref, wg_ref, bg_ref,
                  wa_ref, wb_ref, wc_ref, wd_ref, wo_ref, lg_ref, lb_ref, o_ref):
    D = D_MODEL
    x = x_ref[0]
    xm = (_ln(x) * (1.0 + sc_ref[0]) + sh_ref[0]).astype(bf16)
    m = None
    for j, (y_ref, w_ref) in enumerate(((ya_ref, wa_ref), (yb_ref, wb_ref), (yc_ref, wc_ref), (yd_ref, wd_ref))):
        gate = jax.nn.sigmoid(_dot(xm, wg_ref[:, j * D:(j + 1) * D]) + bg_ref[:, j * D:(j + 1) * D])
        t = gate * _dot(y_ref[0].astype(bf16), w_ref[...])
        m = t if m is None else m + t
    mix = _dot(m.astype(bf16), wo_ref[...])
    z = DEEPNORM_ALPHA * x + g1_ref[0] * mix
    o_ref[0] = _ln(z) * lg_ref[...] + lb_ref[...]


MERGE_TM = 512


def _merge_call(x, sh, sc, g1, ya, yb, yc, yd, wg, bg, wa, wb, wc, wd, wo, lg, lb, tm):
    B, T, D = x.shape
    tok = lambda w: pl.BlockSpec((1, tm, w), lambda b, i: (b, i, 0))
    mod = pl.BlockSpec((1, 1, D), lambda b, i: (b, 0, 0))
    full = lambda a: pl.BlockSpec(a.shape, lambda b, i: (0,) * a.ndim, pipeline_mode=pl.Buffered(1))
    return pl.pallas_call(
        _merge_kernel,
        grid=(B, T // tm),
        in_specs=[tok(D), mod, mod, mod, tok(ya.shape[-1]), tok(yb.shape[-1]), tok(yc.shape[-1]), tok(yd.shape[-1]),
                  full(wg), full(bg), full(wa), full(wb), full(wc), full(wd), full(wo), full(lg), full(lb)],
        out_specs=tok(D),
        out_shape=jax.ShapeDtypeStruct((B, T, D), f32),
        compiler_params=_cparams(("parallel", "parallel")),
        name="merge_ln",
    )(x, sh, sc, g1, ya, yb, yc, yd, wg, bg, wa, wb, wc, wd, wo, lg, lb)


PEER_TM = 512
PEER_EBLK = 16
PEER_SEL_HEADS = 4
PEER_DENSE_J = 4
BIG_NEG = -3.0e38
SQRT_HALF = 0.7071067811865476


def _top_values(s, count, rank_below=None):
    vals = []
    cur = s
    rank = None if rank_below is None else jnp.full(s.shape, float(rank_below), f32)
    for r in range(count):
        m = jnp.max(cur, axis=0, keepdims=True)
        vals.append(m)
        hit = cur == m
        if rank is not None and r < rank_below:
            rank = jnp.where(hit, float(r), rank)
        if r + 1 < count:
            cur = jnp.where(hit, BIG_NEG, cur)
        yield
    return vals, rank


def _peer_select_chunk(h, s1t, s2t, ln, n1_ref, e1_ref, r2_ref, e2_ref):
    k = PEER_TOPK
    pairs = [(i, j) for i in range(k) for j in range(k) if (i + 1) * (j + 1) <= k]
    pad = (-len(pairs)) % SUBLANES
    top_ranks = [a for a in range(k) if (a + 1) * (PEER_DENSE_J + 1) <= k]
    a, _ = yield from _top_values(s1t, k)
    b, rank2 = yield from _top_values(s2t, k, rank_below=k)
    r2_ref[h, :, ln] = rank2.astype(bf16)
    e2_ref[h, :, ln] = jnp.exp(s2t - b[0]).astype(bf16)
    yield
    cand = jnp.concatenate([a[i] + b[j] for i, j in pairs]
                           + [jnp.full((pad, LANES), BIG_NEG, f32)], axis=0)
    top, _ = yield from _top_values(cand, k)
    tau = top[k - 1]
    z = jnp.sum(jnp.where(cand >= tau, jnp.exp(cand - (a[0] + b[0])), 0.0), axis=0, keepdims=True)
    yield
    n1 = jnp.zeros_like(s1t)
    for j in range(PEER_DENSE_J):
        n1 = n1 + jnp.where(s1t + b[j] >= tau, 1.0, 0.0)
        yield
    for r in top_ranks:
        extra = jnp.zeros_like(tau)
        for j in range(PEER_DENSE_J, k // (r + 1)):
            extra = extra + jnp.where(a[r] + b[j] >= tau, 1.0, 0.0)
        n1 = n1 + jnp.where(s1t == a[r], extra, 0.0)
        yield
    n1_ref[h, :, ln] = n1
    e1_ref[h, :, ln] = jnp.exp(s1t - a[0]) * (0.5 / z)
    yield


def _peer_select_lanes(h, off, nlanes, ut, wq_ref, k1_ref, k2_ref, n1_ref, e1_ref, r2_ref, e2_ref):
    nk = PEER_N_KEYS
    r0 = pl.multiple_of(h * 2 * nk, 2 * nk)
    u_cols = ut[:, pl.ds(off, nlanes)]
    q1 = _dot(wq_ref[pl.ds(r0, nk), :], u_cols).astype(bf16)
    yield
    q2 = _dot(wq_ref[pl.ds(r0 + nk, nk), :], u_cols).astype(bf16)
    yield
    s1 = _dot(k1_ref[h], q1)
    s2 = _dot(k2_ref[h], q2)
    yield
    chunks = [_peer_select_chunk(h, s1[:, t * LANES:(t + 1) * LANES], s2[:, t * LANES:(t + 1) * LANES],
                                 pl.ds(off + t * LANES, LANES), n1_ref, e1_ref, r2_ref, e2_ref)
              for t in range(nlanes // LANES)]
    while chunks:
        for g in list(chunks):
            try:
                next(g)
                yield
            except StopIteration:
                chunks.remove(g)


def _advance(gen, steps=None):
    n = 0
    for _ in gen:
        n += 1
        if steps is not None and n >= steps:
            break


def _peer_kernel(x_ref, sh_ref, sc_ref, g2_ref, wq_ref, k1_ref, k2_ref, eu_ref, ev_ref, lg_ref, lb_ref,
                 o_ref, ut_ref, n1_ref, e1_ref, r2_ref, e2_ref, acc_ref, *, tm, nsteps):
    step = pl.program_id(2)
    nk = PEER_N_KEYS

    @pl.when(step == 0)
    def _select():
        u = _ln(x_ref[0]) * (1.0 + sc_ref[0]) + sh_ref[0]
        ut_ref[...] = u.T.astype(bf16)
        acc_ref[...] = jnp.zeros_like(acc_ref)

        def heads(i, carry):
            _run_together([_peer_select_lanes(i * PEER_SEL_HEADS + d, 0, tm, ut_ref, wq_ref, k1_ref, k2_ref,
                                              n1_ref, e1_ref, r2_ref, e2_ref) for d in range(PEER_SEL_HEADS)])
            return carry

        lax.fori_loop(0, PEER_HEADS // PEER_SEL_HEADS, heads, 0)

    gs = []
    for b in range(PEER_EBLK):
        i1 = step * PEER_EBLK + b
        ht = _dot(eu_ref[b * nk:(b + 1) * nk, :], ut_ref[...])
        w = jnp.zeros((nk, tm), bf16)
        for h in range(PEER_HEADS):
            n1 = n1_ref[h, pl.ds(i1, 1), :].astype(bf16)
            e1 = e1_ref[h, pl.ds(i1, 1), :].astype(bf16)
            w = w + jnp.where(r2_ref[h] < n1, e2_ref[h] * e1, jnp.zeros((), bf16))
        act = ht * (1.0 + lax.erf(ht * SQRT_HALF))
        gs.append(w * act.astype(bf16))
    acc_ref[...] += _dot(ev_ref[...], jnp.concatenate(gs, axis=0))

    @pl.when(step == nsteps - 1)
    def _finish():
        z = DEEPNORM_ALPHA * x_ref[0] + g2_ref[0] * acc_ref[...].T
        o_ref[0] = _ln(z) * lg_ref[...] + lb_ref[...]


def _peer_call(x, sh, sc, g2, wq_t, k1, k2, eu, ev_t, lg, lb):
    B, T, D = x.shape
    tm = PEER_TM
    nk = PEER_N_KEYS
    ne = eu.shape[0]
    eb = PEER_EBLK * nk
    nsteps = ne // eb
    kern = functools.partial(_peer_kernel, tm=tm, nsteps=nsteps)
    mod = pl.BlockSpec((1, 1, D), lambda b, i, s: (b, 0, 0))
    full = lambda a, **kw: pl.BlockSpec(a.shape, lambda b, i, s: (0,) * a.ndim, **kw)
    sel = lambda dt: pltpu.VMEM((PEER_HEADS, nk, tm), dt)
    return pl.pallas_call(
        kern,
        grid=(B, T // tm, nsteps),
        in_specs=[pl.BlockSpec((1, tm, D), lambda b, i, s: (b, i, 0)), mod, mod, mod,
                  full(wq_t, pipeline_mode=pl.Buffered(1)), full(k1), full(k2),
                  pl.BlockSpec((eb, D), lambda b, i, s: (s, 0)),
                  pl.BlockSpec((D, eb), lambda b, i, s: (0, s)),
                  full(lg), full(lb)],
        out_specs=pl.BlockSpec((1, tm, D), lambda b, i, s: (b, i, 0)),
        out_shape=jax.ShapeDtypeStruct((B, T, D), f32),
        scratch_shapes=[pltpu.VMEM((D, tm), bf16), sel(f32), sel(f32), sel(bf16), sel(bf16),
                        pltpu.VMEM((D, tm), f32)],
        compiler_params=_cparams(("parallel", "parallel", "arbitrary")),
        name="peer_ffn",
    )(x, sh, sc, g2, wq_t, k1, k2, eu, ev_t, lg, lb)


def _gqa_head_perm():
    cols = []
    for c in range(GQA_R):
        cols += list(range(c * HEAD_DIM, (c + 1) * HEAD_DIM))
        cols += list(range((GQA_R + c) * HEAD_DIM, (GQA_R + c + 1) * HEAD_DIM))
    return np.asarray(cols)


def kernel(x, c, ctx, c_ctx, w_ada, b_ada, w_in, b_in, conv_w, conv_b, conv_ln_g, conv_ln_b, nat_rpb, gqa_sink,
           w_branch_a, w_branch_b, w_branch_c, w_branch_d, w_out, ln1_g, ln1_b, peer_wq, peer_k1, peer_k2,
           peer_u, peer_v, ln2_g, ln2_b):
    B, S, D = x.shape
    L = ctx.shape[1]
    gq0, gq1 = 1536, 2048
    perm = _gqa_head_perm()

    cvec = jnp.concatenate([c, c_ctx[None, :], jnp.zeros((8 - B - 1, D), f32)], axis=0)
    mod = _ada_call(cvec, w_ada, b_ada)
    cos, sin = _rope_tables(S)
    assert S == FFT_N * FFT_N
    ct_mats = _fft_ct_mats()
    cs_ctx = _dft_time_mats(L)
    bd = _dft_chan_mats()
    row = lambda v: v.reshape(1, -1)

    xc = ctx.reshape(1, B * L, D)
    for i in range(DEPTH):
        lat = [mod[i, :B, k * D:(k + 1) * D].reshape(B, 1, D) for k in range(6)]
        con = [mod[i, B:B + 1, k * D:(k + 1) * D].reshape(1, 1, D) for k in range(6)]
        wi, bi = w_in[i], b_in[i]
        w_small = jnp.concatenate([wi[:, :gq0], wi[:, gq0:gq1][:, perm], wi[:, gq1:N_SMALL]], axis=1).astype(bf16)
        b_small = row(jnp.concatenate([bi[:gq0], bi[gq0:gq1][perm], bi[gq1:N_SMALL]]))
        w_gate = wi[:, N_SMALL:].astype(bf16)
        b_gate = row(bi[N_SMALL:])
        wa, wb, wc = w_branch_a[i].astype(bf16), w_branch_b[i].astype(bf16), w_branch_c[i].astype(bf16)
        wd = w_branch_d[i][perm].astype(bf16)
        wo = w_out[i].astype(bf16)
        merge_w = (w_gate, b_gate, wa, wb, wc, wd, wo, row(ln1_g[i]), row(ln1_b[i]))
        peer_w = (peer_wq[i].T.astype(bf16), peer_k1[i].astype(bf16), peer_k2[i].astype(bf16),
                  peer_u[i].astype(bf16), peer_v[i].T.astype(bf16), row(ln2_g[i]), row(ln2_b[i]))
        conv_p = (conv_w[i], conv_b[i], conv_ln_g[i], conv_ln_b[i])

        pc, pcf = _inproj_call(xc, con[0], con[1], w_small, b_small, 512)
        pc, pcf = pc.reshape(B, L, N_SMALL), pcf.reshape(B, L, FNET_W)
        p, pf = _inproj_call(x, lat[0], lat[1], w_small, b_small, 512)
        y_a = _conv_call(p, *conv_p)
        y_b = _nat_call(p, pc, _nat_bias_tables(nat_rpb[i]))
        y_c = _fft_ct_call(pf, ct_mats)
        y_d = _gqa_call(p, pc, cos, sin, gqa_sink[i])
        x = _merge_call(x, lat[0], lat[1], lat[2], y_a, y_b, y_c, y_d, *merge_w, MERGE_TM)
        x = _peer_call(x, lat[3], lat[4], lat[5], *peer_w)

        if i < DEPTH - 1:
            yc_a = _conv_call(pc, *conv_p)
            yc_b, yc_d = _ctx_attn_call(pc, gqa_sink[i])
            yc_c = _fft_call(pcf, cs_ctx, bd)
            flat = lambda a: a.reshape(1, B * L, a.shape[-1])
            xc = _merge_call(xc, con[0], con[1], con[2], flat(yc_a), flat(yc_b), flat(yc_c), flat(yc_d),
                             *merge_w, MERGE_TM)
            xc = _peer_call(xc, con[3], con[4], con[5], *peer_w)
    return x
```

```python
import functools
import math

import numpy as np
import jax
import jax.numpy as jnp
from jax import lax
from jax.experimental import pallas as pl
from jax.experimental.pallas import tpu as pltpu

f32 = jnp.float32
bf16 = jnp.bfloat16

D_MODEL = 1024
DEPTH = 2
GRID_W = 64
CONV_CH = 256
CONV_WIDTH = 31
NAT_HEADS = 4
NAT_KH = 8
NAT_KW = 16
GQA_Q_HEADS = 8
GQA_KV_HEADS = 2
GQA_BLOCK = 128
HEAD_DIM = 64
ROPE_BASE = 10000.0
PEER_HEADS = 8
PEER_N_KEYS = 128
PEER_TOPK = 16
LN_EPS = 1e-5
NEG_INF = -1e30
DEEPNORM_ALPHA = (2 * DEPTH) ** 0.25

LANES = 128
SUBLANES = 8
N_SMALL = 2304
VMEM_LIMIT = 56 * 1024 * 1024


def _cparams(sem, vmem=VMEM_LIMIT):
    return pltpu.CompilerParams(dimension_semantics=sem, vmem_limit_bytes=vmem)


def _ln(x):
    mu = jnp.mean(x, axis=-1, keepdims=True)
    xc = x - mu
    var = jnp.mean(xc * xc, axis=-1, keepdims=True)
    return xc * lax.rsqrt(var + LN_EPS)


def _dot(a, b):
    return jnp.dot(a, b, preferred_element_type=f32)


def _dot_nt(a, b):
    return lax.dot_general(a, b, (((1,), (1,)), ((), ())), preferred_element_type=f32)


def _ada_kernel(c_ref, w_ref, b_ref, o_ref):
    c = c_ref[...]
    h = c * jax.nn.sigmoid(c)
    o_ref[0] = jnp.dot(h, w_ref[0], preferred_element_type=f32,
                       precision=lax.Precision.HIGHEST) + b_ref[0]


ADA_TN = 3072


def _ada_call(cvec, w_ada, b_ada):
    L, D, N = w_ada.shape
    tn = ADA_TN
    return pl.pallas_call(
        _ada_kernel,
        grid=(L, N // tn),
        in_specs=[pl.BlockSpec((8, D), lambda l, j: (0, 0)),
                  pl.BlockSpec((1, D, tn), lambda l, j: (l, 0, j)),
                  pl.BlockSpec((1, 1, tn), lambda l, j: (l, 0, j))],
        out_specs=pl.BlockSpec((1, 8, tn), lambda l, j: (l, 0, j)),
        out_shape=jax.ShapeDtypeStruct((L, 8, N), f32),
        compiler_params=_cparams(("parallel", "parallel")),
        name="ada_mod",
    )(cvec, w_ada, b_ada.reshape(L, 1, N))


F_OFF = 1280
FNET_W = 256


def _inproj_kernel(x_ref, sh_ref, sc_ref, w_ref, b_ref, o_ref, of_ref):
    xm = _ln(x_ref[0]) * (1.0 + sc_ref[0]) + sh_ref[0]
    y = _dot(xm.astype(bf16), w_ref[...]) + b_ref[...]
    o_ref[0] = y.astype(o_ref.dtype)
    of_ref[0] = y[:, F_OFF:F_OFF + FNET_W]


INPROJ_TM = 1024


def _inproj_call(x, sh, sc, w, b, tm):
    B, T, D = x.shape
    N = w.shape[1]
    return pl.pallas_call(
        _inproj_kernel,
        grid=(B, T // tm),
        in_specs=[pl.BlockSpec((1, tm, D), lambda b, i: (b, i, 0)),
                  pl.BlockSpec((1, 1, D), lambda b, i: (b, 0, 0)),
                  pl.BlockSpec((1, 1, D), lambda b, i: (b, 0, 0)),
                  pl.BlockSpec((D, N), lambda b, i: (0, 0), pipeline_mode=pl.Buffered(1)),
                  pl.BlockSpec((1, N), lambda b, i: (0, 0))],
        out_specs=[pl.BlockSpec((1, tm, N), lambda b, i: (b, i, 0)),
                   pl.BlockSpec((1, tm, FNET_W), lambda b, i: (b, i, 0))],
        out_shape=[jax.ShapeDtypeStruct((B, T, N), bf16), jax.ShapeDtypeStruct((B, T, FNET_W), f32)],
        compiler_params=_cparams(("parallel", "parallel")),
        name="in_proj",
    )(x, sh, sc, w, b)


CONV_HALO = 16
CONV_SUB = 128


def _conv_kernel(prev_ref, cur_ref, next_ref, w_ref, cb_ref, g_ref, b_ref, o_ref, hs_ref, rot_ref, *, tc, nchunks):
    i = pl.program_id(1)

    def glu(v):
        v = v.astype(f32)
        return v[:, :CONV_CH] * jax.nn.sigmoid(v[:, CONV_CH:])

    hs_ref[0:CONV_HALO, :] = jnp.where(i > 0, glu(prev_ref[0]), 0.0)
    hs_ref[CONV_HALO:CONV_HALO + tc, :] = glu(cur_ref[0])
    hs_ref[CONV_HALO + tc:2 * CONV_HALO + tc, :] = jnp.where(i < nchunks - 1, glu(next_ref[0]), 0.0)
    base = CONV_HALO - CONV_WIDTH // 2
    span = tc + 2 * CONV_HALO - SUBLANES
    for r in range(1, SUBLANES):
        rot_ref[r] = hs_ref[pl.ds(r, span), :]
    for s in range(tc // CONV_SUB):
        acc = jnp.zeros((CONV_SUB, CONV_CH), f32)
        for j in range(CONV_WIDTH):
            off = base + j
            r, q = off % SUBLANES, off - off % SUBLANES
            src = hs_ref[pl.ds(s * CONV_SUB + q, CONV_SUB), :] if r == 0 else rot_ref[r, pl.ds(s * CONV_SUB + q, CONV_SUB), :]
            acc = acc + src * w_ref[j:j + 1, :]
        y = _ln(acc + cb_ref[...]) * g_ref[...] + b_ref[...]
        y = y * jax.nn.sigmoid(y)
        o_ref[0, s * CONV_SUB:(s + 1) * CONV_SUB, :] = y.astype(o_ref.dtype)


def _conv_call(p, conv_w, conv_b, ln_g, ln_b):
    B, T, _ = p.shape
    tc = min(512, T)
    nchunks = T // tc
    hb = tc // CONV_HALO
    nhb = T // CONV_HALO
    width = 2 * CONV_CH
    kern = functools.partial(_conv_kernel, tc=tc, nchunks=nchunks)
    vec = lambda v: v.reshape(1, CONV_CH)
    return pl.pallas_call(
        kern,
        grid=(B, nchunks),
        in_specs=[pl.BlockSpec((1, CONV_HALO, width), lambda b, i: (b, jnp.maximum(i * hb - 1, 0), 0)),
                  pl.BlockSpec((1, tc, width), lambda b, i: (b, i, 0)),
                  pl.BlockSpec((1, CONV_HALO, width), lambda b, i: (b, jnp.minimum((i + 1) * hb, nhb - 1), 0)),
                  pl.BlockSpec((CONV_WIDTH, CONV_CH), lambda b, i: (0, 0)),
                  pl.BlockSpec((1, CONV_CH), lambda b, i: (0, 0)),
                  pl.BlockSpec((1, CONV_CH), lambda b, i: (0, 0)),
                  pl.BlockSpec((1, CONV_CH), lambda b, i: (0, 0))],
        out_specs=pl.BlockSpec((1, tc, CONV_CH), lambda b, i: (b, i, 0)),
        out_shape=jax.ShapeDtypeStruct((B, T, CONV_CH), bf16),
        scratch_shapes=[pltpu.VMEM((tc + 2 * CONV_HALO, CONV_CH), f32),
                        pltpu.VMEM((SUBLANES, tc + 2 * CONV_HALO - SUBLANES, CONV_CH), f32)],
        compiler_params=_cparams(("parallel", "parallel")),
        name="conv_branch",
    )(p, p, p, conv_w, vec(conv_b), vec(ln_g), vec(ln_b))


FNET_GROUP_DIM = 64


def _fft_kernel(f_ref, cs_ref, bd_ref, o_ref, rhs_ref, *, T, scale):
    i = pl.program_id(0)
    b = pl.program_id(1)

    @pl.when(i == 0)
    def _():
        rows = min(512, T)
        for r in range(T // rows):
            z = _dot(f_ref[0, r * rows:(r + 1) * rows, :].astype(bf16), bd_ref[...])
            rhs_ref[b, r * rows:(r + 1) * rows, :] = z[:, :FNET_W].astype(bf16)
            rhs_ref[b, T + r * rows:T + (r + 1) * rows, :] = (-z[:, FNET_W:]).astype(bf16)

    o_ref[0] = (_dot(cs_ref[...], rhs_ref[b]) * scale).astype(o_ref.dtype)


def _dft_time_mats(T):
    t = jnp.arange(T, dtype=jnp.int32)
    ang = ((t[:, None] * t[None, :]) % T).astype(f32) * f32(2.0 * math.pi / T)
    return jnp.concatenate([jnp.cos(ang), jnp.sin(ang)], axis=1).astype(bf16)


def _dft_chan_mats():
    n = FNET_GROUP_DIM
    k = np.arange(n)
    ang = 2.0 * np.pi * ((k[:, None] * k[None, :]) % n) / n
    eye = np.eye(FNET_W // n)
    return jnp.asarray(np.concatenate([np.kron(eye, np.cos(ang)), np.kron(eye, np.sin(ang))], axis=1), f32).astype(bf16)


def _fft_call(pf, cs, bd):
    B, T, _ = pf.shape
    p = pf
    tm = min(512, T)
    kern = functools.partial(_fft_kernel, T=T, scale=1.0 / math.sqrt(T * FNET_GROUP_DIM))
    return pl.pallas_call(
        kern,
        grid=(T // tm, B),
        in_specs=[pl.BlockSpec((1, T, FNET_W), lambda i, b: (b, 0, 0)),
                  pl.BlockSpec((tm, 2 * T), lambda i, b: (i, 0)),
                  pl.BlockSpec((FNET_W, 2 * FNET_W), lambda i, b: (0, 0))],
        out_specs=pl.BlockSpec((1, tm, FNET_W), lambda i, b: (b, i, 0)),
        out_shape=jax.ShapeDtypeStruct((B, T, FNET_W), bf16),
        scratch_shapes=[pltpu.VMEM((B, 2 * T, FNET_W), bf16)],
        compiler_params=_cparams(("arbitrary", "arbitrary")),
        name="fourier_branch",
    )(p, cs, bd)


FFT_N = 64
FFT_CHUNK = 16


def _fft_s1_kernel(x_ref, cs_ref, cd_ref, twr_ref, twi_ref, o_ref, scr_ref):
    n, w = FFT_N, FNET_W
    for j in range(FFT_CHUNK):
        a = _dot(cs_ref[...], x_ref[0, :, j, :].astype(bf16))
        scr_ref[j * n:(j + 1) * n, 0:w] = a[0:n].astype(bf16)
        scr_ref[j * n:(j + 1) * n, w:2 * w] = a[n:2 * n].astype(bf16)
    z = _dot(scr_ref[...], cd_ref[...])
    twr = jnp.concatenate([twr_ref[...]] * (w // LANES), axis=1)
    twi = jnp.concatenate([twi_ref[...]] * (w // LANES), axis=1)
    zr, zi = z[:, :w], z[:, w:]
    br = zr * twr - zi * twi
    bi = zr * twi + zi * twr
    for j in range(FFT_CHUNK):
        o_ref[0, j, :, 0:w] = br[j * n:(j + 1) * n]
        o_ref[0, j, :, w:2 * w] = bi[j * n:(j + 1) * n]


def _fft_s2_kernel(b_ref, cs_ref, o_ref, *, scale):
    n, w = FFT_N, FNET_W
    for j in range(FFT_CHUNK):
        r = _dot(cs_ref[...], b_ref[0, :, j, :].astype(bf16))
        o_ref[0, :, j, :] = (r[0:n, 0:w] + r[n:2 * n, w:2 * w]) * scale


def _fft_ct_mats():
    n, w = FFT_N, FNET_W
    k = np.arange(n)
    ang = 2.0 * np.pi * ((k[:, None] * k[None, :]) % n) / n
    c, s = np.cos(ang), np.sin(ang)
    g = FNET_GROUP_DIM
    kg = np.arange(g)
    ang_g = 2.0 * np.pi * ((kg[:, None] * kg[None, :]) % g) / g
    eye = np.eye(w // g)
    cbd, sbd = np.kron(eye, np.cos(ang_g)), np.kron(eye, np.sin(ang_g))
    cs1 = np.concatenate([c, -s], axis=0)
    cd = np.block([[cbd, -sbd], [sbd, cbd]])
    cs2 = np.concatenate([c, s], axis=0)
    t2 = np.arange(n)[:, None]
    k1 = np.arange(n)[None, :]
    tw = 2.0 * np.pi * ((t2 * k1) % (n * n)) / (n * n)
    twr = np.broadcast_to(np.cos(tw).reshape(n * n, 1), (n * n, LANES))
    twi = np.broadcast_to(-np.sin(tw).reshape(n * n, 1), (n * n, LANES))
    as_bf = lambda m: jnp.asarray(m, f32).astype(bf16)
    return as_bf(cs1), as_bf(cd), as_bf(cs2), jnp.asarray(twr, f32), jnp.asarray(twi, f32)


def _fft_ct_call(pf, mats):
    B, T, w = pf.shape
    n, ch = FFT_N, FFT_CHUNK
    cs1, cd, cs2, twr, twi = mats
    full = lambda a: pl.BlockSpec(a.shape, lambda b, i: (0,) * a.ndim)
    stage1 = pl.pallas_call(
        _fft_s1_kernel,
        grid=(B, n // ch),
        in_specs=[pl.BlockSpec((1, n, ch, w), lambda b, i: (b, 0, i, 0)), full(cs1), full(cd),
                  pl.BlockSpec((ch * n, LANES), lambda b, i: (i, 0)),
                  pl.BlockSpec((ch * n, LANES), lambda b, i: (i, 0))],
        out_specs=pl.BlockSpec((1, ch, n, 2 * w), lambda b, i: (b, i, 0, 0)),
        out_shape=jax.ShapeDtypeStruct((B, n, n, 2 * w), f32),
        scratch_shapes=[pltpu.VMEM((ch * n, 2 * w), bf16)],
        compiler_params=_cparams(("parallel", "parallel")),
        name="fourier_stage1",
    )(pf.reshape(B, n, n, w), cs1, cd, twr, twi)
    kern2 = functools.partial(_fft_s2_kernel, scale=1.0 / math.sqrt(T * FNET_GROUP_DIM))
    out = pl.pallas_call(
        kern2,
        grid=(B, n // ch),
        in_specs=[pl.BlockSpec((1, n, ch, 2 * w), lambda b, i: (b, 0, i, 0)), full(cs2)],
        out_specs=pl.BlockSpec((1, n, ch, w), lambda b, i: (b, 0, i, 0)),
        out_shape=jax.ShapeDtypeStruct((B, n, n, w), f32),
        compiler_params=_cparams(("parallel", "parallel")),
        name="fourier_stage2",
    )(stage1, cs2)
    return out.reshape(B, T, w)


def _lane_is_low(shape):
    return lax.broadcasted_iota(jnp.int32, shape, len(shape) - 1) < HEAD_DIM


def _softmax_pv_steps(make_logits, values, extra=None):
    def fold(x, op, acc):
        for c in range(x.shape[1] // LANES):
            blk = x[:, c * LANES:(c + 1) * LANES]
            acc = blk if acc is None else op(acc, blk)
        return acc

    logits = make_logits()
    yield
    m_el = None
    for s in logits:
        m_el = fold(s, jnp.maximum, m_el)
    m = m_el.max(axis=-1, keepdims=True)
    if extra is not None:
        m = jnp.maximum(m, extra)
    yield
    d_el = None
    out = None
    for s, v in zip(logits, values):
        p = jnp.exp(s - m)
        d_el = fold(p, jnp.add, d_el)
        o = _dot(p.astype(bf16), v)
        out = o if out is None else out + o
        yield
    den = d_el.sum(axis=-1, keepdims=True)
    if extra is not None:
        den = den + jnp.exp(extra - m)
    return out * (1.0 / den)


def _run_together(gens):
    results = [None] * len(gens)
    live = list(enumerate(gens))
    while live:
        for item in list(live):
            i, g = item
            try:
                next(g)
            except StopIteration as stop:
                results[i] = stop.value
                live.remove(item)
    return results


def _softmax_pv(logits, values, extra=None):
    return _run_together([_softmax_pv_steps(lambda: logits, values, extra)])[0]


NAT_QROWS = 4
NAT_GROUPS = 8
NAT_BAND = NAT_QROWS + NAT_KH
NAT_TQ = NAT_QROWS * GRID_W
NAT_TK = NAT_BAND * GRID_W
NAT_ROWS = 64
NAT_EDGE_GROUPS = -(-(NAT_KH // 2) // NAT_QROWS)
NQ_COL, NK_COL, NV_COL = 2, 3, 4


def _nat_band_start(group):
    return jnp.clip(group * NAT_QROWS - NAT_KH // 2, 0, NAT_ROWS - NAT_BAND)


def _nat_kernel(q_ref, k_ref, v_ref, kc_ref, vc_ref, bias_ref, o_ref):
    j = pl.program_id(1)
    ngroups = NAT_ROWS // NAT_QROWS
    low = _lane_is_low((NAT_TQ, LANES))
    for g in range(NAT_GROUPS):
        group = j * NAT_GROUPS + g
        start = pl.multiple_of(_nat_band_start(group) * GRID_W, NAT_QROWS * GRID_W)
        var = jnp.where(group < NAT_EDGE_GROUPS, group + 1,
                        jnp.where(group >= ngroups - NAT_EDGE_GROUPS,
                                  group - (ngroups - NAT_EDGE_GROUPS) + NAT_EDGE_GROUPS + 1, 0))
        rows = slice(g * NAT_TQ, (g + 1) * NAT_TQ)
        heads = []
        for c in range(NAT_HEADS // 2):
            cols = slice(c * LANES, (c + 1) * LANES)
            q2 = q_ref[0, rows, cols] * 0.125
            k2 = k_ref[0, pl.ds(start, NAT_TK), cols]
            v2 = v_ref[0, pl.ds(start, NAT_TK), cols]
            kc2 = kc_ref[0, :, cols]
            vc2 = vc_ref[0, :, cols]

            def logits_of(e, q2=q2, k2=k2, kc2=kc2, c=c):
                qm = jnp.where(low if e == 0 else jnp.logical_not(low), q2, jnp.zeros_like(q2))
                return [_dot_nt(qm, k2) + bias_ref[var, 2 * c + e], _dot_nt(qm, kc2)]

            heads += [_softmax_pv_steps(functools.partial(logits_of, e), [v2, vc2]) for e in range(2)]
        outs = _run_together(heads)
        for c in range(NAT_HEADS // 2):
            cols = slice(c * LANES, (c + 1) * LANES)
            o_ref[0, rows, cols] = jnp.where(low, outs[2 * c], outs[2 * c + 1]).astype(o_ref.dtype)


def _nat_bias_tables(rpb):
    rows = NAT_ROWS
    ngroups = rows // NAT_QROWS
    reps = [NAT_EDGE_GROUPS] + list(range(NAT_EDGE_GROUPS)) + list(range(ngroups - NAT_EDGE_GROUPS, ngroups))
    nv = len(reps)
    a = np.arange(NAT_QROWS)[:, None]
    m = np.arange(NAT_BAND)[None, :]
    sel_r = np.zeros((nv, NAT_QROWS, NAT_BAND, 2 * NAT_KH - 1), np.float32)
    ok_r = np.zeros((nv, NAT_QROWS, NAT_BAND), bool)
    for v, jj in enumerate(reps):
        band0 = int(np.clip(jj * NAT_QROWS - NAT_KH // 2, 0, rows - NAT_BAND))
        qr = jj * NAT_QROWS + a
        kr = band0 + m
        rstart = np.clip(qr - NAT_KH // 2, 0, rows - NAT_KH)
        ok = (kr >= rstart) & (kr < rstart + NAT_KH)
        dr = np.clip(kr - qr + NAT_KH - 1, 0, 2 * NAT_KH - 2)
        ok_r[v] = ok
        sel_r[v] = np.eye(2 * NAT_KH - 1, dtype=np.float32)[dr] * ok[..., None]
    qc = np.arange(GRID_W)[:, None]
    kc = np.arange(GRID_W)[None, :]
    cstart = np.clip(qc - NAT_KW // 2, 0, GRID_W - NAT_KW)
    ok_c = (kc >= cstart) & (kc < cstart + NAT_KW)
    dc = np.clip(kc - qc + NAT_KW - 1, 0, 2 * NAT_KW - 2)
    sel_c = np.eye(2 * NAT_KW - 1, dtype=np.float32)[dc] * ok_c[..., None]
    valid = ok_r[:, :, None, :, None] & ok_c[None, None, :, None, :]
    valid = valid.reshape(nv, 1, NAT_TQ, NAT_TK)
    t = jnp.einsum('vamr,hrc,qkc->vhaqmk', jnp.asarray(sel_r), rpb.astype(f32), jnp.asarray(sel_c),
                   precision=lax.Precision.HIGHEST)
    t = t.reshape(nv, NAT_HEADS, NAT_TQ, NAT_TK)
    return jnp.where(jnp.asarray(valid), t, NEG_INF)


def _nat_call(p, pc, bias):
    B, T, _ = p.shape
    assert T == NAT_ROWS * GRID_W
    L = pc.shape[1]
    W = NAT_HEADS * HEAD_DIM
    tq = NAT_GROUPS * NAT_TQ
    return pl.pallas_call(
        _nat_kernel,
        grid=(B, T // tq),
        in_specs=[pl.BlockSpec((1, tq, W), lambda b, j: (b, j, NQ_COL)),
                  pl.BlockSpec((1, T, W), lambda b, j: (b, 0, NK_COL)),
                  pl.BlockSpec((1, T, W), lambda b, j: (b, 0, NV_COL)),
                  pl.BlockSpec((1, L, W), lambda b, j: (b, 0, NK_COL)),
                  pl.BlockSpec((1, L, W), lambda b, j: (b, 0, NV_COL)),
                  pl.BlockSpec(bias.shape, lambda b, j: (0, 0, 0, 0), pipeline_mode=pl.Buffered(1))],
        out_specs=pl.BlockSpec((1, tq, W), lambda b, j: (b, j, 0)),
        out_shape=jax.ShapeDtypeStruct((B, T, W), bf16),
        compiler_params=_cparams(("parallel", "arbitrary")),
        name="nat_attention",
    )(p, p, p, pc, pc, bias)


GQ_COL = 1536 // 512
GK_COL = 2048 // LANES
GV_COL = 2176 // LANES
GQA_R = GQA_Q_HEADS // GQA_KV_HEADS


def _rope(x, cos, sin):
    lane = lax.broadcasted_iota(jnp.int32, x.shape, 1)
    first = (lane % (HEAD_DIM // 2)) < (HEAD_DIM // 4)
    swapped = jnp.where(first, pltpu.roll(x, LANES - HEAD_DIM // 4, 1), pltpu.roll(x, HEAD_DIM // 4, 1))
    return x * cos + swapped * sin


def _rope_tables(S):
    t = np.arange(S)
    pos = np.stack([t // GRID_W, t % GRID_W], axis=1).astype(np.float64)
    quarter = HEAD_DIM // 4
    freqs = ROPE_BASE ** (-np.arange(quarter, dtype=np.float64) / quarter)
    lane = np.arange(LANES)
    which = (lane % HEAD_DIM) // (HEAD_DIM // 2)
    ang = pos[:, which] * freqs[lane % quarter][None, :]
    sign = np.where((lane % (HEAD_DIM // 2)) < quarter, -1.0, 1.0)[None, :]
    return jnp.asarray(np.cos(ang), f32), jnp.asarray(np.sin(ang) * sign, f32)


GQA_QB = 8


def _gqa_kernel(q_ref, kvp_ref, kvm_ref, kvn_ref, kvx_ref, tp_ref, tm_ref, tn_ref, sink_ref, band_ref, o_ref, *, nsteps):
    n = pl.program_id(1)
    blk = GQA_BLOCK
    rows = GQA_R * blk
    k_of = lambda ref, r0=None: ref[0, :, 0:LANES] if r0 is None else ref[0, r0:r0 + blk, 0:LANES]
    v_of = lambda ref, r0=None: ref[0, :, LANES:2 * LANES] if r0 is None else ref[0, r0:r0 + blk, LANES:2 * LANES]
    rope_k = lambda k, t: _rope(k.astype(f32), t[:, 0:LANES], t[:, LANES:2 * LANES]).astype(bf16)
    tmid = [tm_ref[i * blk:(i + 1) * blk, :] for i in range(GQA_QB)]
    kb = [rope_k(k_of(kvp_ref), tp_ref[...])] + [rope_k(k_of(kvm_ref, i * blk), tmid[i]) for i in range(GQA_QB)] \
        + [rope_k(k_of(kvn_ref), tn_ref[...])]
    vb = [v_of(kvp_ref)] + [v_of(kvm_ref, i * blk) for i in range(GQA_QB)] + [v_of(kvn_ref)]
    kx, vx = kvx_ref[0, :, 0:LANES], kvx_ref[0, :, LANES:2 * LANES]
    low = _lane_is_low((rows, LANES))
    for i in range(GQA_QB):
        cos_q, sin_q = tmid[i][:, 0:LANES], tmid[i][:, LANES:2 * LANES]
        qs = [(_rope(q_ref[0, i * blk:(i + 1) * blk, r * LANES:(r + 1) * LANES].astype(f32), cos_q, sin_q) * 0.125)
              for r in range(GQA_R)]
        q_all = jnp.concatenate(qs, axis=0)
        has_prev = (n > 0) if i == 0 else True
        has_next = (n < nsteps - 1) if i == GQA_QB - 1 else True
        bias_p = band_ref[0] if has_prev is True else jnp.minimum(band_ref[0], jnp.where(has_prev, 0.0, NEG_INF))
        bias_n = band_ref[1] if has_next is True else jnp.minimum(band_ref[1], jnp.where(has_next, 0.0, NEG_INF))
        halves = []
        for g in range(GQA_KV_HEADS):
            qm = jnp.where(low if g == 0 else jnp.logical_not(low), q_all, 0.0).astype(bf16)
            logits = [_dot_nt(qm, kb[i]) + bias_p, _dot_nt(qm, kb[i + 1]), _dot_nt(qm, kb[i + 2]) + bias_n,
                      _dot_nt(qm, kx)]
            vals = [vb[i], vb[i + 1], vb[i + 2], vx]
            halves.append(_softmax_pv(logits, vals, extra=sink_ref[g][:, 0:1]))
        out = jnp.where(low, halves[0], halves[1])
        for r in range(GQA_R):
            o_ref[0, i * blk:(i + 1) * blk, r * LANES:(r + 1) * LANES] = out[r * blk:(r + 1) * blk].astype(o_ref.dtype)


def _sink_cols(sink, blk):
    s = sink.astype(f32).reshape(GQA_KV_HEADS, GQA_R, 1, 1)
    return jnp.broadcast_to(s, (GQA_KV_HEADS, GQA_R, blk, LANES)).reshape(GQA_KV_HEADS, GQA_R * blk, LANES)


def _gqa_band_masks():
    qi = np.arange(GQA_R * GQA_BLOCK)[:, None] % GQA_BLOCK
    kj = np.arange(GQA_BLOCK)[None, :]
    return jnp.asarray(np.stack([np.where(kj >= qi, 0.0, NEG_INF), np.where(kj <= qi, 0.0, NEG_INF)]), f32)


def _gqa_call(p, pc, cos, sin, sink):
    B, T, _ = p.shape
    L = pc.shape[1]
    blk = GQA_BLOCK
    nb = T // blk
    nsteps = nb // GQA_QB
    QW = GQA_Q_HEADS * HEAD_DIM
    kvw = 2 * LANES
    kv_col = (GK_COL * LANES) // kvw
    assert GV_COL == GK_COL + 1 and GK_COL % 2 == 0
    tabs = jnp.concatenate([cos, sin], axis=1)
    kern = functools.partial(_gqa_kernel, nsteps=nsteps)
    prev = lambda n: jnp.maximum(n * GQA_QB - 1, 0)
    nxt = lambda n: jnp.minimum((n + 1) * GQA_QB, nb - 1)
    once = dict(pipeline_mode=pl.Buffered(1))
    return pl.pallas_call(
        kern,
        grid=(B, nsteps),
        in_specs=[pl.BlockSpec((1, GQA_QB * blk, QW), lambda b, n: (b, n, GQ_COL)),
                  pl.BlockSpec((1, blk, kvw), lambda b, n: (b, prev(n), kv_col)),
                  pl.BlockSpec((1, GQA_QB * blk, kvw), lambda b, n: (b, n, kv_col)),
                  pl.BlockSpec((1, blk, kvw), lambda b, n: (b, nxt(n), kv_col)),
                  pl.BlockSpec((1, L, kvw), lambda b, n: (b, 0, kv_col)),
                  pl.BlockSpec((blk, kvw), lambda b, n: (prev(n), 0)),
                  pl.BlockSpec((GQA_QB * blk, kvw), lambda b, n: (n, 0)),
                  pl.BlockSpec((blk, kvw), lambda b, n: (nxt(n), 0)),
                  pl.BlockSpec((GQA_KV_HEADS, GQA_R * blk, LANES), lambda b, n: (0, 0, 0), **once),
                  pl.BlockSpec((2, GQA_R * blk, blk), lambda b, n: (0, 0, 0), **once)],
        out_specs=pl.BlockSpec((1, GQA_QB * blk, QW), lambda b, n: (b, n, 0)),
        out_shape=jax.ShapeDtypeStruct((B, T, QW), bf16),
        compiler_params=_cparams(("parallel", "parallel")),
        name="window_gqa",
    )(p, p, p, p, pc, tabs, tabs, tabs, _sink_cols(sink, blk), _gqa_band_masks())


def _ctx_attn_kernel(nq_ref, nk_ref, nv_ref, gq_ref, gk_ref, gv_ref, sink_ref, ob_ref, od_ref, *, L):
    low = _lane_is_low((L, LANES))
    for c in range(NAT_HEADS // 2):
        cols = slice(c * LANES, (c + 1) * LANES)
        q2 = nq_ref[0, :, cols] * 0.125
        halves = []
        for e in range(2):
            qm = jnp.where(low if e == 0 else jnp.logical_not(low), q2, jnp.zeros_like(q2))
            halves.append(_softmax_pv([_dot_nt(qm, nk_ref[0, :, cols])], [nv_ref[0, :, cols]]))
        ob_ref[0, :, cols] = jnp.where(low, halves[0], halves[1]).astype(ob_ref.dtype)
    q_all = jnp.concatenate([gq_ref[0, :, r * LANES:(r + 1) * LANES] for r in range(GQA_R)], axis=0) * 0.125
    low4 = _lane_is_low((GQA_R * L, LANES))
    halves = []
    for g in range(GQA_KV_HEADS):
        qm = jnp.where(low4 if g == 0 else jnp.logical_not(low4), q_all, jnp.zeros_like(q_all))
        halves.append(_softmax_pv([_dot_nt(qm, gk_ref[0])], [gv_ref[0]], extra=sink_ref[g][:, 0:1]))
    out = jnp.where(low4, halves[0], halves[1])
    for r in range(GQA_R):
        od_ref[0, :, r * LANES:(r + 1) * LANES] = out[r * L:(r + 1) * L].astype(od_ref.dtype)


def _ctx_attn_call(pc, sink):
    B, L, _ = pc.shape
    W = NAT_HEADS * HEAD_DIM
    QW = GQA_Q_HEADS * HEAD_DIM
    kern = functools.partial(_ctx_attn_kernel, L=L)
    return pl.pallas_call(
        kern,
        grid=(B,),
        in_specs=[pl.BlockSpec((1, L, W), lambda b: (b, 0, NQ_COL)),
                  pl.BlockSpec((1, L, W), lambda b: (b, 0, NK_COL)),
                  pl.BlockSpec((1, L, W), lambda b: (b, 0, NV_COL)),
                  pl.BlockSpec((1, L, QW), lambda b: (b, 0, GQ_COL)),
                  pl.BlockSpec((1, L, LANES), lambda b: (b, 0, GK_COL)),
                  pl.BlockSpec((1, L, LANES), lambda b: (b, 0, GV_COL)),
                  pl.BlockSpec((GQA_KV_HEADS, GQA_R * L, LANES), lambda b: (0, 0, 0))],
        out_specs=[pl.BlockSpec((1, L, W), lambda b: (b, 0, 0)),
                   pl.BlockSpec((1, L, QW), lambda b: (b, 0, 0))],
        out_shape=[jax.ShapeDtypeStruct((B, L, W), bf16), jax.ShapeDtypeStruct((B, L, QW), bf16)],
        compiler_params=_cparams(("parallel",)),
        name="context_attention",
    )(pc, pc, pc, pc, pc, pc, _sink_cols(sink, L))


def _merge_kernel(x_ref, sh_ref, sc_ref, g1_ref, ya_ref, yb_ref, yc_ref, yd_ref, wg_ref, bg_ref,
                  wa_ref, wb_ref, wc_ref, wd_ref, wo_ref, lg_ref, lb_ref, o_ref):
    D = D_MODEL
    x = x_ref[0]
    xm = (_ln(x) * (1.0 + sc_ref[0]) + sh_ref[0]).astype(bf16)
    m = None
    for j, (y_ref, w_ref) in enumerate(((ya_ref, wa_ref), (yb_ref, wb_ref), (yc_ref, wc_ref), (yd_ref, wd_ref))):
        gate = jax.nn.sigmoid(_dot(xm, wg_ref[:, j * D:(j + 1) * D]) + bg_ref[:, j * D:(j + 1) * D])
        t = gate * _dot(y_ref[0].astype(bf16), w_ref[...])
        m = t if m is None else m + t
    mix = _dot(m.astype(bf16), wo_ref[...])
    z = DEEPNORM_ALPHA * x + g1_ref[0] * mix
    o_ref[0] = _ln(z) * lg_ref[...] + lb_ref[...]


MERGE_TM = 512


def _merge_call(x, sh, sc, g1, ya, yb, yc, yd, wg, bg, wa, wb, wc, wd, wo, lg, lb, tm):
    B, T, D = x.shape
    tok = lambda w: pl.BlockSpec((1, tm, w), lambda b, i: (b, i, 0))
    mod = pl.BlockSpec((1, 1, D), lambda b, i: (b, 0, 0))
    full = lambda a: pl.BlockSpec(a.shape, lambda b, i: (0,) * a.ndim, pipeline_mode=pl.Buffered(1))
    return pl.pallas_call(
        _merge_kernel,
        grid=(B, T // tm),
        in_specs=[tok(D), mod, mod, mod, tok(ya.shape[-1]), tok(yb.shape[-1]), tok(yc.shape[-1]), tok(yd.shape[-1]),
                  full(wg), full(bg), full(wa), full(wb), full(wc), full(wd), full(wo), full(lg), full(lb)],
        out_specs=tok(D),
        out_shape=jax.ShapeDtypeStruct((B, T, D), f32),
        compiler_params=_cparams(("parallel", "parallel")),
        name="merge_ln",
    )(x, sh, sc, g1, ya, yb, yc, yd, wg, bg, wa, wb, wc, wd, wo, lg, lb)


PEER_TM = 512
PEER_EBLK = 16
PEER_SEL_HEADS = 4
PEER_DENSE_J = 4
BIG_NEG = -3.0e38
SQRT_HALF = 0.7071067811865476


def _top_values(s, count, rank_below=None):
    vals = []
    cur = s
    rank = None if rank_below is None else jnp.full(s.shape, float(rank_below), f32)
    for r in range(count):
        m = jnp.max(cur, axis=0, keepdims=True)
        vals.append(m)
        hit = cur == m
        if rank is not None and r < rank_below:
            rank = jnp.where(hit, float(r), rank)
        if r + 1 < count:
            cur = jnp.where(hit, BIG_NEG, cur)
        yield
    return vals, rank


def _peer_select_chunk(h, s1t, s2t, ln, n1_ref, e1_ref, r2_ref, e2_ref):
    k = PEER_TOPK
    pairs = [(i, j) for i in range(k) for j in range(k) if (i + 1) * (j + 1) <= k]
    pad = (-len(pairs)) % SUBLANES
    top_ranks = [a for a in range(k) if (a + 1) * (PEER_DENSE_J + 1) <= k]
    a, _ = yield from _top_values(s1t, k)
    b, rank2 = yield from _top_values(s2t, k, rank_below=k)
    r2_ref[h, :, ln] = rank2.astype(bf16)
    e2_ref[h, :, ln] = jnp.exp(s2t - b[0]).astype(bf16)
    yield
    cand = jnp.concatenate([a[i] + b[j] for i, j in pairs]
                           + [jnp.full((pad, LANES), BIG_NEG, f32)], axis=0)
    top, _ = yield from _top_values(cand, k)
    tau = top[k - 1]
    z = jnp.sum(jnp.where(cand >= tau, jnp.exp(cand - (a[0] + b[0])), 0.0), axis=0, keepdims=True)
    yield
    n1 = jnp.zeros_like(s1t)
    for j in range(PEER_DENSE_J):
        n1 = n1 + jnp.where(s1t + b[j] >= tau, 1.0, 0.0)
        yield
    for r in top_ranks:
        extra = jnp.zeros_like(tau)
        for j in range(PEER_DENSE_J, k // (r + 1)):
            extra = extra + jnp.where(a[r] + b[j] >= tau, 1.0, 0.0)
        n1 = n1 + jnp.where(s1t == a[r], extra, 0.0)
        yield
    n1_ref[h, :, ln] = n1
    e1_ref[h, :, ln] = jnp.exp(s1t - a[0]) * (0.5 / z)
    yield


def _peer_select_lanes(h, off, nlanes, ut, wq_ref, k1_ref, k2_ref, n1_ref, e1_ref, r2_ref, e2_ref):
    nk = PEER_N_KEYS
    r0 = pl.multiple_of(h * 2 * nk, 2 * nk)
    u_cols = ut[:, pl.ds(off, nlanes)]
    q1 = _dot(wq_ref[pl.ds(r0, nk), :], u_cols).astype(bf16)
    yield
    q2 = _dot(wq_ref[pl.ds(r0 + nk, nk), :], u_cols).astype(bf16)
    yield
    s1 = _dot(k1_ref[h], q1)
    s2 = _dot(k2_ref[h], q2)
    yield
    chunks = [_peer_select_chunk(h, s1[:, t * LANES:(t + 1) * LANES], s2[:, t * LANES:(t + 1) * LANES],
                                 pl.ds(off + t * LANES, LANES), n1_ref, e1_ref, r2_ref, e2_ref)
              for t in range(nlanes // LANES)]
    while chunks:
        for g in list(chunks):
            try:
                next(g)
                yield
            except StopIteration:
                chunks.remove(g)


def _peer_kernel(x_ref, sh_ref, sc_ref, g2_ref, wq_ref, k1_ref, k2_ref, eu_ref, ev_ref, lg_ref, lb_ref,
                 o_ref, ut_ref, n1_ref, e1_ref, r2_ref, e2_ref, acc_ref, *, tm, nsteps):
    step = pl.program_id(2)
    nk = PEER_N_KEYS

    @pl.when(step == 0)
    def _select():
        u = _ln(x_ref[0]) * (1.0 + sc_ref[0]) + sh_ref[0]
        ut_ref[...] = u.T.astype(bf16)
        acc_ref[...] = jnp.zeros_like(acc_ref)

        def heads(i, carry):
            _run_together([_peer_select_lanes(i * PEER_SEL_HEADS + d, 0, tm, ut_ref, wq_ref, k1_ref, k2_ref,
                                              n1_ref, e1_ref, r2_ref, e2_ref) for d in range(PEER_SEL_HEADS)])
            return carry

        lax.fori_loop(0, PEER_HEADS // PEER_SEL_HEADS, heads, 0)

    gs = []
    for b in range(PEER_EBLK):
        i1 = step * PEER_EBLK + b
        ht = _dot(eu_ref[b * nk:(b + 1) * nk, :], ut_ref[...])
        w = jnp.zeros((nk, tm), bf16)
        for h in range(PEER_HEADS):
            n1 = n1_ref[h, pl.ds(i1, 1), :].astype(bf16)
            e1 = e1_ref[h, pl.ds(i1, 1), :].astype(bf16)
            w = w + jnp.where(r2_ref[h] < n1, e2_ref[h] * e1, jnp.zeros((), bf16))
        act = ht * (1.0 + lax.erf(ht * SQRT_HALF))
        gs.append(w * act.astype(bf16))
    acc_ref[...] += _dot(ev_ref[...], jnp.concatenate(gs, axis=0))

    @pl.when(step == nsteps - 1)
    def _finish():
        z = DEEPNORM_ALPHA * x_ref[0] + g2_ref[0] * acc_ref[...].T
        o_ref[0] = _ln(z) * lg_ref[...] + lb_ref[...]


def _peer_call(x, sh, sc, g2, wq_t, k1, k2, eu, ev_t, lg, lb):
    B, T, D = x.shape
    tm = PEER_TM
    nk = PEER_N_KEYS
    ne = eu.shape[0]
    eb = PEER_EBLK * nk
    nsteps = ne // eb
    kern = functools.partial(_peer_kernel, tm=tm, nsteps=nsteps)
    mod = pl.BlockSpec((1, 1, D), lambda b, i, s: (b, 0, 0))
    full = lambda a, **kw: pl.BlockSpec(a.shape, lambda b, i, s: (0,) * a.ndim, **kw)
    sel = lambda dt: pltpu.VMEM((PEER_HEADS, nk, tm), dt)
    return pl.pallas_call(
        kern,
        grid=(B, T // tm, nsteps),
        in_specs=[pl.BlockSpec((1, tm, D), lambda b, i, s: (b, i, 0)), mod, mod, mod,
                  full(wq_t, pipeline_mode=pl.Buffered(1)), full(k1), full(k2),
                  pl.BlockSpec((eb, D), lambda b, i, s: (s, 0)),
                  pl.BlockSpec((D, eb), lambda b, i, s: (0, s)),
                  full(lg), full(lb)],
        out_specs=pl.BlockSpec((1, tm, D), lambda b, i, s: (b, i, 0)),
        out_shape=jax.ShapeDtypeStruct((B, T, D), f32),
        scratch_shapes=[pltpu.VMEM((D, tm), bf16), sel(f32), sel(f32), sel(bf16), sel(bf16),
                        pltpu.VMEM((D, tm), f32)],
        compiler_params=_cparams(("parallel", "parallel", "arbitrary")),
        name="peer_ffn",
    )(x, sh, sc, g2, wq_t, k1, k2, eu, ev_t, lg, lb)


def _pair_kv_groups(a, axis):
    shape = a.shape
    a = a.reshape(shape[:axis] + (GQA_KV_HEADS, GQA_R, HEAD_DIM) + shape[axis + 1:])
    return jnp.swapaxes(a, axis, axis + 1).reshape(shape)


def kernel(x, c, ctx, c_ctx, w_ada, b_ada, w_in, b_in, conv_w, conv_b, conv_ln_g, conv_ln_b, nat_rpb, gqa_sink,
           w_branch_a, w_branch_b, w_branch_c, w_branch_d, w_out, ln1_g, ln1_b, peer_wq, peer_k1, peer_k2,
           peer_u, peer_v, ln2_g, ln2_b):
    B, S, D = x.shape
    L = ctx.shape[1]
    gq0, gq1 = 1536, 2048

    cvec = jnp.concatenate([c, c_ctx[None, :], jnp.zeros((8 - B - 1, D), f32)], axis=0)
    mod = _ada_call(cvec, w_ada, b_ada)
    cos, sin = _rope_tables(S)
    assert S == FFT_N * FFT_N
    ct_mats = _fft_ct_mats()
    cs_ctx = _dft_time_mats(L)
    bd = _dft_chan_mats()
    row = lambda v: v.reshape(1, -1)

    xc = ctx.reshape(1, B * L, D)
    for i in range(DEPTH):
        lat = [mod[i, :B, k * D:(k + 1) * D].reshape(B, 1, D) for k in range(6)]
        con = [mod[i, B:B + 1, k * D:(k + 1) * D].reshape(1, 1, D) for k in range(6)]
        wi, bi = w_in[i], b_in[i]
        w_small = jnp.concatenate([wi[:, :gq0], _pair_kv_groups(wi[:, gq0:gq1], 1), wi[:, gq1:N_SMALL]],
                                  axis=1).astype(bf16)
        b_small = row(jnp.concatenate([bi[:gq0], _pair_kv_groups(bi[gq0:gq1], 0), bi[gq1:N_SMALL]]))
        w_gate = wi[:, N_SMALL:].astype(bf16)
        b_gate = row(bi[N_SMALL:])
        wa, wb, wc = w_branch_a[i].astype(bf16), w_branch_b[i].astype(bf16), w_branch_c[i].astype(bf16)
        wd = _pair_kv_groups(w_branch_d[i], 0).astype(bf16)
        wo = w_out[i].astype(bf16)
        merge_w = (w_gate, b_gate, wa, wb, wc, wd, wo, row(ln1_g[i]), row(ln1_b[i]))
        peer_w = (peer_wq[i].T.astype(bf16), peer_k1[i].astype(bf16), peer_k2[i].astype(bf16),
                  peer_u[i].astype(bf16), peer_v[i].T.astype(bf16), row(ln2_g[i]), row(ln2_b[i]))
        conv_p = (conv_w[i], conv_b[i], conv_ln_g[i], conv_ln_b[i])

        pc, pcf = _inproj_call(xc, con[0], con[1], w_small, b_small, INPROJ_TM)
        pc, pcf = pc.reshape(B, L, N_SMALL), pcf.reshape(B, L, FNET_W)
        p, pf = _inproj_call(x, lat[0], lat[1], w_small, b_small, INPROJ_TM)
        y_a = _conv_call(p, *conv_p)
        y_b = _nat_call(p, pc, _nat_bias_tables(nat_rpb[i]))
        y_c = _fft_ct_call(pf, ct_mats)
        y_d = _gqa_call(p, pc, cos, sin, gqa_sink[i])
        x = _merge_call(x, lat[0], lat[1], lat[2], y_a, y_b, y_c, y_d, *merge_w, MERGE_TM)
        x = _peer_call(x, lat[3], lat[4], lat[5], *peer_w)

        if i < DEPTH - 1:
            yc_a = _conv_call(pc, *conv_p)
            yc_b, yc_d = _ctx_attn_call(pc, gqa_sink[i])
            yc_c = _fft_call(pcf, cs_ctx, bd)
            flat = lambda a: a.reshape(1, B * L, a.shape[-1])
            xc = _merge_call(xc, con[0], con[1], con[2], flat(yc_a), flat(yc_b), flat(yc_c), flat(yc_d),
                             *merge_w, MERGE_TM)
            xc = _peer_call(xc, con[3], con[4], con[5], *peer_w)
    return x
```

```python
import functools
import math

import numpy as np
import jax
import jax.numpy as jnp
from jax import lax
from jax.experimental import pallas as pl
from jax.experimental.pallas import tpu as pltpu

f32 = jnp.float32
bf16 = jnp.bfloat16

D_MODEL = 1024
DEPTH = 2
GRID_W = 64
CONV_CH = 256
CONV_WIDTH = 31
NAT_HEADS = 4
NAT_KH = 8
NAT_KW = 16
GQA_Q_HEADS = 8
GQA_KV_HEADS = 2
GQA_BLOCK = 128
HEAD_DIM = 64
ROPE_BASE = 10000.0
PEER_HEADS = 8
PEER_N_KEYS = 128
PEER_TOPK = 16
LN_EPS = 1e-5
NEG_INF = -1e30
DEEPNORM_ALPHA = (2 * DEPTH) ** 0.25

LANES = 128
SUBLANES = 8
N_SMALL = 2304
VMEM_LIMIT = 56 * 1024 * 1024


def _cparams(sem, vmem=VMEM_LIMIT):
    return pltpu.CompilerParams(dimension_semantics=sem, vmem_limit_bytes=vmem)


def _ln(x):
    mu = jnp.mean(x, axis=-1, keepdims=True)
    xc = x - mu
    var = jnp.mean(xc * xc, axis=-1, keepdims=True)
    return xc * lax.rsqrt(var + LN_EPS)


def _dot(a, b):
    return jnp.dot(a, b, preferred_element_type=f32)


def _dot_nt(a, b):
    return lax.dot_general(a, b, (((1,), (1,)), ((), ())), preferred_element_type=f32)


def _ada_kernel(c_ref, w_ref, b_ref, o_ref):
    c = c_ref[...]
    h = c * jax.nn.sigmoid(c)
    o_ref[0] = jnp.dot(h, w_ref[0], preferred_element_type=f32,
                       precision=lax.Precision.HIGHEST) + b_ref[0]


ADA_TN = 3072


def _ada_call(cvec, w_ada, b_ada):
    L, D, N = w_ada.shape
    tn = ADA_TN
    return pl.pallas_call(
        _ada_kernel,
        grid=(L, N // tn),
        in_specs=[pl.BlockSpec((8, D), lambda l, j: (0, 0)),
                  pl.BlockSpec((1, D, tn), lambda l, j: (l, 0, j)),
                  pl.BlockSpec((1, 1, tn), lambda l, j: (l, 0, j))],
        out_specs=pl.BlockSpec((1, 8, tn), lambda l, j: (l, 0, j)),
        out_shape=jax.ShapeDtypeStruct((L, 8, N), f32),
        compiler_params=_cparams(("parallel", "parallel")),
        name="ada_mod",
    )(cvec, w_ada, b_ada.reshape(L, 1, N))


F_OFF = 1280
FNET_W = 256


def _inproj_kernel(x_ref, sh_ref, sc_ref, w_ref, b_ref, o_ref, of_ref):
    xm = _ln(x_ref[0]) * (1.0 + sc_ref[0]) + sh_ref[0]
    y = _dot(xm.astype(bf16), w_ref[...]) + b_ref[...]
    o_ref[0] = y.astype(o_ref.dtype)
    of_ref[0] = y[:, F_OFF:F_OFF + FNET_W]


INPROJ_TM = 1024


def _inproj_call(x, sh, sc, w, b, tm):
    B, T, D = x.shape
    N = w.shape[1]
    return pl.pallas_call(
        _inproj_kernel,
        grid=(B, T // tm),
        in_specs=[pl.BlockSpec((1, tm, D), lambda b, i: (b, i, 0)),
                  pl.BlockSpec((1, 1, D), lambda b, i: (b, 0, 0)),
                  pl.BlockSpec((1, 1, D), lambda b, i: (b, 0, 0)),
                  pl.BlockSpec((D, N), lambda b, i: (0, 0), pipeline_mode=pl.Buffered(1)),
                  pl.BlockSpec((1, N), lambda b, i: (0, 0))],
        out_specs=[pl.BlockSpec((1, tm, N), lambda b, i: (b, i, 0)),
                   pl.BlockSpec((1, tm, FNET_W), lambda b, i: (b, i, 0))],
        out_shape=[jax.ShapeDtypeStruct((B, T, N), bf16), jax.ShapeDtypeStruct((B, T, FNET_W), f32)],
        compiler_params=_cparams(("parallel", "parallel")),
        name="in_proj",
    )(x, sh, sc, w, b)


CONV_HALO = 16
CONV_SUB = 128


def _conv_kernel(prev_ref, cur_ref, next_ref, w_ref, cb_ref, g_ref, b_ref, o_ref, hs_ref, rot_ref, *, tc, nchunks):
    i = pl.program_id(1)

    def glu(v):
        v = v.astype(f32)
        return v[:, :CONV_CH] * jax.nn.sigmoid(v[:, CONV_CH:])

    hs_ref[0:CONV_HALO, :] = jnp.where(i > 0, glu(prev_ref[0]), 0.0)
    hs_ref[CONV_HALO:CONV_HALO + tc, :] = glu(cur_ref[0])
    hs_ref[CONV_HALO + tc:2 * CONV_HALO + tc, :] = jnp.where(i < nchunks - 1, glu(next_ref[0]), 0.0)
    base = CONV_HALO - CONV_WIDTH // 2
    span = tc + 2 * CONV_HALO - SUBLANES
    for r in range(1, SUBLANES):
        rot_ref[r] = hs_ref[pl.ds(r, span), :]
    for s in range(tc // CONV_SUB):
        acc = jnp.zeros((CONV_SUB, CONV_CH), f32)
        for j in range(CONV_WIDTH):
            off = base + j
            r, q = off % SUBLANES, off - off % SUBLANES
            src = hs_ref[pl.ds(s * CONV_SUB + q, CONV_SUB), :] if r == 0 else rot_ref[r, pl.ds(s * CONV_SUB + q, CONV_SUB), :]
            acc = acc + src * w_ref[j:j + 1, :]
        y = _ln(acc + cb_ref[...]) * g_ref[...] + b_ref[...]
        y = y * jax.nn.sigmoid(y)
        o_ref[0, s * CONV_SUB:(s + 1) * CONV_SUB, :] = y.astype(o_ref.dtype)


def _conv_call(p, conv_w, conv_b, ln_g, ln_b):
    B, T, _ = p.shape
    tc = min(512, T)
    nchunks = T // tc
    hb = tc // CONV_HALO
    nhb = T // CONV_HALO
    width = 2 * CONV_CH
    kern = functools.partial(_conv_kernel, tc=tc, nchunks=nchunks)
    vec = lambda v: v.reshape(1, CONV_CH)
    return pl.pallas_call(
        kern,
        grid=(B, nchunks),
        in_specs=[pl.BlockSpec((1, CONV_HALO, width), lambda b, i: (b, jnp.maximum(i * hb - 1, 0), 0)),
                  pl.BlockSpec((1, tc, width), lambda b, i: (b, i, 0)),
                  pl.BlockSpec((1, CONV_HALO, width), lambda b, i: (b, jnp.minimum((i + 1) * hb, nhb - 1), 0)),
                  pl.BlockSpec((CONV_WIDTH, CONV_CH), lambda b, i: (0, 0)),
                  pl.BlockSpec((1, CONV_CH), lambda b, i: (0, 0)),
                  pl.BlockSpec((1, CONV_CH), lambda b, i: (0, 0)),
                  pl.BlockSpec((1, CONV_CH), lambda b, i: (0, 0))],
        out_specs=pl.BlockSpec((1, tc, CONV_CH), lambda b, i: (b, i, 0)),
        out_shape=jax.ShapeDtypeStruct((B, T, CONV_CH), bf16),
        scratch_shapes=[pltpu.VMEM((tc + 2 * CONV_HALO, CONV_CH), f32),
                        pltpu.VMEM((SUBLANES, tc + 2 * CONV_HALO - SUBLANES, CONV_CH), f32)],
        compiler_params=_cparams(("parallel", "parallel")),
        name="conv_branch",
    )(p, p, p, conv_w, vec(conv_b), vec(ln_g), vec(ln_b))


FNET_GROUP_DIM = 64


def _fft_kernel(f_ref, cs_ref, bd_ref, o_ref, rhs_ref, *, T, scale):
    i = pl.program_id(0)
    b = pl.program_id(1)

    @pl.when(i == 0)
    def _():
        rows = min(512, T)
        for r in range(T // rows):
            z = _dot(f_ref[0, r * rows:(r + 1) * rows, :].astype(bf16), bd_ref[...])
            rhs_ref[b, r * rows:(r + 1) * rows, :] = z[:, :FNET_W].astype(bf16)
            rhs_ref[b, T + r * rows:T + (r + 1) * rows, :] = (-z[:, FNET_W:]).astype(bf16)

    o_ref[0] = (_dot(cs_ref[...], rhs_ref[b]) * scale).astype(o_ref.dtype)


def _dft_time_mats(T):
    t = jnp.arange(T, dtype=jnp.int32)
    ang = ((t[:, None] * t[None, :]) % T).astype(f32) * f32(2.0 * math.pi / T)
    return jnp.concatenate([jnp.cos(ang), jnp.sin(ang)], axis=1).astype(bf16)


def _dft_chan_mats():
    n = FNET_GROUP_DIM
    k = np.arange(n)
    ang = 2.0 * np.pi * ((k[:, None] * k[None, :]) % n) / n
    eye = np.eye(FNET_W // n)
    return jnp.asarray(np.concatenate([np.kron(eye, np.cos(ang)), np.kron(eye, np.sin(ang))], axis=1), f32).astype(bf16)


def _fft_call(pf, cs, bd):
    B, T, _ = pf.shape
    p = pf
    tm = min(512, T)
    kern = functools.partial(_fft_kernel, T=T, scale=1.0 / math.sqrt(T * FNET_GROUP_DIM))
    return pl.pallas_call(
        kern,
        grid=(T // tm, B),
        in_specs=[pl.BlockSpec((1, T, FNET_W), lambda i, b: (b, 0, 0)),
                  pl.BlockSpec((tm, 2 * T), lambda i, b: (i, 0)),
                  pl.BlockSpec((FNET_W, 2 * FNET_W), lambda i, b: (0, 0))],
        out_specs=pl.BlockSpec((1, tm, FNET_W), lambda i, b: (b, i, 0)),
        out_shape=jax.ShapeDtypeStruct((B, T, FNET_W), bf16),
        scratch_shapes=[pltpu.VMEM((B, 2 * T, FNET_W), bf16)],
        compiler_params=_cparams(("arbitrary", "arbitrary")),
        name="fourier_branch",
    )(p, cs, bd)


FFT_N = 64
FFT_CHUNK = 16


def _fft_s1_kernel(x_ref, cs_ref, cd_ref, twr_ref, twi_ref, o_ref, scr_ref):
    n, w = FFT_N, FNET_W
    for j in range(FFT_CHUNK):
        a = _dot(cs_ref[...], x_ref[0, :, j, :].astype(bf16))
        scr_ref[j * n:(j + 1) * n, 0:w] = a[0:n].astype(bf16)
        scr_ref[j * n:(j + 1) * n, w:2 * w] = a[n:2 * n].astype(bf16)
    z = _dot(scr_ref[...], cd_ref[...])
    twr = jnp.concatenate([twr_ref[...]] * (w // LANES), axis=1)
    twi = jnp.concatenate([twi_ref[...]] * (w // LANES), axis=1)
    zr, zi = z[:, :w], z[:, w:]
    br = zr * twr - zi * twi
    bi = zr * twi + zi * twr
    for j in range(FFT_CHUNK):
        o_ref[0, j, :, 0:w] = br[j * n:(j + 1) * n]
        o_ref[0, j, :, w:2 * w] = bi[j * n:(j + 1) * n]


def _fft_s2_kernel(b_ref, cs_ref, o_ref, *, scale):
    n, w = FFT_N, FNET_W
    for j in range(FFT_CHUNK):
        r = _dot(cs_ref[...], b_ref[0, :, j, :].astype(bf16))
        o_ref[0, :, j, :] = (r[0:n, 0:w] + r[n:2 * n, w:2 * w]) * scale


def _fft_ct_mats():
    n, w = FFT_N, FNET_W
    k = np.arange(n)
    ang = 2.0 * np.pi * ((k[:, None] * k[None, :]) % n) / n
    c, s = np.cos(ang), np.sin(ang)
    g = FNET_GROUP_DIM
    kg = np.arange(g)
    ang_g = 2.0 * np.pi * ((kg[:, None] * kg[None, :]) % g) / g
    eye = np.eye(w // g)
    cbd, sbd = np.kron(eye, np.cos(ang_g)), np.kron(eye, np.sin(ang_g))
    cs1 = np.concatenate([c, -s], axis=0)
    cd = np.block([[cbd, -sbd], [sbd, cbd]])
    cs2 = np.concatenate([c, s], axis=0)
    t2 = np.arange(n)[:, None]
    k1 = np.arange(n)[None, :]
    tw = 2.0 * np.pi * ((t2 * k1) % (n * n)) / (n * n)
    twr = np.broadcast_to(np.cos(tw).reshape(n * n, 1), (n * n, LANES))
    twi = np.broadcast_to(-np.sin(tw).reshape(n * n, 1), (n * n, LANES))
    as_bf = lambda m: jnp.asarray(m, f32).astype(bf16)
    return as_bf(cs1), as_bf(cd), as_bf(cs2), jnp.asarray(twr, f32), jnp.asarray(twi, f32)


def _fft_ct_call(pf, mats):
    B, T, w = pf.shape
    n, ch = FFT_N, FFT_CHUNK
    cs1, cd, cs2, twr, twi = mats
    full = lambda a: pl.BlockSpec(a.shape, lambda b, i: (0,) * a.ndim)
    stage1 = pl.pallas_call(
        _fft_s1_kernel,
        grid=(B, n // ch),
        in_specs=[pl.BlockSpec((1, n, ch, w), lambda b, i: (b, 0, i, 0)), full(cs1), full(cd),
                  pl.BlockSpec((ch * n, LANES), lambda b, i: (i, 0)),
                  pl.BlockSpec((ch * n, LANES), lambda b, i: (i, 0))],
        out_specs=pl.BlockSpec((1, ch, n, 2 * w), lambda b, i: (b, i, 0, 0)),
        out_shape=jax.ShapeDtypeStruct((B, n, n, 2 * w), f32),
        scratch_shapes=[pltpu.VMEM((ch * n, 2 * w), bf16)],
        compiler_params=_cparams(("parallel", "parallel")),
        name="fourier_stage1",
    )(pf.reshape(B, n, n, w), cs1, cd, twr, twi)
    kern2 = functools.partial(_fft_s2_kernel, scale=1.0 / math.sqrt(T * FNET_GROUP_DIM))
    out = pl.pallas_call(
        kern2,
        grid=(B, n // ch),
        in_specs=[pl.BlockSpec((1, n, ch, 2 * w), lambda b, i: (b, 0, i, 0)), full(cs2)],
        out_specs=pl.BlockSpec((1, n, ch, w), lambda b, i: (b, 0, i, 0)),
        out_shape=jax.ShapeDtypeStruct((B, n, n, w), f32),
        compiler_params=_cparams(("parallel", "parallel")),
        name="fourier_stage2",
    )(stage1, cs2)
    return out.reshape(B, T, w)


def _lane_is_low(shape):
    return lax.broadcasted_iota(jnp.int32, shape, len(shape) - 1) < HEAD_DIM


def _softmax_pv_steps(make_logits, values, extra=None):
    def fold(x, op, acc):
        for c in range(x.shape[1] // LANES):
            blk = x[:, c * LANES:(c + 1) * LANES]
            acc = blk if acc is None else op(acc, blk)
        return acc

    logits = make_logits()
    yield
    m_el = None
    for s in logits:
        m_el = fold(s, jnp.maximum, m_el)
    m = m_el.max(axis=-1, keepdims=True)
    if extra is not None:
        m = jnp.maximum(m, extra)
    yield
    d_el = None
    out = None
    for s, v in zip(logits, values):
        p = jnp.exp(s - m)
        d_el = fold(p, jnp.add, d_el)
        o = _dot(p.astype(bf16), v)
        out = o if out is None else out + o
        yield
    den = d_el.sum(axis=-1, keepdims=True)
    if extra is not None:
        den = den + jnp.exp(extra - m)
    return out * (1.0 / den)


def _run_together(gens):
    results = [None] * len(gens)
    live = list(enumerate(gens))
    while live:
        for item in list(live):
            i, g = item
            try:
                next(g)
            except StopIteration as stop:
                results[i] = stop.value
                live.remove(item)
    return results


def _softmax_pv(logits, values, extra=None):
    return _run_together([_softmax_pv_steps(lambda: logits, values, extra)])[0]


NAT_QROWS = 4
NAT_GROUPS = 8
NAT_BAND = NAT_QROWS + NAT_KH
NAT_TQ = NAT_QROWS * GRID_W
NAT_TK = NAT_BAND * GRID_W
NAT_ROWS = 64
NAT_EDGE_GROUPS = -(-(NAT_KH // 2) // NAT_QROWS)
NQ_COL, NK_COL, NV_COL = 2, 3, 4


def _nat_band_start(group):
    return jnp.clip(group * NAT_QROWS - NAT_KH // 2, 0, NAT_ROWS - NAT_BAND)


def _nat_kernel(q_ref, k_ref, v_ref, kc_ref, vc_ref, bias_ref, o_ref):
    j = pl.program_id(1)
    ngroups = NAT_ROWS // NAT_QROWS
    low = _lane_is_low((NAT_TQ, LANES))
    for g in range(NAT_GROUPS):
        group = j * NAT_GROUPS + g
        start = pl.multiple_of(_nat_band_start(group) * GRID_W, NAT_QROWS * GRID_W)
        var = jnp.where(group < NAT_EDGE_GROUPS, group + 1,
                        jnp.where(group >= ngroups - NAT_EDGE_GROUPS,
                                  group - (ngroups - NAT_EDGE_GROUPS) + NAT_EDGE_GROUPS + 1, 0))
        rows = slice(g * NAT_TQ, (g + 1) * NAT_TQ)
        heads = []
        for c in range(NAT_HEADS // 2):
            cols = slice(c * LANES, (c + 1) * LANES)
            q2 = q_ref[0, rows, cols] * 0.125
            k2 = k_ref[0, pl.ds(start, NAT_TK), cols]
            v2 = v_ref[0, pl.ds(start, NAT_TK), cols]
            kc2 = kc_ref[0, :, cols]
            vc2 = vc_ref[0, :, cols]

            def logits_of(e, q2=q2, k2=k2, kc2=kc2, c=c):
                qm = jnp.where(low if e == 0 else jnp.logical_not(low), q2, jnp.zeros_like(q2))
                return [_dot_nt(qm, k2) + bias_ref[var, 2 * c + e], _dot_nt(qm, kc2)]

            heads += [_softmax_pv_steps(functools.partial(logits_of, e), [v2, vc2]) for e in range(2)]
        outs = _run_together(heads)
        for c in range(NAT_HEADS // 2):
            cols = slice(c * LANES, (c + 1) * LANES)
            o_ref[0, rows, cols] = jnp.where(low, outs[2 * c], outs[2 * c + 1]).astype(o_ref.dtype)


def _nat_bias_tables(rpb):
    rows = NAT_ROWS
    ngroups = rows // NAT_QROWS
    reps = [NAT_EDGE_GROUPS] + list(range(NAT_EDGE_GROUPS)) + list(range(ngroups - NAT_EDGE_GROUPS, ngroups))
    nv = len(reps)
    a = np.arange(NAT_QROWS)[:, None]
    m = np.arange(NAT_BAND)[None, :]
    sel_r = np.zeros((nv, NAT_QROWS, NAT_BAND, 2 * NAT_KH - 1), np.float32)
    ok_r = np.zeros((nv, NAT_QROWS, NAT_BAND), bool)
    for v, jj in enumerate(reps):
        band0 = int(np.clip(jj * NAT_QROWS - NAT_KH // 2, 0, rows - NAT_BAND))
        qr = jj * NAT_QROWS + a
        kr = band0 + m
        rstart = np.clip(qr - NAT_KH // 2, 0, rows - NAT_KH)
        ok = (kr >= rstart) & (kr < rstart + NAT_KH)
        dr = np.clip(kr - qr + NAT_KH - 1, 0, 2 * NAT_KH - 2)
        ok_r[v] = ok
        sel_r[v] = np.eye(2 * NAT_KH - 1, dtype=np.float32)[dr] * ok[..., None]
    qc = np.arange(GRID_W)[:, None]
    kc = np.arange(GRID_W)[None, :]
    cstart = np.clip(qc - NAT_KW // 2, 0, GRID_W - NAT_KW)
    ok_c = (kc >= cstart) & (kc < cstart + NAT_KW)
    dc = np.clip(kc - qc + NAT_KW - 1, 0, 2 * NAT_KW - 2)
    sel_c = np.eye(2 * NAT_KW - 1, dtype=np.float32)[dc] * ok_c[..., None]
    valid = ok_r[:, :, None, :, None] & ok_c[None, None, :, None, :]
    valid = valid.reshape(nv, 1, NAT_TQ, NAT_TK)
    t = jnp.einsum('vamr,hrc,qkc->vhaqmk', jnp.asarray(sel_r), rpb.astype(f32), jnp.asarray(sel_c),
                   precision=lax.Precision.HIGHEST)
    t = t.reshape(nv, NAT_HEADS, NAT_TQ, NAT_TK)
    return jnp.where(jnp.asarray(valid), t, NEG_INF)


def _nat_call(p, pc, bias):
    B, T, _ = p.shape
    assert T == NAT_ROWS * GRID_W
    L = pc.shape[1]
    W = NAT_HEADS * HEAD_DIM
    tq = NAT_GROUPS * NAT_TQ
    return pl.pallas_call(
        _nat_kernel,
        grid=(B, T // tq),
        in_specs=[pl.BlockSpec((1, tq, W), lambda b, j: (b, j, NQ_COL)),
                  pl.BlockSpec((1, T, W), lambda b, j: (b, 0, NK_COL)),
                  pl.BlockSpec((1, T, W), lambda b, j: (b, 0, NV_COL)),
                  pl.BlockSpec((1, L, W), lambda b, j: (b, 0, NK_COL)),
                  pl.BlockSpec((1, L, W), lambda b, j: (b, 0, NV_COL)),
                  pl.BlockSpec(bias.shape, lambda b, j: (0, 0, 0, 0), pipeline_mode=pl.Buffered(1))],
        out_specs=pl.BlockSpec((1, tq, W), lambda b, j: (b, j, 0)),
        out_shape=jax.ShapeDtypeStruct((B, T, W), bf16),
        compiler_params=_cparams(("parallel", "arbitrary")),
        name="nat_attention",
    )(p, p, p, pc, pc, bias)


GQ_COL = 1536 // 512
GK_COL = 2048 // LANES
GV_COL = 2176 // LANES
GQA_R = GQA_Q_HEADS // GQA_KV_HEADS


def _rope(x, cos, sin):
    lane = lax.broadcasted_iota(jnp.int32, x.shape, 1)
    first = (lane % (HEAD_DIM // 2)) < (HEAD_DIM // 4)
    swapped = jnp.where(first, pltpu.roll(x, LANES - HEAD_DIM // 4, 1), pltpu.roll(x, HEAD_DIM // 4, 1))
    return x * cos + swapped * sin


def _rope_tables(S):
    t = np.arange(S)
    pos = np.stack([t // GRID_W, t % GRID_W], axis=1).astype(np.float64)
    quarter = HEAD_DIM // 4
    freqs = ROPE_BASE ** (-np.arange(quarter, dtype=np.float64) / quarter)
    lane = np.arange(LANES)
    which = (lane % HEAD_DIM) // (HEAD_DIM // 2)
    ang = pos[:, which] * freqs[lane % quarter][None, :]
    sign = np.where((lane % (HEAD_DIM // 2)) < quarter, -1.0, 1.0)[None, :]
    return jnp.asarray(np.cos(ang), f32), jnp.asarray(np.sin(ang) * sign, f32)


GQA_QB = 8


def _gqa_kernel(q_ref, kvp_ref, kvm_ref, kvn_ref, kvx_ref, tp_ref, tm_ref, tn_ref, sink_ref, band_ref, o_ref, *, nsteps):
    n = pl.program_id(1)
    blk = GQA_BLOCK
    rows = GQA_R * blk
    k_of = lambda ref, r0=None: ref[0, :, 0:LANES] if r0 is None else ref[0, r0:r0 + blk, 0:LANES]
    v_of = lambda ref, r0=None: ref[0, :, LANES:2 * LANES] if r0 is None else ref[0, r0:r0 + blk, LANES:2 * LANES]
    rope_k = lambda k, t: _rope(k.astype(f32), t[:, 0:LANES], t[:, LANES:2 * LANES]).astype(bf16)
    tmid = [tm_ref[i * blk:(i + 1) * blk, :] for i in range(GQA_QB)]
    kb = [rope_k(k_of(kvp_ref), tp_ref[...])] + [rope_k(k_of(kvm_ref, i * blk), tmid[i]) for i in range(GQA_QB)] \
        + [rope_k(k_of(kvn_ref), tn_ref[...])]
    vb = [v_of(kvp_ref)] + [v_of(kvm_ref, i * blk) for i in range(GQA_QB)] + [v_of(kvn_ref)]
    kx, vx = kvx_ref[0, :, 0:LANES], kvx_ref[0, :, LANES:2 * LANES]
    low = _lane_is_low((rows, LANES))
    for i in range(GQA_QB):
        cos_q, sin_q = tmid[i][:, 0:LANES], tmid[i][:, LANES:2 * LANES]
        qs = [(_rope(q_ref[0, i * blk:(i + 1) * blk, r * LANES:(r + 1) * LANES].astype(f32), cos_q, sin_q) * 0.125)
              for r in range(GQA_R)]
        q_all = jnp.concatenate(qs, axis=0)
        has_prev = (n > 0) if i == 0 else True
        has_next = (n < nsteps - 1) if i == GQA_QB - 1 else True
        bias_p = band_ref[0] if has_prev is True else jnp.minimum(band_ref[0], jnp.where(has_prev, 0.0, NEG_INF))
        bias_n = band_ref[1] if has_next is True else jnp.minimum(band_ref[1], jnp.where(has_next, 0.0, NEG_INF))
        halves = []
        for g in range(GQA_KV_HEADS):
            qm = jnp.where(low if g == 0 else jnp.logical_not(low), q_all, 0.0).astype(bf16)
            logits = [_dot_nt(qm, kb[i]) + bias_p, _dot_nt(qm, kb[i + 1]), _dot_nt(qm, kb[i + 2]) + bias_n,
                      _dot_nt(qm, kx)]
            vals = [vb[i], vb[i + 1], vb[i + 2], vx]
            halves.append(_softmax_pv(logits, vals, extra=sink_ref[g][:, 0:1]))
        out = jnp.where(low, halves[0], halves[1])
        for r in range(GQA_R):
            o_ref[0, i * blk:(i + 1) * blk, r * LANES:(r + 1) * LANES] = out[r * blk:(r + 1) * blk].astype(o_ref.dtype)


def _sink_cols(sink, blk):
    s = sink.astype(f32).reshape(GQA_KV_HEADS, GQA_R, 1, 1)
    return jnp.broadcast_to(s, (GQA_KV_HEADS, GQA_R, blk, LANES)).reshape(GQA_KV_HEADS, GQA_R * blk, LANES)


def _gqa_band_masks():
    qi = np.arange(GQA_R * GQA_BLOCK)[:, None] % GQA_BLOCK
    kj = np.arange(GQA_BLOCK)[None, :]
    return jnp.asarray(np.stack([np.where(kj >= qi, 0.0, NEG_INF), np.where(kj <= qi, 0.0, NEG_INF)]), f32)


def _gqa_call(p, pc, cos, sin, sink):
    B, T, _ = p.shape
    L = pc.shape[1]
    blk = GQA_BLOCK
    nb = T // blk
    nsteps = nb // GQA_QB
    QW = GQA_Q_HEADS * HEAD_DIM
    kvw = 2 * LANES
    kv_col = (GK_COL * LANES) // kvw
    assert GV_COL == GK_COL + 1 and GK_COL % 2 == 0
    tabs = jnp.concatenate([cos, sin], axis=1)
    kern = functools.partial(_gqa_kernel, nsteps=nsteps)
    prev = lambda n: jnp.maximum(n * GQA_QB - 1, 0)
    nxt = lambda n: jnp.minimum((n + 1) * GQA_QB, nb - 1)
    once = dict(pipeline_mode=pl.Buffered(1))
    return pl.pallas_call(
        kern,
        grid=(B, nsteps),
        in_specs=[pl.BlockSpec((1, GQA_QB * blk, QW), lambda b, n: (b, n, GQ_COL)),
                  pl.BlockSpec((1, blk, kvw), lambda b, n: (b, prev(n), kv_col)),
                  pl.BlockSpec((1, GQA_QB * blk, kvw), lambda b, n: (b, n, kv_col)),
                  pl.BlockSpec((1, blk, kvw), lambda b, n: (b, nxt(n), kv_col)),
                  pl.BlockSpec((1, L, kvw), lambda b, n: (b, 0, kv_col)),
                  pl.BlockSpec((blk, kvw), lambda b, n: (prev(n), 0)),
                  pl.BlockSpec((GQA_QB * blk, kvw), lambda b, n: (n, 0)),
                  pl.BlockSpec((blk, kvw), lambda b, n: (nxt(n), 0)),
                  pl.BlockSpec((GQA_KV_HEADS, GQA_R * blk, LANES), lambda b, n: (0, 0, 0), **once),
                  pl.BlockSpec((2, GQA_R * blk, blk), lambda b, n: (0, 0, 0), **once)],
        out_specs=pl.BlockSpec((1, GQA_QB * blk, QW), lambda b, n: (b, n, 0)),
        out_shape=jax.ShapeDtypeStruct((B, T, QW), bf16),
        compiler_params=_cparams(("parallel", "parallel")),
        name="window_gqa",
    )(p, p, p, p, pc, tabs, tabs, tabs, _sink_cols(sink, blk), _gqa_band_masks())


def _ctx_attn_kernel(nq_ref, nk_ref, nv_ref, gq_ref, gk_ref, gv_ref, sink_ref, ob_ref, od_ref, *, L):
    low = _lane_is_low((L, LANES))
    for c in range(NAT_HEADS // 2):
        cols = slice(c * LANES, (c + 1) * LANES)
        q2 = nq_ref[0, :, cols] * 0.125
        halves = []
        for e in range(2):
            qm = jnp.where(low if e == 0 else jnp.logical_not(low), q2, jnp.zeros_like(q2))
            halves.append(_softmax_pv([_dot_nt(qm, nk_ref[0, :, cols])], [nv_ref[0, :, cols]]))
        ob_ref[0, :, cols] = jnp.where(low, halves[0], halves[1]).astype(ob_ref.dtype)
    q_all = jnp.concatenate([gq_ref[0, :, r * LANES:(r + 1) * LANES] for r in range(GQA_R)], axis=0) * 0.125
    low4 = _lane_is_low((GQA_R * L, LANES))
    halves = []
    for g in range(GQA_KV_HEADS):
        qm = jnp.where(low4 if g == 0 else jnp.logical_not(low4), q_all, jnp.zeros_like(q_all))
        halves.append(_softmax_pv([_dot_nt(qm, gk_ref[0])], [gv_ref[0]], extra=sink_ref[g][:, 0:1]))
    out = jnp.where(low4, halves[0], halves[1])
    for r in range(GQA_R):
        od_ref[0, :, r * LANES:(r + 1) * LANES] = out[r * L:(r + 1) * L].astype(od_ref.dtype)


def _ctx_attn_call(pc, sink):
    B, L, _ = pc.shape
    W = NAT_HEADS * HEAD_DIM
    QW = GQA_Q_HEADS * HEAD_DIM
    kern = functools.partial(_ctx_attn_kernel, L=L)
    return pl.pallas_call(
        kern,
        grid=(B,),
        in_specs=[pl.BlockSpec((1, L, W), lambda b: (b, 0, NQ_COL)),
                  pl.BlockSpec((1, L, W), lambda b: (b, 0, NK_COL)),
                  pl.BlockSpec((1, L, W), lambda b: (b, 0, NV_COL)),
                  pl.BlockSpec((1, L, QW), lambda b: (b, 0, GQ_COL)),
                  pl.BlockSpec((1, L, LANES), lambda b: (b, 0, GK_COL)),
                  pl.BlockSpec((1, L, LANES), lambda b: (b, 0, GV_COL)),
                  pl.BlockSpec((GQA_KV_HEADS, GQA_R * L, LANES), lambda b: (0, 0, 0))],
        out_specs=[pl.BlockSpec((1, L, W), lambda b: (b, 0, 0)),
                   pl.BlockSpec((1, L, QW), lambda b: (b, 0, 0))],
        out_shape=[jax.ShapeDtypeStruct((B, L, W), bf16), jax.ShapeDtypeStruct((B, L, QW), bf16)],
        compiler_params=_cparams(("parallel",)),
        name="context_attention",
    )(pc, pc, pc, pc, pc, pc, _sink_cols(sink, L))


def _merge_kernel(x_ref, sh_ref, sc_ref, g1_ref, ya_ref, yb_ref, yc_ref, yd_ref, wg_ref, bg_ref,
                  wa_ref, wb_ref, wc_ref, wd_ref, wo_ref, lg_ref, lb_ref, o_ref):
    D = D_MODEL
    x = x_ref[0]
    xm = (_ln(x) * (1.0 + sc_ref[0]) + sh_ref[0]).astype(bf16)
    m = None
    for j, (y_ref, w_ref) in enumerate(((ya_ref, wa_ref), (yb_ref, wb_ref), (yc_ref, wc_ref), (yd_ref, wd_ref))):
        gate = jax.nn.sigmoid(_dot(xm, wg_ref[:, j * D:(j + 1) * D]) + bg_ref[:, j * D:(j + 1) * D])
        t = gate * _dot(y_ref[0].astype(bf16), w_ref[...])
        m = t if m is None else m + t
    mix = _dot(m.astype(bf16), wo_ref[...])
    z = DEEPNORM_ALPHA * x + g1_ref[0] * mix
    o_ref[0] = _ln(z) * lg_ref[...] + lb_ref[...]


MERGE_TM = 512


def _merge_call(x, sh, sc, g1, ya, yb, yc, yd, wg, bg, wa, wb, wc, wd, wo, lg, lb, tm):
    B, T, D = x.shape
    tok = lambda w: pl.BlockSpec((1, tm, w), lambda b, i: (b, i, 0))
    mod = pl.BlockSpec((1, 1, D), lambda b, i: (b, 0, 0))
    full = lambda a: pl.BlockSpec(a.shape, lambda b, i: (0,) * a.ndim, pipeline_mode=pl.Buffered(1))
    return pl.pallas_call(
        _merge_kernel,
        grid=(B, T // tm),
        in_specs=[tok(D), mod, mod, mod, tok(ya.shape[-1]), tok(yb.shape[-1]), tok(yc.shape[-1]), tok(yd.shape[-1]),
                  full(wg), full(bg), full(wa), full(wb), full(wc), full(wd), full(wo), full(lg), full(lb)],
        out_specs=tok(D),
        out_shape=jax.ShapeDtypeStruct((B, T, D), f32),
        compiler_params=_cparams(("parallel", "parallel")),
        name="merge_ln",
    )(x, sh, sc, g1, ya, yb, yc, yd, wg, bg, wa, wb, wc, wd, wo, lg, lb)


PEER_TM = 512
PEER_EBLK = 16
PEER_SEL_HEADS = 4
PEER_DENSE_J = 4
BIG_NEG = -3.0e38
SQRT_HALF = 0.7071067811865476


def _top_values(s, count, rank_below=None):
    vals = []
    cur = s
    rank = None if rank_below is None else jnp.full(s.shape, float(rank_below), f32)
    for r in range(count):
        m = jnp.max(cur, axis=0, keepdims=True)
        vals.append(m)
        hit = cur == m
        if rank is not None and r < rank_below:
            rank = jnp.where(hit, float(r), rank)
        if r + 1 < count:
            cur = jnp.where(hit, BIG_NEG, cur)
        yield
    return vals, rank


def _bitonic_merge(vs):
    n = len(vs)
    k = n // 2
    while k >= 1:
        for i in range(n):
            if i % (2 * k) < k:
                hi, lo = jnp.maximum(vs[i], vs[i + k]), jnp.minimum(vs[i], vs[i + k])
                vs[i], vs[i + k] = hi, lo
        yield
        k //= 2
    return vs


def _sort_desc(vs):
    if len(vs) == 1:
        return list(vs)
    half = len(vs) // 2
    top = yield from _sort_desc(vs[:half])
    bot = yield from _sort_desc(vs[half:])
    return (yield from _bitonic_merge(top + bot[::-1]))


def _sorted_top(s, count):
    m = s.shape[0] // SUBLANES
    vs = yield from _sort_desc([s[i * SUBLANES:(i + 1) * SUBLANES] for i in range(m)])
    shift = SUBLANES // 2
    if len(vs) < count:
        other = [pltpu.roll(v, shift, 0) for v in vs]
        vs = yield from _bitonic_merge(vs + other[::-1])
        shift //= 2
    while shift >= 1:
        n = len(vs)
        vs = yield from _bitonic_merge([jnp.maximum(vs[i], pltpu.roll(vs[n - 1 - i], shift, 0)) for i in range(n)])
        shift //= 2
    return vs[:count]


def _peer_select_chunk(h, s1t, s2t, ln, n1_ref, e1_ref, r2_ref, e2_ref):
    k = PEER_TOPK
    pairs = [(i, j) for i in range(k) for j in range(k) if (i + 1) * (j + 1) <= k]
    cand_rows = SUBLANES * pl.next_power_of_2(-(-len(pairs) // SUBLANES))
    top_ranks = [a for a in range(k) if (a + 1) * (PEER_DENSE_J + 1) <= k]
    a = [v[0:1, :] for v in (yield from _sorted_top(s1t, k))]
    b, rank2 = yield from _top_values(s2t, k, rank_below=k)
    r2_ref[h, :, ln] = rank2.astype(bf16)
    e2_ref[h, :, ln] = jnp.exp(s2t - b[0]).astype(bf16)
    yield
    cand = jnp.concatenate([a[i] + b[j] for i, j in pairs]
                           + [jnp.full((cand_rows - len(pairs), LANES), BIG_NEG, f32)], axis=0)
    tau = (yield from _sorted_top(cand, k))[k - 1][0:1, :]
    z = jnp.sum(jnp.where(cand >= tau, jnp.exp(cand - (a[0] + b[0])), 0.0), axis=0, keepdims=True)
    yield
    n1 = jnp.zeros_like(s1t)
    for j in range(PEER_DENSE_J):
        n1 = n1 + jnp.where(s1t + b[j] >= tau, 1.0, 0.0)
        yield
    for r in top_ranks:
        extra = jnp.zeros_like(tau)
        for j in range(PEER_DENSE_J, k // (r + 1)):
            extra = extra + jnp.where(a[r] + b[j] >= tau, 1.0, 0.0)
        n1 = n1 + jnp.where(s1t == a[r], extra, 0.0)
        yield
    n1_ref[h, :, ln] = n1
    e1_ref[h, :, ln] = jnp.exp(s1t - a[0]) * (0.5 / z)
    yield


def _peer_select_lanes(h, off, nlanes, ut, wq_ref, k1_ref, k2_ref, n1_ref, e1_ref, r2_ref, e2_ref):
    nk = PEER_N_KEYS
    r0 = pl.multiple_of(h * 2 * nk, 2 * nk)
    u_cols = ut[:, pl.ds(off, nlanes)]
    q1 = _dot(wq_ref[pl.ds(r0, nk), :], u_cols).astype(bf16)
    yield
    q2 = _dot(wq_ref[pl.ds(r0 + nk, nk), :], u_cols).astype(bf16)
    yield
    s1 = _dot(k1_ref[h], q1)
    s2 = _dot(k2_ref[h], q2)
    yield
    chunks = [_peer_select_chunk(h, s1[:, t * LANES:(t + 1) * LANES], s2[:, t * LANES:(t + 1) * LANES],
                                 pl.ds(off + t * LANES, LANES), n1_ref, e1_ref, r2_ref, e2_ref)
              for t in range(nlanes // LANES)]
    while chunks:
        for g in list(chunks):
            try:
                next(g)
                yield
            except StopIteration:
                chunks.remove(g)


def _peer_kernel(x_ref, sh_ref, sc_ref, g2_ref, wq_ref, k1_ref, k2_ref, eu_ref, ev_ref, lg_ref, lb_ref,
                 o_ref, ut_ref, n1_ref, e1_ref, r2_ref, e2_ref, acc_ref, *, tm, nsteps):
    step = pl.program_id(2)
    nk = PEER_N_KEYS

    @pl.when(step == 0)
    def _select():
        u = _ln(x_ref[0]) * (1.0 + sc_ref[0]) + sh_ref[0]
        ut_ref[...] = u.T.astype(bf16)
        acc_ref[...] = jnp.zeros_like(acc_ref)

        def heads(i, carry):
            _run_together([_peer_select_lanes(i * PEER_SEL_HEADS + d, 0, tm, ut_ref, wq_ref, k1_ref, k2_ref,
                                              n1_ref, e1_ref, r2_ref, e2_ref) for d in range(PEER_SEL_HEADS)])
            return carry

        lax.fori_loop(0, PEER_HEADS // PEER_SEL_HEADS, heads, 0)

    gs = []
    for b in range(PEER_EBLK):
        i1 = step * PEER_EBLK + b
        ht = _dot(eu_ref[b * nk:(b + 1) * nk, :], ut_ref[...])
        w = jnp.zeros((nk, tm), bf16)
        for h in range(PEER_HEADS):
            n1 = n1_ref[h, pl.ds(i1, 1), :].astype(bf16)
            e1 = e1_ref[h, pl.ds(i1, 1), :].astype(bf16)
            w = w + jnp.where(r2_ref[h] < n1, e2_ref[h] * e1, jnp.zeros((), bf16))
        act = ht * (1.0 + lax.erf(ht * SQRT_HALF))
        gs.append(w * act.astype(bf16))
    acc_ref[...] += _dot(ev_ref[...], jnp.concatenate(gs, axis=0))

    @pl.when(step == nsteps - 1)
    def _finish():
        z = DEEPNORM_ALPHA * x_ref[0] + g2_ref[0] * acc_ref[...].T
        o_ref[0] = _ln(z) * lg_ref[...] + lb_ref[...]


def _peer_call(x, sh, sc, g2, wq_t, k1, k2, eu, ev_t, lg, lb):
    B, T, D = x.shape
    tm = PEER_TM
    nk = PEER_N_KEYS
    ne = eu.shape[0]
    eb = PEER_EBLK * nk
    nsteps = ne // eb
    kern = functools.partial(_peer_kernel, tm=tm, nsteps=nsteps)
    mod = pl.BlockSpec((1, 1, D), lambda b, i, s: (b, 0, 0))
    full = lambda a, **kw: pl.BlockSpec(a.shape, lambda b, i, s: (0,) * a.ndim, **kw)
    sel = lambda dt: pltpu.VMEM((PEER_HEADS, nk, tm), dt)
    return pl.pallas_call(
        kern,
        grid=(B, T // tm, nsteps),
        in_specs=[pl.BlockSpec((1, tm, D), lambda b, i, s: (b, i, 0)), mod, mod, mod,
                  full(wq_t, pipeline_mode=pl.Buffered(1)), full(k1), full(k2),
                  pl.BlockSpec((eb, D), lambda b, i, s: (s, 0)),
                  pl.BlockSpec((D, eb), lambda b, i, s: (0, s)),
                  full(lg), full(lb)],
        out_specs=pl.BlockSpec((1, tm, D), lambda b, i, s: (b, i, 0)),
        out_shape=jax.ShapeDtypeStruct((B, T, D), f32),
        scratch_shapes=[pltpu.VMEM((D, tm), bf16), sel(f32), sel(f32), sel(bf16), sel(bf16),
                        pltpu.VMEM((D, tm), f32)],
        compiler_params=_cparams(("parallel", "parallel", "arbitrary")),
        name="peer_ffn",
    )(x, sh, sc, g2, wq_t, k1, k2, eu, ev_t, lg, lb)


def _pair_kv_groups(a, axis):
    shape = a.shape
    a = a.reshape(shape[:axis] + (GQA_KV_HEADS, GQA_R, HEAD_DIM) + shape[axis + 1:])
    return jnp.swapaxes(a, axis, axis + 1).reshape(shape)


def kernel(x, c, ctx, c_ctx, w_ada, b_ada, w_in, b_in, conv_w, conv_b, conv_ln_g, conv_ln_b, nat_rpb, gqa_sink,
           w_branch_a, w_branch_b, w_branch_c, w_branch_d, w_out, ln1_g, ln1_b, peer_wq, peer_k1, peer_k2,
           peer_u, peer_v, ln2_g, ln2_b):
    B, S, D = x.shape
    L = ctx.shape[1]
    gq0, gq1 = 1536, 2048

    cvec = jnp.concatenate([c, c_ctx[None, :], jnp.zeros((8 - B - 1, D), f32)], axis=0)
    mod = _ada_call(cvec, w_ada, b_ada)
    cos, sin = _rope_tables(S)
    assert S == FFT_N * FFT_N
    ct_mats = _fft_ct_mats()
    cs_ctx = _dft_time_mats(L)
    bd = _dft_chan_mats()
    row = lambda v: v.reshape(1, -1)

    xc = ctx.reshape(1, B * L, D)
    for i in range(DEPTH):
        lat = [mod[i, :B, k * D:(k + 1) * D].reshape(B, 1, D) for k in range(6)]
        con = [mod[i, B:B + 1, k * D:(k + 1) * D].reshape(1, 1, D) for k in range(6)]
        wi, bi = w_in[i], b_in[i]
        w_small = jnp.concatenate([wi[:, :gq0], _pair_kv_groups(wi[:, gq0:gq1], 1), wi[:, gq1:N_SMALL]],
                                  axis=1).astype(bf16)
        b_small = row(jnp.concatenate([bi[:gq0], _pair_kv_groups(bi[gq0:gq1], 0), bi[gq1:N_SMALL]]))
        w_gate = wi[:, N_SMALL:].astype(bf16)
        b_gate = row(bi[N_SMALL:])
        wa, wb, wc = w_branch_a[i].astype(bf16), w_branch_b[i].astype(bf16), w_branch_c[i].astype(bf16)
        wd = _pair_kv_groups(w_branch_d[i], 0).astype(bf16)
        wo = w_out[i].astype(bf16)
        merge_w = (w_gate, b_gate, wa, wb, wc, wd, wo, row(ln1_g[i]), row(ln1_b[i]))
        peer_w = (peer_wq[i].T.astype(bf16), peer_k1[i].astype(bf16), peer_k2[i].astype(bf16),
                  peer_u[i].astype(bf16), peer_v[i].T.astype(bf16), row(ln2_g[i]), row(ln2_b[i]))
        conv_p = (conv_w[i], conv_b[i], conv_ln_g[i], conv_ln_b[i])

        pc, pcf = _inproj_call(xc, con[0], con[1], w_small, b_small, INPROJ_TM)
        pc, pcf = pc.reshape(B, L, N_SMALL), pcf.reshape(B, L, FNET_W)
        p, pf = _inproj_call(x, lat[0], lat[1], w_small, b_small, INPROJ_TM)
        y_a = _conv_call(p, *conv_p)
        y_b = _nat_call(p, pc, _nat_bias_tables(nat_rpb[i]))
        y_c = _fft_ct_call(pf, ct_mats)
        y_d = _gqa_call(p, pc, cos, sin, gqa_sink[i])
        x = _merge_call(x, lat[0], lat[1], lat[2], y_a, y_b, y_c, y_d, *merge_w, MERGE_TM)
        x = _peer_call(x, lat[3], lat[4], lat[5], *peer_w)

        if i < DEPTH - 1:
            yc_a = _conv_call(pc, *conv_p)
            yc_b, yc_d = _ctx_attn_call(pc, gqa_sink[i])
            yc_c = _fft_call(pcf, cs_ctx, bd)
            flat = lambda a: a.reshape(1, B * L, a.shape[-1])
            xc = _merge_call(xc, con[0], con[1], con[2], flat(yc_a), flat(yc_b), flat(yc_c), flat(yc_d),
                             *merge_w, MERGE_TM)
            xc = _peer_call(xc, con[3], con[4], con[5], *peer_w)
    return x
```

```python
import functools
import math

import numpy as np
import jax
import jax.numpy as jnp
from jax import lax
from jax.experimental import pallas as pl
from jax.experimental.pallas import tpu as pltpu

f32 = jnp.float32
bf16 = jnp.bfloat16

D_MODEL = 1024
DEPTH = 2
GRID_W = 64
CONV_CH = 256
CONV_WIDTH = 31
NAT_HEADS = 4
NAT_KH = 8
NAT_KW = 16
GQA_Q_HEADS = 8
GQA_KV_HEADS = 2
GQA_BLOCK = 128
HEAD_DIM = 64
ROPE_BASE = 10000.0
PEER_HEADS = 8
PEER_N_KEYS = 128
PEER_TOPK = 16
LN_EPS = 1e-5
NEG_INF = -1e30
DEEPNORM_ALPHA = (2 * DEPTH) ** 0.25

LANES = 128
SUBLANES = 8
N_SMALL = 2304
VMEM_LIMIT = 56 * 1024 * 1024


def _cparams(sem, vmem=VMEM_LIMIT):
    return pltpu.CompilerParams(dimension_semantics=sem, vmem_limit_bytes=vmem)


def _ln(x):
    mu = jnp.mean(x, axis=-1, keepdims=True)
    xc = x - mu
    var = jnp.mean(xc * xc, axis=-1, keepdims=True)
    return xc * lax.rsqrt(var + LN_EPS)


def _dot(a, b):
    return jnp.dot(a, b, preferred_element_type=f32)


def _dot_nt(a, b):
    return lax.dot_general(a, b, (((1,), (1,)), ((), ())), preferred_element_type=f32)


def _ada_kernel(c_ref, w_ref, b_ref, o_ref):
    c = c_ref[...]
    h = c * jax.nn.sigmoid(c)
    o_ref[0] = jnp.dot(h, w_ref[0], preferred_element_type=f32,
                       precision=lax.Precision.HIGHEST) + b_ref[0]


ADA_TN = 3072


def _ada_call(cvec, w_ada, b_ada):
    L, D, N = w_ada.shape
    tn = ADA_TN
    return pl.pallas_call(
        _ada_kernel,
        grid=(L, N // tn),
        in_specs=[pl.BlockSpec((8, D), lambda l, j: (0, 0)),
                  pl.BlockSpec((1, D, tn), lambda l, j: (l, 0, j)),
                  pl.BlockSpec((1, 1, tn), lambda l, j: (l, 0, j))],
        out_specs=pl.BlockSpec((1, 8, tn), lambda l, j: (l, 0, j)),
        out_shape=jax.ShapeDtypeStruct((L, 8, N), f32),
        compiler_params=_cparams(("parallel", "parallel")),
        name="ada_mod",
    )(cvec, w_ada, b_ada.reshape(L, 1, N))


F_OFF = 1280
FNET_W = 256


def _inproj_kernel(x_ref, sh_ref, sc_ref, w_ref, b_ref, o_ref, of_ref):
    xm = _ln(x_ref[0]) * (1.0 + sc_ref[0]) + sh_ref[0]
    y = _dot(xm.astype(bf16), w_ref[...]) + b_ref[...]
    o_ref[0] = y.astype(o_ref.dtype)
    of_ref[0] = y[:, F_OFF:F_OFF + FNET_W]


INPROJ_TM = 1024


def _inproj_call(x, sh, sc, w, b, tm):
    B, T, D = x.shape
    N = w.shape[1]
    return pl.pallas_call(
        _inproj_kernel,
        grid=(B, T // tm),
        in_specs=[pl.BlockSpec((1, tm, D), lambda b, i: (b, i, 0)),
                  pl.BlockSpec((1, 1, D), lambda b, i: (b, 0, 0)),
                  pl.BlockSpec((1, 1, D), lambda b, i: (b, 0, 0)),
                  pl.BlockSpec((D, N), lambda b, i: (0, 0), pipeline_mode=pl.Buffered(1)),
                  pl.BlockSpec((1, N), lambda b, i: (0, 0))],
        out_specs=[pl.BlockSpec((1, tm, N), lambda b, i: (b, i, 0)),
                   pl.BlockSpec((1, tm, FNET_W), lambda b, i: (b, i, 0))],
        out_shape=[jax.ShapeDtypeStruct((B, T, N), bf16), jax.ShapeDtypeStruct((B, T, FNET_W), f32)],
        compiler_params=_cparams(("parallel", "parallel")),
        name="in_proj",
    )(x, sh, sc, w, b)


CONV_HALO = 16
CONV_SUB = 128


def _conv_kernel(prev_ref, cur_ref, next_ref, w_ref, cb_ref, g_ref, b_ref, o_ref, hs_ref, rot_ref, *, tc, nchunks):
    i = pl.program_id(1)

    def glu(v):
        v = v.astype(f32)
        return v[:, :CONV_CH] * jax.nn.sigmoid(v[:, CONV_CH:])

    hs_ref[0:CONV_HALO, :] = jnp.where(i > 0, glu(prev_ref[0]), 0.0)
    hs_ref[CONV_HALO:CONV_HALO + tc, :] = glu(cur_ref[0])
    hs_ref[CONV_HALO + tc:2 * CONV_HALO + tc, :] = jnp.where(i < nchunks - 1, glu(next_ref[0]), 0.0)
    base = CONV_HALO - CONV_WIDTH // 2
    span = tc + 2 * CONV_HALO - SUBLANES
    for r in range(1, SUBLANES):
        rot_ref[r] = hs_ref[pl.ds(r, span), :]
    for s in range(tc // CONV_SUB):
        acc = jnp.zeros((CONV_SUB, CONV_CH), f32)
        for j in range(CONV_WIDTH):
            off = base + j
            r, q = off % SUBLANES, off - off % SUBLANES
            src = hs_ref[pl.ds(s * CONV_SUB + q, CONV_SUB), :] if r == 0 else rot_ref[r, pl.ds(s * CONV_SUB + q, CONV_SUB), :]
            acc = acc + src * w_ref[j:j + 1, :]
        y = _ln(acc + cb_ref[...]) * g_ref[...] + b_ref[...]
        y = y * jax.nn.sigmoid(y)
        o_ref[0, s * CONV_SUB:(s + 1) * CONV_SUB, :] = y.astype(o_ref.dtype)


def _conv_call(p, conv_w, conv_b, ln_g, ln_b):
    B, T, _ = p.shape
    tc = min(512, T)
    nchunks = T // tc
    hb = tc // CONV_HALO
    nhb = T // CONV_HALO
    width = 2 * CONV_CH
    kern = functools.partial(_conv_kernel, tc=tc, nchunks=nchunks)
    vec = lambda v: v.reshape(1, CONV_CH)
    return pl.pallas_call(
        kern,
        grid=(B, nchunks),
        in_specs=[pl.BlockSpec((1, CONV_HALO, width), lambda b, i: (b, jnp.maximum(i * hb - 1, 0), 0)),
                  pl.BlockSpec((1, tc, width), lambda b, i: (b, i, 0)),
                  pl.BlockSpec((1, CONV_HALO, width), lambda b, i: (b, jnp.minimum((i + 1) * hb, nhb - 1), 0)),
                  pl.BlockSpec((CONV_WIDTH, CONV_CH), lambda b, i: (0, 0)),
                  pl.BlockSpec((1, CONV_CH), lambda b, i: (0, 0)),
                  pl.BlockSpec((1, CONV_CH), lambda b, i: (0, 0)),
                  pl.BlockSpec((1, CONV_CH), lambda b, i: (0, 0))],
        out_specs=pl.BlockSpec((1, tc, CONV_CH), lambda b, i: (b, i, 0)),
        out_shape=jax.ShapeDtypeStruct((B, T, CONV_CH), bf16),
        scratch_shapes=[pltpu.VMEM((tc + 2 * CONV_HALO, CONV_CH), f32),
                        pltpu.VMEM((SUBLANES, tc + 2 * CONV_HALO - SUBLANES, CONV_CH), f32)],
        compiler_params=_cparams(("parallel", "parallel")),
        name="conv_branch",
    )(p, p, p, conv_w, vec(conv_b), vec(ln_g), vec(ln_b))


FNET_GROUP_DIM = 64


def _fft_kernel(f_ref, cs_ref, bd_ref, o_ref, rhs_ref, *, T, scale):
    i = pl.program_id(0)
    b = pl.program_id(1)

    @pl.when(i == 0)
    def _():
        rows = min(512, T)
        for r in range(T // rows):
            z = _dot(f_ref[0, r * rows:(r + 1) * rows, :].astype(bf16), bd_ref[...])
            rhs_ref[b, r * rows:(r + 1) * rows, :] = z[:, :FNET_W].astype(bf16)
            rhs_ref[b, T + r * rows:T + (r + 1) * rows, :] = (-z[:, FNET_W:]).astype(bf16)

    o_ref[0] = (_dot(cs_ref[...], rhs_ref[b]) * scale).astype(o_ref.dtype)


def _dft_time_mats(T):
    t = jnp.arange(T, dtype=jnp.int32)
    ang = ((t[:, None] * t[None, :]) % T).astype(f32) * f32(2.0 * math.pi / T)
    return jnp.concatenate([jnp.cos(ang), jnp.sin(ang)], axis=1).astype(bf16)


def _dft_chan_mats():
    n = FNET_GROUP_DIM
    k = np.arange(n)
    ang = 2.0 * np.pi * ((k[:, None] * k[None, :]) % n) / n
    eye = np.eye(FNET_W // n)
    return jnp.asarray(np.concatenate([np.kron(eye, np.cos(ang)), np.kron(eye, np.sin(ang))], axis=1), f32).astype(bf16)


def _fft_call(pf, cs, bd):
    B, T, _ = pf.shape
    p = pf
    tm = min(512, T)
    kern = functools.partial(_fft_kernel, T=T, scale=1.0 / math.sqrt(T * FNET_GROUP_DIM))
    return pl.pallas_call(
        kern,
        grid=(T // tm, B),
        in_specs=[pl.BlockSpec((1, T, FNET_W), lambda i, b: (b, 0, 0)),
                  pl.BlockSpec((tm, 2 * T), lambda i, b: (i, 0)),
                  pl.BlockSpec((FNET_W, 2 * FNET_W), lambda i, b: (0, 0))],
        out_specs=pl.BlockSpec((1, tm, FNET_W), lambda i, b: (b, i, 0)),
        out_shape=jax.ShapeDtypeStruct((B, T, FNET_W), bf16),
        scratch_shapes=[pltpu.VMEM((B, 2 * T, FNET_W), bf16)],
        compiler_params=_cparams(("arbitrary", "arbitrary")),
        name="fourier_branch",
    )(p, cs, bd)


FFT_N = 64
FFT_CHUNK = 16


def _fft_s1_kernel(x_ref, cs_ref, cd_ref, twr_ref, twi_ref, o_ref, scr_ref):
    n, w = FFT_N, FNET_W
    for j in range(FFT_CHUNK):
        a = _dot(cs_ref[...], x_ref[0, :, j, :].astype(bf16))
        scr_ref[j * n:(j + 1) * n, 0:w] = a[0:n].astype(bf16)
        scr_ref[j * n:(j + 1) * n, w:2 * w] = a[n:2 * n].astype(bf16)
    z = _dot(scr_ref[...], cd_ref[...])
    twr = jnp.concatenate([twr_ref[...]] * (w // LANES), axis=1)
    twi = jnp.concatenate([twi_ref[...]] * (w // LANES), axis=1)
    zr, zi = z[:, :w], z[:, w:]
    br = zr * twr - zi * twi
    bi = zr * twi + zi * twr
    for j in range(FFT_CHUNK):
        o_ref[0, j, :, 0:w] = br[j * n:(j + 1) * n]
        o_ref[0, j, :, w:2 * w] = bi[j * n:(j + 1) * n]


def _fft_s2_kernel(b_ref, cs_ref, o_ref, *, scale):
    n, w = FFT_N, FNET_W
    for j in range(FFT_CHUNK):
        r = _dot(cs_ref[...], b_ref[0, :, j, :].astype(bf16))
        o_ref[0, :, j, :] = (r[0:n, 0:w] + r[n:2 * n, w:2 * w]) * scale


def _fft_ct_mats():
    n, w = FFT_N, FNET_W
    k = np.arange(n)
    ang = 2.0 * np.pi * ((k[:, None] * k[None, :]) % n) / n
    c, s = np.cos(ang), np.sin(ang)
    g = FNET_GROUP_DIM
    kg = np.arange(g)
    ang_g = 2.0 * np.pi * ((kg[:, None] * kg[None, :]) % g) / g
    eye = np.eye(w // g)
    cbd, sbd = np.kron(eye, np.cos(ang_g)), np.kron(eye, np.sin(ang_g))
    cs1 = np.concatenate([c, -s], axis=0)
    cd = np.block([[cbd, -sbd], [sbd, cbd]])
    cs2 = np.concatenate([c, s], axis=0)
    t2 = np.arange(n)[:, None]
    k1 = np.arange(n)[None, :]
    tw = 2.0 * np.pi * ((t2 * k1) % (n * n)) / (n * n)
    twr = np.broadcast_to(np.cos(tw).reshape(n * n, 1), (n * n, LANES))
    twi = np.broadcast_to(-np.sin(tw).reshape(n * n, 1), (n * n, LANES))
    as_bf = lambda m: jnp.asarray(m, f32).astype(bf16)
    return as_bf(cs1), as_bf(cd), as_bf(cs2), jnp.asarray(twr, f32), jnp.asarray(twi, f32)


def _fft_ct_call(pf, mats):
    B, T, w = pf.shape
    n, ch = FFT_N, FFT_CHUNK
    cs1, cd, cs2, twr, twi = mats
    full = lambda a: pl.BlockSpec(a.shape, lambda b, i: (0,) * a.ndim)
    stage1 = pl.pallas_call(
        _fft_s1_kernel,
        grid=(B, n // ch),
        in_specs=[pl.BlockSpec((1, n, ch, w), lambda b, i: (b, 0, i, 0)), full(cs1), full(cd),
                  pl.BlockSpec((ch * n, LANES), lambda b, i: (i, 0)),
                  pl.BlockSpec((ch * n, LANES), lambda b, i: (i, 0))],
        out_specs=pl.BlockSpec((1, ch, n, 2 * w), lambda b, i: (b, i, 0, 0)),
        out_shape=jax.ShapeDtypeStruct((B, n, n, 2 * w), f32),
        scratch_shapes=[pltpu.VMEM((ch * n, 2 * w), bf16)],
        compiler_params=_cparams(("parallel", "parallel")),
        name="fourier_stage1",
    )(pf.reshape(B, n, n, w), cs1, cd, twr, twi)
    kern2 = functools.partial(_fft_s2_kernel, scale=1.0 / math.sqrt(T * FNET_GROUP_DIM))
    out = pl.pallas_call(
        kern2,
        grid=(B, n // ch),
        in_specs=[pl.BlockSpec((1, n, ch, 2 * w), lambda b, i: (b, 0, i, 0)), full(cs2)],
        out_specs=pl.BlockSpec((1, n, ch, w), lambda b, i: (b, 0, i, 0)),
        out_shape=jax.ShapeDtypeStruct((B, n, n, w), f32),
        compiler_params=_cparams(("parallel", "parallel")),
        name="fourier_stage2",
    )(stage1, cs2)
    return out.reshape(B, T, w)


def _lane_is_low(shape):
    return lax.broadcasted_iota(jnp.int32, shape, len(shape) - 1) < HEAD_DIM


def _softmax_pv_steps(make_logits, values, extra=None):
    def fold(x, op, acc):
        for c in range(x.shape[1] // LANES):
            blk = x[:, c * LANES:(c + 1) * LANES]
            acc = blk if acc is None else op(acc, blk)
        return acc

    logits = make_logits()
    yield
    m_el = None
    for s in logits:
        m_el = fold(s, jnp.maximum, m_el)
    m = m_el.max(axis=-1, keepdims=True)
    if extra is not None:
        m = jnp.maximum(m, extra)
    yield
    d_el = None
    out = None
    for s, v in zip(logits, values):
        p = jnp.exp(s - m)
        d_el = fold(p, jnp.add, d_el)
        o = _dot(p.astype(bf16), v)
        out = o if out is None else out + o
        yield
    den = d_el.sum(axis=-1, keepdims=True)
    if extra is not None:
        den = den + jnp.exp(extra - m)
    return out * (1.0 / den)


def _run_together(gens):
    results = [None] * len(gens)
    live = list(enumerate(gens))
    while live:
        for item in list(live):
            i, g = item
            try:
                next(g)
            except StopIteration as stop:
                results[i] = stop.value
                live.remove(item)
    return results


def _softmax_pv(logits, values, extra=None):
    return _run_together([_softmax_pv_steps(lambda: logits, values, extra)])[0]


NAT_QROWS = 4
NAT_GROUPS = 8
NAT_BAND = NAT_QROWS + NAT_KH
NAT_TQ = NAT_QROWS * GRID_W
NAT_TK = NAT_BAND * GRID_W
NAT_ROWS = 64
NAT_EDGE_GROUPS = -(-(NAT_KH // 2) // NAT_QROWS)
NQ_COL, NK_COL, NV_COL = 2, 3, 4


def _nat_band_start(group):
    return jnp.clip(group * NAT_QROWS - NAT_KH // 2, 0, NAT_ROWS - NAT_BAND)


def _nat_kernel(q_ref, k_ref, v_ref, kc_ref, vc_ref, bias_ref, o_ref):
    j = pl.program_id(1)
    ngroups = NAT_ROWS // NAT_QROWS
    low = _lane_is_low((NAT_TQ, LANES))
    for g in range(NAT_GROUPS):
        group = j * NAT_GROUPS + g
        start = pl.multiple_of(_nat_band_start(group) * GRID_W, NAT_QROWS * GRID_W)
        var = jnp.where(group < NAT_EDGE_GROUPS, group + 1,
                        jnp.where(group >= ngroups - NAT_EDGE_GROUPS,
                                  group - (ngroups - NAT_EDGE_GROUPS) + NAT_EDGE_GROUPS + 1, 0))
        rows = slice(g * NAT_TQ, (g + 1) * NAT_TQ)
        heads = []
        for c in range(NAT_HEADS // 2):
            cols = slice(c * LANES, (c + 1) * LANES)
            q2 = q_ref[0, rows, cols] * 0.125
            k2 = k_ref[0, pl.ds(start, NAT_TK), cols]
            v2 = v_ref[0, pl.ds(start, NAT_TK), cols]
            kc2 = kc_ref[0, :, cols]
            vc2 = vc_ref[0, :, cols]

            def logits_of(e, q2=q2, k2=k2, kc2=kc2, c=c):
                qm = jnp.where(low if e == 0 else jnp.logical_not(low), q2, jnp.zeros_like(q2))
                return [_dot_nt(qm, k2) + bias_ref[var, 2 * c + e], _dot_nt(qm, kc2)]

            heads += [_softmax_pv_steps(functools.partial(logits_of, e), [v2, vc2]) for e in range(2)]
        outs = _run_together(heads)
        for c in range(NAT_HEADS // 2):
            cols = slice(c * LANES, (c + 1) * LANES)
            o_ref[0, rows, cols] = jnp.where(low, outs[2 * c], outs[2 * c + 1]).astype(o_ref.dtype)


def _nat_bias_tables(rpb):
    rows = NAT_ROWS
    ngroups = rows // NAT_QROWS
    reps = [NAT_EDGE_GROUPS] + list(range(NAT_EDGE_GROUPS)) + list(range(ngroups - NAT_EDGE_GROUPS, ngroups))
    nv = len(reps)
    a = np.arange(NAT_QROWS)[:, None]
    m = np.arange(NAT_BAND)[None, :]
    sel_r = np.zeros((nv, NAT_QROWS, NAT_BAND, 2 * NAT_KH - 1), np.float32)
    ok_r = np.zeros((nv, NAT_QROWS, NAT_BAND), bool)
    for v, jj in enumerate(reps):
        band0 = int(np.clip(jj * NAT_QROWS - NAT_KH // 2, 0, rows - NAT_BAND))
        qr = jj * NAT_QROWS + a
        kr = band0 + m
        rstart = np.clip(qr - NAT_KH // 2, 0, rows - NAT_KH)
        ok = (kr >= rstart) & (kr < rstart + NAT_KH)
        dr = np.clip(kr - qr + NAT_KH - 1, 0, 2 * NAT_KH - 2)
        ok_r[v] = ok
        sel_r[v] = np.eye(2 * NAT_KH - 1, dtype=np.float32)[dr] * ok[..., None]
    qc = np.arange(GRID_W)[:, None]
    kc = np.arange(GRID_W)[None, :]
    cstart = np.clip(qc - NAT_KW // 2, 0, GRID_W - NAT_KW)
    ok_c = (kc >= cstart) & (kc < cstart + NAT_KW)
    dc = np.clip(kc - qc + NAT_KW - 1, 0, 2 * NAT_KW - 2)
    sel_c = np.eye(2 * NAT_KW - 1, dtype=np.float32)[dc] * ok_c[..., None]
    valid = ok_r[:, :, None, :, None] & ok_c[None, None, :, None, :]
    valid = valid.reshape(nv, 1, NAT_TQ, NAT_TK)
    t = jnp.einsum('vamr,hrc,qkc->vhaqmk', jnp.asarray(sel_r), rpb.astype(f32), jnp.asarray(sel_c),
                   precision=lax.Precision.HIGHEST)
    t = t.reshape(nv, NAT_HEADS, NAT_TQ, NAT_TK)
    return jnp.where(jnp.asarray(valid), t, NEG_INF)


def _nat_call(p, pc, bias):
    B, T, _ = p.shape
    assert T == NAT_ROWS * GRID_W
    L = pc.shape[1]
    W = NAT_HEADS * HEAD_DIM
    tq = NAT_GROUPS * NAT_TQ
    return pl.pallas_call(
        _nat_kernel,
        grid=(B, T // tq),
        in_specs=[pl.BlockSpec((1, tq, W), lambda b, j: (b, j, NQ_COL)),
                  pl.BlockSpec((1, T, W), lambda b, j: (b, 0, NK_COL)),
                  pl.BlockSpec((1, T, W), lambda b, j: (b, 0, NV_COL)),
                  pl.BlockSpec((1, L, W), lambda b, j: (b, 0, NK_COL)),
                  pl.BlockSpec((1, L, W), lambda b, j: (b, 0, NV_COL)),
                  pl.BlockSpec(bias.shape, lambda b, j: (0, 0, 0, 0), pipeline_mode=pl.Buffered(1))],
        out_specs=pl.BlockSpec((1, tq, W), lambda b, j: (b, j, 0)),
        out_shape=jax.ShapeDtypeStruct((B, T, W), bf16),
        compiler_params=_cparams(("parallel", "arbitrary")),
        name="nat_attention",
    )(p, p, p, pc, pc, bias)


GQ_COL = 1536 // 512
GK_COL = 2048 // LANES
GV_COL = 2176 // LANES
GQA_R = GQA_Q_HEADS // GQA_KV_HEADS


def _rope(x, cos, sin):
    lane = lax.broadcasted_iota(jnp.int32, x.shape, 1)
    first = (lane % (HEAD_DIM // 2)) < (HEAD_DIM // 4)
    swapped = jnp.where(first, pltpu.roll(x, LANES - HEAD_DIM // 4, 1), pltpu.roll(x, HEAD_DIM // 4, 1))
    return x * cos + swapped * sin


def _rope_tables(S):
    t = np.arange(S)
    pos = np.stack([t // GRID_W, t % GRID_W], axis=1).astype(np.float64)
    quarter = HEAD_DIM // 4
    freqs = ROPE_BASE ** (-np.arange(quarter, dtype=np.float64) / quarter)
    lane = np.arange(LANES)
    which = (lane % HEAD_DIM) // (HEAD_DIM // 2)
    ang = pos[:, which] * freqs[lane % quarter][None, :]
    sign = np.where((lane % (HEAD_DIM // 2)) < quarter, -1.0, 1.0)[None, :]
    return jnp.asarray(np.cos(ang), f32), jnp.asarray(np.sin(ang) * sign, f32)


GQA_QB = 8


def _gqa_kernel(q_ref, kvp_ref, kvm_ref, kvn_ref, kvx_ref, tp_ref, tm_ref, tn_ref, sink_ref, band_ref, o_ref, *, nsteps):
    n = pl.program_id(1)
    blk = GQA_BLOCK
    rows = GQA_R * blk
    k_of = lambda ref, r0=None: ref[0, :, 0:LANES] if r0 is None else ref[0, r0:r0 + blk, 0:LANES]
    v_of = lambda ref, r0=None: ref[0, :, LANES:2 * LANES] if r0 is None else ref[0, r0:r0 + blk, LANES:2 * LANES]
    rope_k = lambda k, t: _rope(k.astype(f32), t[:, 0:LANES], t[:, LANES:2 * LANES]).astype(bf16)
    tmid = [tm_ref[i * blk:(i + 1) * blk, :] for i in range(GQA_QB)]
    kb = [rope_k(k_of(kvp_ref), tp_ref[...])] + [rope_k(k_of(kvm_ref, i * blk), tmid[i]) for i in range(GQA_QB)] \
        + [rope_k(k_of(kvn_ref), tn_ref[...])]
    vb = [v_of(kvp_ref)] + [v_of(kvm_ref, i * blk) for i in range(GQA_QB)] + [v_of(kvn_ref)]
    kx, vx = kvx_ref[0, :, 0:LANES], kvx_ref[0, :, LANES:2 * LANES]
    low = _lane_is_low((rows, LANES))
    for i in range(GQA_QB):
        cos_q, sin_q = tmid[i][:, 0:LANES], tmid[i][:, LANES:2 * LANES]
        qs = [(_rope(q_ref[0, i * blk:(i + 1) * blk, r * LANES:(r + 1) * LANES].astype(f32), cos_q, sin_q) * 0.125)
              for r in range(GQA_R)]
        q_all = jnp.concatenate(qs, axis=0)
        has_prev = (n > 0) if i == 0 else True
        has_next = (n < nsteps - 1) if i == GQA_QB - 1 else True
        bias_p = band_ref[0] if has_prev is True else jnp.minimum(band_ref[0], jnp.where(has_prev, 0.0, NEG_INF))
        bias_n = band_ref[1] if has_next is True else jnp.minimum(band_ref[1], jnp.where(has_next, 0.0, NEG_INF))
        halves = []
        for g in range(GQA_KV_HEADS):
            qm = jnp.where(low if g == 0 else jnp.logical_not(low), q_all, 0.0).astype(bf16)
            logits = [_dot_nt(qm, kb[i]) + bias_p, _dot_nt(qm, kb[i + 1]), _dot_nt(qm, kb[i + 2]) + bias_n,
                      _dot_nt(qm, kx)]
            vals = [vb[i], vb[i + 1], vb[i + 2], vx]
            halves.append(_softmax_pv(logits, vals, extra=sink_ref[g][:, 0:1]))
        out = jnp.where(low, halves[0], halves[1])
        for r in range(GQA_R):
            o_ref[0, i * blk:(i + 1) * blk, r * LANES:(r + 1) * LANES] = out[r * blk:(r + 1) * blk].astype(o_ref.dtype)


def _sink_cols(sink, blk):
    s = sink.astype(f32).reshape(GQA_KV_HEADS, GQA_R, 1, 1)
    return jnp.broadcast_to(s, (GQA_KV_HEADS, GQA_R, blk, LANES)).reshape(GQA_KV_HEADS, GQA_R * blk, LANES)


def _gqa_band_masks():
    qi = np.arange(GQA_R * GQA_BLOCK)[:, None] % GQA_BLOCK
    kj = np.arange(GQA_BLOCK)[None, :]
    return jnp.asarray(np.stack([np.where(kj >= qi, 0.0, NEG_INF), np.where(kj <= qi, 0.0, NEG_INF)]), f32)


def _gqa_call(p, pc, cos, sin, sink):
    B, T, _ = p.shape
    L = pc.shape[1]
    blk = GQA_BLOCK
    nb = T // blk
    nsteps = nb // GQA_QB
    QW = GQA_Q_HEADS * HEAD_DIM
    kvw = 2 * LANES
    kv_col = (GK_COL * LANES) // kvw
    assert GV_COL == GK_COL + 1 and GK_COL % 2 == 0
    tabs = jnp.concatenate([cos, sin], axis=1)
    kern = functools.partial(_gqa_kernel, nsteps=nsteps)
    prev = lambda n: jnp.maximum(n * GQA_QB - 1, 0)
    nxt = lambda n: jnp.minimum((n + 1) * GQA_QB, nb - 1)
    once = dict(pipeline_mode=pl.Buffered(1))
    return pl.pallas_call(
        kern,
        grid=(B, nsteps),
        in_specs=[pl.BlockSpec((1, GQA_QB * blk, QW), lambda b, n: (b, n, GQ_COL)),
                  pl.BlockSpec((1, blk, kvw), lambda b, n: (b, prev(n), kv_col)),
                  pl.BlockSpec((1, GQA_QB * blk, kvw), lambda b, n: (b, n, kv_col)),
                  pl.BlockSpec((1, blk, kvw), lambda b, n: (b, nxt(n), kv_col)),
                  pl.BlockSpec((1, L, kvw), lambda b, n: (b, 0, kv_col)),
                  pl.BlockSpec((blk, kvw), lambda b, n: (prev(n), 0)),
                  pl.BlockSpec((GQA_QB * blk, kvw), lambda b, n: (n, 0)),
                  pl.BlockSpec((blk, kvw), lambda b, n: (nxt(n), 0)),
                  pl.BlockSpec((GQA_KV_HEADS, GQA_R * blk, LANES), lambda b, n: (0, 0, 0), **once),
                  pl.BlockSpec((2, GQA_R * blk, blk), lambda b, n: (0, 0, 0), **once)],
        out_specs=pl.BlockSpec((1, GQA_QB * blk, QW), lambda b, n: (b, n, 0)),
        out_shape=jax.ShapeDtypeStruct((B, T, QW), bf16),
        compiler_params=_cparams(("parallel", "parallel")),
        name="window_gqa",
    )(p, p, p, p, pc, tabs, tabs, tabs, _sink_cols(sink, blk), _gqa_band_masks())


def _ctx_attn_kernel(nq_ref, nk_ref, nv_ref, gq_ref, gk_ref, gv_ref, sink_ref, ob_ref, od_ref, *, L):
    low = _lane_is_low((L, LANES))
    for c in range(NAT_HEADS // 2):
        cols = slice(c * LANES, (c + 1) * LANES)
        q2 = nq_ref[0, :, cols] * 0.125
        halves = []
        for e in range(2):
            qm = jnp.where(low if e == 0 else jnp.logical_not(low), q2, jnp.zeros_like(q2))
            halves.append(_softmax_pv([_dot_nt(qm, nk_ref[0, :, cols])], [nv_ref[0, :, cols]]))
        ob_ref[0, :, cols] = jnp.where(low, halves[0], halves[1]).astype(ob_ref.dtype)
    q_all = jnp.concatenate([gq_ref[0, :, r * LANES:(r + 1) * LANES] for r in range(GQA_R)], axis=0) * 0.125
    low4 = _lane_is_low((GQA_R * L, LANES))
    halves = []
    for g in range(GQA_KV_HEADS):
        qm = jnp.where(low4 if g == 0 else jnp.logical_not(low4), q_all, jnp.zeros_like(q_all))
        halves.append(_softmax_pv([_dot_nt(qm, gk_ref[0])], [gv_ref[0]], extra=sink_ref[g][:, 0:1]))
    out = jnp.where(low4, halves[0], halves[1])
    for r in range(GQA_R):
        od_ref[0, :, r * LANES:(r + 1) * LANES] = out[r * L:(r + 1) * L].astype(od_ref.dtype)


def _ctx_attn_call(pc, sink):
    B, L, _ = pc.shape
    W = NAT_HEADS * HEAD_DIM
    QW = GQA_Q_HEADS * HEAD_DIM
    kern = functools.partial(_ctx_attn_kernel, L=L)
    return pl.pallas_call(
        kern,
        grid=(B,),
        in_specs=[pl.BlockSpec((1, L, W), lambda b: (b, 0, NQ_COL)),
                  pl.BlockSpec((1, L, W), lambda b: (b, 0, NK_COL)),
                  pl.BlockSpec((1, L, W), lambda b: (b, 0, NV_COL)),
                  pl.BlockSpec((1, L, QW), lambda b: (b, 0, GQ_COL)),
                  pl.BlockSpec((1, L, LANES), lambda b: (b, 0, GK_COL)),
                  pl.BlockSpec((1, L, LANES), lambda b: (b, 0, GV_COL)),
                  pl.BlockSpec((GQA_KV_HEADS, GQA_R * L, LANES), lambda b: (0, 0, 0))],
        out_specs=[pl.BlockSpec((1, L, W), lambda b: (b, 0, 0)),
                   pl.BlockSpec((1, L, QW), lambda b: (b, 0, 0))],
        out_shape=[jax.ShapeDtypeStruct((B, L, W), bf16), jax.ShapeDtypeStruct((B, L, QW), bf16)],
        compiler_params=_cparams(("parallel",)),
        name="context_attention",
    )(pc, pc, pc, pc, pc, pc, _sink_cols(sink, L))


def _merge_kernel(x_ref, sh_ref, sc_ref, g1_ref, ya_ref, yb_ref, yc_ref, yd_ref, wg_ref, bg_ref,
                  wa_ref, wb_ref, wc_ref, wd_ref, wo_ref, lg_ref, lb_ref, o_ref):
    D = D_MODEL
    x = x_ref[0]
    xm = (_ln(x) * (1.0 + sc_ref[0]) + sh_ref[0]).astype(bf16)
    m = None
    for j, (y_ref, w_ref) in enumerate(((ya_ref, wa_ref), (yb_ref, wb_ref), (yc_ref, wc_ref), (yd_ref, wd_ref))):
        gate = jax.nn.sigmoid(_dot(xm, wg_ref[:, j * D:(j + 1) * D]) + bg_ref[:, j * D:(j + 1) * D])
        t = gate * _dot(y_ref[0].astype(bf16), w_ref[...])
        m = t if m is None else m + t
    mix = _dot(m.astype(bf16), wo_ref[...])
    z = DEEPNORM_ALPHA * x + g1_ref[0] * mix
    o_ref[0] = _ln(z) * lg_ref[...] + lb_ref[...]


MERGE_TM = 512


def _merge_call(x, sh, sc, g1, ya, yb, yc, yd, wg, bg, wa, wb, wc, wd, wo, lg, lb, tm):
    B, T, D = x.shape
    tok = lambda w: pl.BlockSpec((1, tm, w), lambda b, i: (b, i, 0))
    mod = pl.BlockSpec((1, 1, D), lambda b, i: (b, 0, 0))
    full = lambda a: pl.BlockSpec(a.shape, lambda b, i: (0,) * a.ndim, pipeline_mode=pl.Buffered(1))
    return pl.pallas_call(
        _merge_kernel,
        grid=(B, T // tm),
        in_specs=[tok(D), mod, mod, mod, tok(ya.shape[-1]), tok(yb.shape[-1]), tok(yc.shape[-1]), tok(yd.shape[-1]),
                  full(wg), full(bg), full(wa), full(wb), full(wc), full(wd), full(wo), full(lg), full(lb)],
        out_specs=tok(D),
        out_shape=jax.ShapeDtypeStruct((B, T, D), f32),
        compiler_params=_cparams(("parallel", "parallel")),
        name="merge_ln",
    )(x, sh, sc, g1, ya, yb, yc, yd, wg, bg, wa, wb, wc, wd, wo, lg, lb)


PEER_TM = 512
PEER_EBLK = 16
PEER_SEL_HEADS = 4
PEER_DENSE_J = 4
BIG_NEG = -3.0e38
SQRT_HALF = 0.7071067811865476


def _bitonic_merge(vs):
    n = len(vs)
    k = n // 2
    while k >= 1:
        for i in range(n):
            if i % (2 * k) < k:
                hi, lo = jnp.maximum(vs[i], vs[i + k]), jnp.minimum(vs[i], vs[i + k])
                vs[i], vs[i + k] = hi, lo
        yield
        k //= 2
    return vs


def _sort_desc(vs):
    if len(vs) == 1:
        return list(vs)
    half = len(vs) // 2
    top = yield from _sort_desc(vs[:half])
    bot = yield from _sort_desc(vs[half:])
    return (yield from _bitonic_merge(top + bot[::-1]))


def _sorted_top(s, count):
    m = s.shape[0] // SUBLANES
    vs = yield from _sort_desc([s[i * SUBLANES:(i + 1) * SUBLANES] for i in range(m)])
    shift = SUBLANES // 2
    if len(vs) < count:
        other = [pltpu.roll(v, shift, 0) for v in vs]
        vs = yield from _bitonic_merge(vs + other[::-1])
        shift //= 2
    while shift >= 1:
        n = len(vs)
        vs = yield from _bitonic_merge([jnp.maximum(vs[i], pltpu.roll(vs[n - 1 - i], shift, 0)) for i in range(n)])
        shift //= 2
    return vs[:count]


def _peer_select_chunk(h, s1t, s2t, ln, n1_ref, e1_ref, r2_ref, e2_ref):
    k = PEER_TOPK
    pairs = [(i, j) for i in range(k) for j in range(k) if (i + 1) * (j + 1) <= k]
    cand_rows = SUBLANES * pl.next_power_of_2(-(-len(pairs) // SUBLANES))
    top_ranks = [a for a in range(k) if (a + 1) * (PEER_DENSE_J + 1) <= k]
    a = [v[0:1, :] for v in (yield from _sorted_top(s1t, k))]
    b = [v[0:1, :] for v in (yield from _sorted_top(s2t, k))]
    rank2 = jnp.full(s2t.shape, float(k), f32)
    for r in reversed(range(k)):
        rank2 = jnp.where(s2t == b[r], float(r), rank2)
        if r % 4 == 0:
            yield
    r2_ref[h, :, ln] = rank2.astype(bf16)
    e2_ref[h, :, ln] = jnp.exp(s2t - b[0]).astype(bf16)
    yield
    cand = jnp.concatenate([a[i] + b[j] for i, j in pairs]
                           + [jnp.full((cand_rows - len(pairs), LANES), BIG_NEG, f32)], axis=0)
    tau = (yield from _sorted_top(cand, k))[k - 1][0:1, :]
    z = jnp.sum(jnp.where(cand >= tau, jnp.exp(cand - (a[0] + b[0])), 0.0), axis=0, keepdims=True)
    yield
    n1 = jnp.zeros_like(s1t)
    for j in range(PEER_DENSE_J):
        n1 = n1 + jnp.where(s1t + b[j] >= tau, 1.0, 0.0)
        yield
    for r in top_ranks:
        extra = jnp.zeros_like(tau)
        for j in range(PEER_DENSE_J, k // (r + 1)):
            extra = extra + jnp.where(a[r] + b[j] >= tau, 1.0, 0.0)
        n1 = n1 + jnp.where(s1t == a[r], extra, 0.0)
        yield
    n1_ref[h, :, ln] = n1
    e1_ref[h, :, ln] = jnp.exp(s1t - a[0]) * (0.5 / z)
    yield


def _peer_select_lanes(h, off, nlanes, ut, wq_ref, k1_ref, k2_ref, n1_ref, e1_ref, r2_ref, e2_ref):
    nk = PEER_N_KEYS
    r0 = pl.multiple_of(h * 2 * nk, 2 * nk)
    u_cols = ut[:, pl.ds(off, nlanes)]
    q1 = _dot(wq_ref[pl.ds(r0, nk), :], u_cols).astype(bf16)
    yield
    q2 = _dot(wq_ref[pl.ds(r0 + nk, nk), :], u_cols).astype(bf16)
    yield
    s1 = _dot(k1_ref[h], q1)
    s2 = _dot(k2_ref[h], q2)
    yield
    chunks = [_peer_select_chunk(h, s1[:, t * LANES:(t + 1) * LANES], s2[:, t * LANES:(t + 1) * LANES],
                                 pl.ds(off + t * LANES, LANES), n1_ref, e1_ref, r2_ref, e2_ref)
              for t in range(nlanes // LANES)]
    while chunks:
        for g in list(chunks):
            try:
                next(g)
                yield
            except StopIteration:
                chunks.remove(g)


def _peer_kernel(x_ref, sh_ref, sc_ref, g2_ref, wq_ref, k1_ref, k2_ref, eu_ref, ev_ref, lg_ref, lb_ref,
                 o_ref, ut_ref, n1_ref, e1_ref, r2_ref, e2_ref, acc_ref, *, tm, nsteps):
    step = pl.program_id(2)
    nk = PEER_N_KEYS

    @pl.when(step == 0)
    def _select():
        u = _ln(x_ref[0]) * (1.0 + sc_ref[0]) + sh_ref[0]
        ut_ref[...] = u.T.astype(bf16)
        acc_ref[...] = jnp.zeros_like(acc_ref)

        def heads(i, carry):
            _run_together([_peer_select_lanes(i * PEER_SEL_HEADS + d, 0, tm, ut_ref, wq_ref, k1_ref, k2_ref,
                                              n1_ref, e1_ref, r2_ref, e2_ref) for d in range(PEER_SEL_HEADS)])
            return carry

        lax.fori_loop(0, PEER_HEADS // PEER_SEL_HEADS, heads, 0)

    gs = []
    for b in range(PEER_EBLK):
        i1 = step * PEER_EBLK + b
        ht = _dot(eu_ref[b * nk:(b + 1) * nk, :], ut_ref[...])
        w = jnp.zeros((nk, tm), bf16)
        for h in range(PEER_HEADS):
            n1 = n1_ref[h, pl.ds(i1, 1), :].astype(bf16)
            e1 = e1_ref[h, pl.ds(i1, 1), :].astype(bf16)
            w = w + jnp.where(r2_ref[h] < n1, e2_ref[h] * e1, jnp.zeros((), bf16))
        act = ht * (1.0 + lax.erf(ht * SQRT_HALF))
        gs.append(w * act.astype(bf16))
    acc_ref[...] += _dot(ev_ref[...], jnp.concatenate(gs, axis=0))

    @pl.when(step == nsteps - 1)
    def _finish():
        z = DEEPNORM_ALPHA * x_ref[0] + g2_ref[0] * acc_ref[...].T
        o_ref[0] = _ln(z) * lg_ref[...] + lb_ref[...]


def _peer_call(x, sh, sc, g2, wq_t, k1, k2, eu, ev_t, lg, lb):
    B, T, D = x.shape
    tm = PEER_TM
    nk = PEER_N_KEYS
    ne = eu.shape[0]
    eb = PEER_EBLK * nk
    nsteps = ne // eb
    kern = functools.partial(_peer_kernel, tm=tm, nsteps=nsteps)
    mod = pl.BlockSpec((1, 1, D), lambda b, i, s: (b, 0, 0))
    full = lambda a, **kw: pl.BlockSpec(a.shape, lambda b, i, s: (0,) * a.ndim, **kw)
    sel = lambda dt: pltpu.VMEM((PEER_HEADS, nk, tm), dt)
    return pl.pallas_call(
        kern,
        grid=(B, T // tm, nsteps),
        in_specs=[pl.BlockSpec((1, tm, D), lambda b, i, s: (b, i, 0)), mod, mod, mod,
                  full(wq_t, pipeline_mode=pl.Buffered(1)), full(k1), full(k2),
                  pl.BlockSpec((eb, D), lambda b, i, s: (s, 0)),
                  pl.BlockSpec((D, eb), lambda b, i, s: (0, s)),
                  full(lg), full(lb)],
        out_specs=pl.BlockSpec((1, tm, D), lambda b, i, s: (b, i, 0)),
        out_shape=jax.ShapeDtypeStruct((B, T, D), f32),
        scratch_shapes=[pltpu.VMEM((D, tm), bf16), sel(f32), sel(f32), sel(bf16), sel(bf16),
                        pltpu.VMEM((D, tm), f32)],
        compiler_params=_cparams(("parallel", "parallel", "arbitrary")),
        name="peer_ffn",
    )(x, sh, sc, g2, wq_t, k1, k2, eu, ev_t, lg, lb)


def _pair_kv_groups(a, axis):
    shape = a.shape
    a = a.reshape(shape[:axis] + (GQA_KV_HEADS, GQA_R, HEAD_DIM) + shape[axis + 1:])
    return jnp.swapaxes(a, axis, axis + 1).reshape(shape)


def kernel(x, c, ctx, c_ctx, w_ada, b_ada, w_in, b_in, conv_w, conv_b, conv_ln_g, conv_ln_b, nat_rpb, gqa_sink,
           w_branch_a, w_branch_b, w_branch_c, w_branch_d, w_out, ln1_g, ln1_b, peer_wq, peer_k1, peer_k2,
           peer_u, peer_v, ln2_g, ln2_b):
    B, S, D = x.shape
    L = ctx.shape[1]
    gq0, gq1 = 1536, 2048

    cvec = jnp.concatenate([c, c_ctx[None, :], jnp.zeros((8 - B - 1, D), f32)], axis=0)
    mod = _ada_call(cvec, w_ada, b_ada)
    cos, sin = _rope_tables(S)
    assert S == FFT_N * FFT_N
    ct_mats = _fft_ct_mats()
    cs_ctx = _dft_time_mats(L)
    bd = _dft_chan_mats()
    row = lambda v: v.reshape(1, -1)

    xc = ctx.reshape(1, B * L, D)
    for i in range(DEPTH):
        lat = [mod[i, :B, k * D:(k + 1) * D].reshape(B, 1, D) for k in range(6)]
        con = [mod[i, B:B + 1, k * D:(k + 1) * D].reshape(1, 1, D) for k in range(6)]
        wi, bi = w_in[i], b_in[i]
        w_small = jnp.concatenate([wi[:, :gq0], _pair_kv_groups(wi[:, gq0:gq1], 1), wi[:, gq1:N_SMALL]],
                                  axis=1).astype(bf16)
        b_small = row(jnp.concatenate([bi[:gq0], _pair_kv_groups(bi[gq0:gq1], 0), bi[gq1:N_SMALL]]))
        w_gate = wi[:, N_SMALL:].astype(bf16)
        b_gate = row(bi[N_SMALL:])
        wa, wb, wc = w_branch_a[i].astype(bf16), w_branch_b[i].astype(bf16), w_branch_c[i].astype(bf16)
        wd = _pair_kv_groups(w_branch_d[i], 0).astype(bf16)
        wo = w_out[i].astype(bf16)
        merge_w = (w_gate, b_gate, wa, wb, wc, wd, wo, row(ln1_g[i]), row(ln1_b[i]))
        peer_w = (peer_wq[i].T.astype(bf16), peer_k1[i].astype(bf16), peer_k2[i].astype(bf16),
                  peer_u[i].astype(bf16), peer_v[i].T.astype(bf16), row(ln2_g[i]), row(ln2_b[i]))
        conv_p = (conv_w[i], conv_b[i], conv_ln_g[i], conv_ln_b[i])

        pc, pcf = _inproj_call(xc, con[0], con[1], w_small, b_small, INPROJ_TM)
        pc, pcf = pc.reshape(B, L, N_SMALL), pcf.reshape(B, L, FNET_W)
        p, pf = _inproj_call(x, lat[0], lat[1], w_small, b_small, INPROJ_TM)
        y_a = _conv_call(p, *conv_p)
        y_b = _nat_call(p, pc, _nat_bias_tables(nat_rpb[i]))
        y_c = _fft_ct_call(pf, ct_mats)
        y_d = _gqa_call(p, pc, cos, sin, gqa_sink[i])
        x = _merge_call(x, lat[0], lat[1], lat[2], y_a, y_b, y_c, y_d, *merge_w, MERGE_TM)
        x = _peer_call(x, lat[3], lat[4], lat[5], *peer_w)

        if i < DEPTH - 1:
            yc_a = _conv_call(pc, *conv_p)
            yc_b, yc_d = _ctx_attn_call(pc, gqa_sink[i])
            yc_c = _fft_call(pcf, cs_ctx, bd)
            flat = lambda a: a.reshape(1, B * L, a.shape[-1])
            xc = _merge_call(xc, con[0], con[1], con[2], flat(yc_a), flat(yc_b), flat(yc_c), flat(yc_d),
                             *merge_w, MERGE_TM)
            xc = _peer_call(xc, con[3], con[4], con[5], *peer_w)
    return x
```

```python
import functools
import math

import numpy as np
import jax
import jax.numpy as jnp
from jax import lax
from jax.experimental import pallas as pl
from jax.experimental.pallas import tpu as pltpu

f32 = jnp.float32
bf16 = jnp.bfloat16

D_MODEL = 1024
DEPTH = 2
GRID_W = 64
CONV_CH = 256
CONV_WIDTH = 31
NAT_HEADS = 4
NAT_KH = 8
NAT_KW = 16
GQA_Q_HEADS = 8
GQA_KV_HEADS = 2
GQA_BLOCK = 128
HEAD_DIM = 64
ROPE_BASE = 10000.0
PEER_HEADS = 8
PEER_N_KEYS = 128
PEER_TOPK = 16
LN_EPS = 1e-5
NEG_INF = -1e30
DEEPNORM_ALPHA = (2 * DEPTH) ** 0.25

LANES = 128
SUBLANES = 8
N_SMALL = 2304
VMEM_LIMIT = 56 * 1024 * 1024


def _cparams(sem, vmem=VMEM_LIMIT):
    return pltpu.CompilerParams(dimension_semantics=sem, vmem_limit_bytes=vmem)


def _ln(x):
    mu = jnp.mean(x, axis=-1, keepdims=True)
    xc = x - mu
    var = jnp.mean(xc * xc, axis=-1, keepdims=True)
    return xc * lax.rsqrt(var + LN_EPS)


def _dot(a, b):
    return jnp.dot(a, b, preferred_element_type=f32)


def _dot_nt(a, b):
    return lax.dot_general(a, b, (((1,), (1,)), ((), ())), preferred_element_type=f32)


def _ada_kernel(c_ref, w_ref, b_ref, o_ref):
    c = c_ref[...]
    h = c * jax.nn.sigmoid(c)
    o_ref[0] = jnp.dot(h, w_ref[0], preferred_element_type=f32,
                       precision=lax.Precision.HIGHEST) + b_ref[0]


ADA_TN = 3072


def _ada_call(cvec, w_ada, b_ada):
    L, D, N = w_ada.shape
    tn = ADA_TN
    return pl.pallas_call(
        _ada_kernel,
        grid=(L, N // tn),
        in_specs=[pl.BlockSpec((8, D), lambda l, j: (0, 0)),
                  pl.BlockSpec((1, D, tn), lambda l, j: (l, 0, j)),
                  pl.BlockSpec((1, 1, tn), lambda l, j: (l, 0, j))],
        out_specs=pl.BlockSpec((1, 8, tn), lambda l, j: (l, 0, j)),
        out_shape=jax.ShapeDtypeStruct((L, 8, N), f32),
        compiler_params=_cparams(("parallel", "parallel")),
        name="ada_mod",
    )(cvec, w_ada, b_ada.reshape(L, 1, N))


F_OFF = 1280
FNET_W = 256


def _inproj_kernel(x_ref, sh_ref, sc_ref, w_ref, b_ref, o_ref, of_ref):
    xm = _ln(x_ref[0]) * (1.0 + sc_ref[0]) + sh_ref[0]
    y = _dot(xm.astype(bf16), w_ref[...]) + b_ref[...]
    o_ref[0] = y.astype(o_ref.dtype)
    of_ref[0] = y[:, F_OFF:F_OFF + FNET_W]


INPROJ_TM = 1024


def _inproj_call(x, sh, sc, w, b, tm):
    B, T, D = x.shape
    N = w.shape[1]
    return pl.pallas_call(
        _inproj_kernel,
        grid=(B, T // tm),
        in_specs=[pl.BlockSpec((1, tm, D), lambda b, i: (b, i, 0)),
                  pl.BlockSpec((1, 1, D), lambda b, i: (b, 0, 0)),
                  pl.BlockSpec((1, 1, D), lambda b, i: (b, 0, 0)),
                  pl.BlockSpec((D, N), lambda b, i: (0, 0), pipeline_mode=pl.Buffered(1)),
                  pl.BlockSpec((1, N), lambda b, i: (0, 0))],
        out_specs=[pl.BlockSpec((1, tm, N), lambda b, i: (b, i, 0)),
                   pl.BlockSpec((1, tm, FNET_W), lambda b, i: (b, i, 0))],
        out_shape=[jax.ShapeDtypeStruct((B, T, N), bf16), jax.ShapeDtypeStruct((B, T, FNET_W), f32)],
        compiler_params=_cparams(("parallel", "parallel")),
        name="in_proj",
    )(x, sh, sc, w, b)


CONV_HALO = 16
CONV_SUB = 128


def _conv_kernel(prev_ref, cur_ref, next_ref, w_ref, cb_ref, g_ref, b_ref, o_ref, hs_ref, rot_ref, *, tc, nchunks):
    i = pl.program_id(1)

    def glu(v):
        v = v.astype(f32)
        return v[:, :CONV_CH] * jax.nn.sigmoid(v[:, CONV_CH:])

    hs_ref[0:CONV_HALO, :] = jnp.where(i > 0, glu(prev_ref[0]), 0.0)
    hs_ref[CONV_HALO:CONV_HALO + tc, :] = glu(cur_ref[0])
    hs_ref[CONV_HALO + tc:2 * CONV_HALO + tc, :] = jnp.where(i < nchunks - 1, glu(next_ref[0]), 0.0)
    base = CONV_HALO - CONV_WIDTH // 2
    span = tc + 2 * CONV_HALO - SUBLANES
    for r in range(1, SUBLANES):
        rot_ref[r] = hs_ref[pl.ds(r, span), :]
    for s in range(tc // CONV_SUB):
        acc = jnp.zeros((CONV_SUB, CONV_CH), f32)
        for j in range(CONV_WIDTH):
            off = base + j
            r, q = off % SUBLANES, off - off % SUBLANES
            src = hs_ref[pl.ds(s * CONV_SUB + q, CONV_SUB), :] if r == 0 else rot_ref[r, pl.ds(s * CONV_SUB + q, CONV_SUB), :]
            acc = acc + src * w_ref[j:j + 1, :]
        y = _ln(acc + cb_ref[...]) * g_ref[...] + b_ref[...]
        y = y * jax.nn.sigmoid(y)
        o_ref[0, s * CONV_SUB:(s + 1) * CONV_SUB, :] = y.astype(o_ref.dtype)


def _conv_call(p, conv_w, conv_b, ln_g, ln_b):
    B, T, _ = p.shape
    tc = min(512, T)
    nchunks = T // tc
    hb = tc // CONV_HALO
    nhb = T // CONV_HALO
    width = 2 * CONV_CH
    kern = functools.partial(_conv_kernel, tc=tc, nchunks=nchunks)
    vec = lambda v: v.reshape(1, CONV_CH)
    return pl.pallas_call(
        kern,
        grid=(B, nchunks),
        in_specs=[pl.BlockSpec((1, CONV_HALO, width), lambda b, i: (b, jnp.maximum(i * hb - 1, 0), 0)),
                  pl.BlockSpec((1, tc, width), lambda b, i: (b, i, 0)),
                  pl.BlockSpec((1, CONV_HALO, width), lambda b, i: (b, jnp.minimum((i + 1) * hb, nhb - 1), 0)),
                  pl.BlockSpec((CONV_WIDTH, CONV_CH), lambda b, i: (0, 0)),
                  pl.BlockSpec((1, CONV_CH), lambda b, i: (0, 0)),
                  pl.BlockSpec((1, CONV_CH), lambda b, i: (0, 0)),
                  pl.BlockSpec((1, CONV_CH), lambda b, i: (0, 0))],
        out_specs=pl.BlockSpec((1, tc, CONV_CH), lambda b, i: (b, i, 0)),
        out_shape=jax.ShapeDtypeStruct((B, T, CONV_CH), bf16),
        scratch_shapes=[pltpu.VMEM((tc + 2 * CONV_HALO, CONV_CH), f32),
                        pltpu.VMEM((SUBLANES, tc + 2 * CONV_HALO - SUBLANES, CONV_CH), f32)],
        compiler_params=_cparams(("parallel", "parallel")),
        name="conv_branch",
    )(p, p, p, conv_w, vec(conv_b), vec(ln_g), vec(ln_b))


FNET_GROUP_DIM = 64


def _fft_kernel(f_ref, cs_ref, bd_ref, o_ref, rhs_ref, *, T, scale):
    i = pl.program_id(0)
    b = pl.program_id(1)

    @pl.when(i == 0)
    def _():
        rows = min(512, T)
        for r in range(T // rows):
            z = _dot(f_ref[0, r * rows:(r + 1) * rows, :].astype(bf16), bd_ref[...])
            rhs_ref[b, r * rows:(r + 1) * rows, :] = z[:, :FNET_W].astype(bf16)
            rhs_ref[b, T + r * rows:T + (r + 1) * rows, :] = (-z[:, FNET_W:]).astype(bf16)

    o_ref[0] = (_dot(cs_ref[...], rhs_ref[b]) * scale).astype(o_ref.dtype)


def _dft_time_mats(T):
    t = jnp.arange(T, dtype=jnp.int32)
    ang = ((t[:, None] * t[None, :]) % T).astype(f32) * f32(2.0 * math.pi / T)
    return jnp.concatenate([jnp.cos(ang), jnp.sin(ang)], axis=1).astype(bf16)


def _dft_chan_mats():
    n = FNET_GROUP_DIM
    k = np.arange(n)
    ang = 2.0 * np.pi * ((k[:, None] * k[None, :]) % n) / n
    eye = np.eye(FNET_W // n)
    return jnp.asarray(np.concatenate([np.kron(eye, np.cos(ang)), np.kron(eye, np.sin(ang))], axis=1), f32).astype(bf16)


def _fft_call(pf, cs, bd):
    B, T, _ = pf.shape
    p = pf
    tm = min(512, T)
    kern = functools.partial(_fft_kernel, T=T, scale=1.0 / math.sqrt(T * FNET_GROUP_DIM))
    return pl.pallas_call(
        kern,
        grid=(T // tm, B),
        in_specs=[pl.BlockSpec((1, T, FNET_W), lambda i, b: (b, 0, 0)),
                  pl.BlockSpec((tm, 2 * T), lambda i, b: (i, 0)),
                  pl.BlockSpec((FNET_W, 2 * FNET_W), lambda i, b: (0, 0))],
        out_specs=pl.BlockSpec((1, tm, FNET_W), lambda i, b: (b, i, 0)),
        out_shape=jax.ShapeDtypeStruct((B, T, FNET_W), bf16),
        scratch_shapes=[pltpu.VMEM((B, 2 * T, FNET_W), bf16)],
        compiler_params=_cparams(("arbitrary", "arbitrary")),
        name="fourier_branch",
    )(p, cs, bd)


FFT_N = 64
FFT_CHUNK = 32


def _fft_s1_kernel(x_ref, cs_ref, cd_ref, twr_ref, twi_ref, o_ref, scr_ref):
    n, w = FFT_N, FNET_W
    for j in range(FFT_CHUNK):
        a = _dot(cs_ref[...], x_ref[0, :, j, :].astype(bf16))
        scr_ref[j * n:(j + 1) * n, 0:w] = a[0:n].astype(bf16)
        scr_ref[j * n:(j + 1) * n, w:2 * w] = a[n:2 * n].astype(bf16)
    z = _dot(scr_ref[...], cd_ref[...])
    twr = jnp.concatenate([twr_ref[...]] * (w // LANES), axis=1)
    twi = jnp.concatenate([twi_ref[...]] * (w // LANES), axis=1)
    zr, zi = z[:, :w], z[:, w:]
    br = zr * twr - zi * twi
    bi = zr * twi + zi * twr
    for j in range(FFT_CHUNK):
        o_ref[0, j, :, 0:w] = br[j * n:(j + 1) * n]
        o_ref[0, j, :, w:2 * w] = bi[j * n:(j + 1) * n]


def _fft_s2_kernel(b_ref, cs_ref, o_ref, *, scale):
    n, w = FFT_N, FNET_W
    for j in range(FFT_CHUNK):
        r = _dot(cs_ref[...], b_ref[0, :, j, :].astype(bf16))
        o_ref[0, :, j, :] = (r[0:n, 0:w] + r[n:2 * n, w:2 * w]) * scale


def _fft_ct_mats():
    n, w = FFT_N, FNET_W
    k = np.arange(n)
    ang = 2.0 * np.pi * ((k[:, None] * k[None, :]) % n) / n
    c, s = np.cos(ang), np.sin(ang)
    g = FNET_GROUP_DIM
    kg = np.arange(g)
    ang_g = 2.0 * np.pi * ((kg[:, None] * kg[None, :]) % g) / g
    eye = np.eye(w // g)
    cbd, sbd = np.kron(eye, np.cos(ang_g)), np.kron(eye, np.sin(ang_g))
    cs1 = np.concatenate([c, -s], axis=0)
    cd = np.block([[cbd, -sbd], [sbd, cbd]])
    cs2 = np.concatenate([c, s], axis=0)
    t2 = np.arange(n)[:, None]
    k1 = np.arange(n)[None, :]
    tw = 2.0 * np.pi * ((t2 * k1) % (n * n)) / (n * n)
    twr = np.broadcast_to(np.cos(tw).reshape(n * n, 1), (n * n, LANES))
    twi = np.broadcast_to(-np.sin(tw).reshape(n * n, 1), (n * n, LANES))
    as_bf = lambda m: jnp.asarray(m, f32).astype(bf16)
    return as_bf(cs1), as_bf(cd), as_bf(cs2), jnp.asarray(twr, f32), jnp.asarray(twi, f32)


def _fft_ct_call(pf, mats):
    B, T, w = pf.shape
    n, ch = FFT_N, FFT_CHUNK
    cs1, cd, cs2, twr, twi = mats
    full = lambda a: pl.BlockSpec(a.shape, lambda b, i: (0,) * a.ndim)
    stage1 = pl.pallas_call(
        _fft_s1_kernel,
        grid=(B, n // ch),
        in_specs=[pl.BlockSpec((1, n, ch, w), lambda b, i: (b, 0, i, 0)), full(cs1), full(cd),
                  pl.BlockSpec((ch * n, LANES), lambda b, i: (i, 0)),
                  pl.BlockSpec((ch * n, LANES), lambda b, i: (i, 0))],
        out_specs=pl.BlockSpec((1, ch, n, 2 * w), lambda b, i: (b, i, 0, 0)),
        out_shape=jax.ShapeDtypeStruct((B, n, n, 2 * w), f32),
        scratch_shapes=[pltpu.VMEM((ch * n, 2 * w), bf16)],
        compiler_params=_cparams(("parallel", "parallel")),
        name="fourier_stage1",
    )(pf.reshape(B, n, n, w), cs1, cd, twr, twi)
    kern2 = functools.partial(_fft_s2_kernel, scale=1.0 / math.sqrt(T * FNET_GROUP_DIM))
    out = pl.pallas_call(
        kern2,
        grid=(B, n // ch),
        in_specs=[pl.BlockSpec((1, n, ch, 2 * w), lambda b, i: (b, 0, i, 0)), full(cs2)],
        out_specs=pl.BlockSpec((1, n, ch, w), lambda b, i: (b, 0, i, 0)),
        out_shape=jax.ShapeDtypeStruct((B, n, n, w), f32),
        compiler_params=_cparams(("parallel", "parallel")),
        name="fourier_stage2",
    )(stage1, cs2)
    return out.reshape(B, T, w)


def _lane_is_low(shape):
    return lax.broadcasted_iota(jnp.int32, shape, len(shape) - 1) < HEAD_DIM


def _softmax_pv_steps(make_logits, values, extra=None):
    def fold(x, op, acc):
        for c in range(x.shape[1] // LANES):
            blk = x[:, c * LANES:(c + 1) * LANES]
            acc = blk if acc is None else op(acc, blk)
        return acc

    logits = make_logits()
    yield
    m_el = None
    for s in logits:
        m_el = fold(s, jnp.maximum, m_el)
    m = m_el.max(axis=-1, keepdims=True)
    if extra is not None:
        m = jnp.maximum(m, extra)
    yield
    d_el = None
    out = None
    for s, v in zip(logits, values):
        p = jnp.exp(s - m)
        d_el = fold(p, jnp.add, d_el)
        o = _dot(p.astype(bf16), v)
        out = o if out is None else out + o
        yield
    den = d_el.sum(axis=-1, keepdims=True)
    if extra is not None:
        den = den + jnp.exp(extra - m)
    return out * (1.0 / den)


def _run_together(gens):
    results = [None] * len(gens)
    live = list(enumerate(gens))
    while live:
        for item in list(live):
            i, g = item
            try:
                next(g)
            except StopIteration as stop:
                results[i] = stop.value
                live.remove(item)
    return results


def _softmax_pv(logits, values, extra=None):
    return _run_together([_softmax_pv_steps(lambda: logits, values, extra)])[0]


NAT_QROWS = 4
NAT_GROUPS = 8
NAT_BAND = NAT_QROWS + NAT_KH
NAT_TQ = NAT_QROWS * GRID_W
NAT_TK = NAT_BAND * GRID_W
NAT_ROWS = 64
NAT_EDGE_GROUPS = -(-(NAT_KH // 2) // NAT_QROWS)
NQ_COL, NK_COL, NV_COL = 2, 3, 4


def _nat_band_start(group):
    return jnp.clip(group * NAT_QROWS - NAT_KH // 2, 0, NAT_ROWS - NAT_BAND)


def _nat_kernel(q_ref, k_ref, v_ref, kc_ref, vc_ref, bias_ref, o_ref):
    j = pl.program_id(1)
    ngroups = NAT_ROWS // NAT_QROWS
    low = _lane_is_low((NAT_TQ, LANES))
    for g in range(NAT_GROUPS):
        group = j * NAT_GROUPS + g
        start = pl.multiple_of(_nat_band_start(group) * GRID_W, NAT_QROWS * GRID_W)
        var = jnp.where(group < NAT_EDGE_GROUPS, group + 1,
                        jnp.where(group >= ngroups - NAT_EDGE_GROUPS,
                                  group - (ngroups - NAT_EDGE_GROUPS) + NAT_EDGE_GROUPS + 1, 0))
        rows = slice(g * NAT_TQ, (g + 1) * NAT_TQ)
        heads = []
        for c in range(NAT_HEADS // 2):
            cols = slice(c * LANES, (c + 1) * LANES)
            q2 = q_ref[0, rows, cols] * 0.125
            k2 = k_ref[0, pl.ds(start, NAT_TK), cols]
            v2 = v_ref[0, pl.ds(start, NAT_TK), cols]
            kc2 = kc_ref[0, :, cols]
            vc2 = vc_ref[0, :, cols]

            def logits_of(e, q2=q2, k2=k2, kc2=kc2, c=c):
                qm = jnp.where(low if e == 0 else jnp.logical_not(low), q2, jnp.zeros_like(q2))
                return [_dot_nt(qm, k2) + bias_ref[var, 2 * c + e], _dot_nt(qm, kc2)]

            heads += [_softmax_pv_steps(functools.partial(logits_of, e), [v2, vc2]) for e in range(2)]
        outs = _run_together(heads)
        for c in range(NAT_HEADS // 2):
            cols = slice(c * LANES, (c + 1) * LANES)
            o_ref[0, rows, cols] = jnp.where(low, outs[2 * c], outs[2 * c + 1]).astype(o_ref.dtype)


def _nat_bias_tables(rpb):
    rows = NAT_ROWS
    ngroups = rows // NAT_QROWS
    reps = [NAT_EDGE_GROUPS] + list(range(NAT_EDGE_GROUPS)) + list(range(ngroups - NAT_EDGE_GROUPS, ngroups))
    nv = len(reps)
    a = np.arange(NAT_QROWS)[:, None]
    m = np.arange(NAT_BAND)[None, :]
    sel_r = np.zeros((nv, NAT_QROWS, NAT_BAND, 2 * NAT_KH - 1), np.float32)
    ok_r = np.zeros((nv, NAT_QROWS, NAT_BAND), bool)
    for v, jj in enumerate(reps):
        band0 = int(np.clip(jj * NAT_QROWS - NAT_KH // 2, 0, rows - NAT_BAND))
        qr = jj * NAT_QROWS + a
        kr = band0 + m
        rstart = np.clip(qr - NAT_KH // 2, 0, rows - NAT_KH)
        ok = (kr >= rstart) & (kr < rstart + NAT_KH)
        dr = np.clip(kr - qr + NAT_KH - 1, 0, 2 * NAT_KH - 2)
        ok_r[v] = ok
        sel_r[v] = np.eye(2 * NAT_KH - 1, dtype=np.float32)[dr] * ok[..., None]
    qc = np.arange(GRID_W)[:, None]
    kc = np.arange(GRID_W)[None, :]
    cstart = np.clip(qc - NAT_KW // 2, 0, GRID_W - NAT_KW)
    ok_c = (kc >= cstart) & (kc < cstart + NAT_KW)
    dc = np.clip(kc - qc + NAT_KW - 1, 0, 2 * NAT_KW - 2)
    sel_c = np.eye(2 * NAT_KW - 1, dtype=np.float32)[dc] * ok_c[..., None]
    valid = ok_r[:, :, None, :, None] & ok_c[None, None, :, None, :]
    valid = valid.reshape(nv, 1, NAT_TQ, NAT_TK)
    t = jnp.einsum('vamr,hrc,qkc->vhaqmk', jnp.asarray(sel_r), rpb.astype(f32), jnp.asarray(sel_c),
                   precision=lax.Precision.HIGHEST)
    t = t.reshape(nv, NAT_HEADS, NAT_TQ, NAT_TK)
    return jnp.where(jnp.asarray(valid), t, NEG_INF)


def _nat_call(p, pc, bias):
    B, T, _ = p.shape
    assert T == NAT_ROWS * GRID_W
    L = pc.shape[1]
    W = NAT_HEADS * HEAD_DIM
    tq = NAT_GROUPS * NAT_TQ
    return pl.pallas_call(
        _nat_kernel,
        grid=(B, T // tq),
        in_specs=[pl.BlockSpec((1, tq, W), lambda b, j: (b, j, NQ_COL)),
                  pl.BlockSpec((1, T, W), lambda b, j: (b, 0, NK_COL)),
                  pl.BlockSpec((1, T, W), lambda b, j: (b, 0, NV_COL)),
                  pl.BlockSpec((1, L, W), lambda b, j: (b, 0, NK_COL)),
                  pl.BlockSpec((1, L, W), lambda b, j: (b, 0, NV_COL)),
                  pl.BlockSpec(bias.shape, lambda b, j: (0, 0, 0, 0), pipeline_mode=pl.Buffered(1))],
        out_specs=pl.BlockSpec((1, tq, W), lambda b, j: (b, j, 0)),
        out_shape=jax.ShapeDtypeStruct((B, T, W), bf16),
        compiler_params=_cparams(("parallel", "arbitrary")),
        name="nat_attention",
    )(p, p, p, pc, pc, bias)


GQ_COL = 1536 // 512
GK_COL = 2048 // LANES
GV_COL = 2176 // LANES
GQA_R = GQA_Q_HEADS // GQA_KV_HEADS


def _rope(x, cos, sin):
    lane = lax.broadcasted_iota(jnp.int32, x.shape, 1)
    first = (lane % (HEAD_DIM // 2)) < (HEAD_DIM // 4)
    swapped = jnp.where(first, pltpu.roll(x, LANES - HEAD_DIM // 4, 1), pltpu.roll(x, HEAD_DIM // 4, 1))
    return x * cos + swapped * sin


def _rope_tables(S):
    t = np.arange(S)
    pos = np.stack([t // GRID_W, t % GRID_W], axis=1).astype(np.float64)
    quarter = HEAD_DIM // 4
    freqs = ROPE_BASE ** (-np.arange(quarter, dtype=np.float64) / quarter)
    lane = np.arange(LANES)
    which = (lane % HEAD_DIM) // (HEAD_DIM // 2)
    ang = pos[:, which] * freqs[lane % quarter][None, :]
    sign = np.where((lane % (HEAD_DIM // 2)) < quarter, -1.0, 1.0)[None, :]
    return jnp.asarray(np.cos(ang), f32), jnp.asarray(np.sin(ang) * sign, f32)


GQA_QB = 8


def _gqa_kernel(q_ref, kvp_ref, kvm_ref, kvn_ref, kvx_ref, tp_ref, tm_ref, tn_ref, sink_ref, band_ref, o_ref, *, nsteps):
    n = pl.program_id(1)
    blk = GQA_BLOCK
    rows = GQA_R * blk
    k_of = lambda ref, r0=None: ref[0, :, 0:LANES] if r0 is None else ref[0, r0:r0 + blk, 0:LANES]
    v_of = lambda ref, r0=None: ref[0, :, LANES:2 * LANES] if r0 is None else ref[0, r0:r0 + blk, LANES:2 * LANES]
    rope_k = lambda k, t: _rope(k.astype(f32), t[:, 0:LANES], t[:, LANES:2 * LANES]).astype(bf16)
    tmid = [tm_ref[i * blk:(i + 1) * blk, :] for i in range(GQA_QB)]
    kb = [rope_k(k_of(kvp_ref), tp_ref[...])] + [rope_k(k_of(kvm_ref, i * blk), tmid[i]) for i in range(GQA_QB)] \
        + [rope_k(k_of(kvn_ref), tn_ref[...])]
    vb = [v_of(kvp_ref)] + [v_of(kvm_ref, i * blk) for i in range(GQA_QB)] + [v_of(kvn_ref)]
    kx, vx = kvx_ref[0, :, 0:LANES], kvx_ref[0, :, LANES:2 * LANES]
    low = _lane_is_low((rows, LANES))
    for i in range(GQA_QB):
        cos_q, sin_q = tmid[i][:, 0:LANES], tmid[i][:, LANES:2 * LANES]
        qs = [(_rope(q_ref[0, i * blk:(i + 1) * blk, r * LANES:(r + 1) * LANES].astype(f32), cos_q, sin_q) * 0.125)
              for r in range(GQA_R)]
        q_all = jnp.concatenate(qs, axis=0)
        has_prev = (n > 0) if i == 0 else True
        has_next = (n < nsteps - 1) if i == GQA_QB - 1 else True
        bias_p = band_ref[0] if has_prev is True else jnp.minimum(band_ref[0], jnp.where(has_prev, 0.0, NEG_INF))
        bias_n = band_ref[1] if has_next is True else jnp.minimum(band_ref[1], jnp.where(has_next, 0.0, NEG_INF))
        halves = []
        for g in range(GQA_KV_HEADS):
            qm = jnp.where(low if g == 0 else jnp.logical_not(low), q_all, 0.0).astype(bf16)
            logits = [_dot_nt(qm, kb[i]) + bias_p, _dot_nt(qm, kb[i + 1]), _dot_nt(qm, kb[i + 2]) + bias_n,
                      _dot_nt(qm, kx)]
            vals = [vb[i], vb[i + 1], vb[i + 2], vx]
            halves.append(_softmax_pv(logits, vals, extra=sink_ref[g][:, 0:1]))
        out = jnp.where(low, halves[0], halves[1])
        for r in range(GQA_R):
            o_ref[0, i * blk:(i + 1) * blk, r * LANES:(r + 1) * LANES] = out[r * blk:(r + 1) * blk].astype(o_ref.dtype)


def _sink_cols(sink, blk):
    s = sink.astype(f32).reshape(GQA_KV_HEADS, GQA_R, 1, 1)
    return jnp.broadcast_to(s, (GQA_KV_HEADS, GQA_R, blk, LANES)).reshape(GQA_KV_HEADS, GQA_R * blk, LANES)


def _gqa_band_masks():
    qi = np.arange(GQA_R * GQA_BLOCK)[:, None] % GQA_BLOCK
    kj = np.arange(GQA_BLOCK)[None, :]
    return jnp.asarray(np.stack([np.where(kj >= qi, 0.0, NEG_INF), np.where(kj <= qi, 0.0, NEG_INF)]), f32)


def _gqa_call(p, pc, cos, sin, sink):
    B, T, _ = p.shape
    L = pc.shape[1]
    blk = GQA_BLOCK
    nb = T // blk
    nsteps = nb // GQA_QB
    QW = GQA_Q_HEADS * HEAD_DIM
    kvw = 2 * LANES
    kv_col = (GK_COL * LANES) // kvw
    assert GV_COL == GK_COL + 1 and GK_COL % 2 == 0
    tabs = jnp.concatenate([cos, sin], axis=1)
    kern = functools.partial(_gqa_kernel, nsteps=nsteps)
    prev = lambda n: jnp.maximum(n * GQA_QB - 1, 0)
    nxt = lambda n: jnp.minimum((n + 1) * GQA_QB, nb - 1)
    once = dict(pipeline_mode=pl.Buffered(1))
    return pl.pallas_call(
        kern,
        grid=(B, nsteps),
        in_specs=[pl.BlockSpec((1, GQA_QB * blk, QW), lambda b, n: (b, n, GQ_COL)),
                  pl.BlockSpec((1, blk, kvw), lambda b, n: (b, prev(n), kv_col)),
                  pl.BlockSpec((1, GQA_QB * blk, kvw), lambda b, n: (b, n, kv_col)),
                  pl.BlockSpec((1, blk, kvw), lambda b, n: (b, nxt(n), kv_col)),
                  pl.BlockSpec((1, L, kvw), lambda b, n: (b, 0, kv_col)),
                  pl.BlockSpec((blk, kvw), lambda b, n: (prev(n), 0)),
                  pl.BlockSpec((GQA_QB * blk, kvw), lambda b, n: (n, 0)),
                  pl.BlockSpec((blk, kvw), lambda b, n: (nxt(n), 0)),
                  pl.BlockSpec((GQA_KV_HEADS, GQA_R * blk, LANES), lambda b, n: (0, 0, 0), **once),
                  pl.BlockSpec((2, GQA_R * blk, blk), lambda b, n: (0, 0, 0), **once)],
        out_specs=pl.BlockSpec((1, GQA_QB * blk, QW), lambda b, n: (b, n, 0)),
        out_shape=jax.ShapeDtypeStruct((B, T, QW), bf16),
        compiler_params=_cparams(("parallel", "parallel")),
        name="window_gqa",
    )(p, p, p, p, pc, tabs, tabs, tabs, _sink_cols(sink, blk), _gqa_band_masks())


def _ctx_attn_kernel(nq_ref, nk_ref, nv_ref, gq_ref, gk_ref, gv_ref, sink_ref, ob_ref, od_ref, *, L):
    low = _lane_is_low((L, LANES))
    for c in range(NAT_HEADS // 2):
        cols = slice(c * LANES, (c + 1) * LANES)
        q2 = nq_ref[0, :, cols] * 0.125
        halves = []
        for e in range(2):
            qm = jnp.where(low if e == 0 else jnp.logical_not(low), q2, jnp.zeros_like(q2))
            halves.append(_softmax_pv([_dot_nt(qm, nk_ref[0, :, cols])], [nv_ref[0, :, cols]]))
        ob_ref[0, :, cols] = jnp.where(low, halves[0], halves[1]).astype(ob_ref.dtype)
    q_all = jnp.concatenate([gq_ref[0, :, r * LANES:(r + 1) * LANES] for r in range(GQA_R)], axis=0) * 0.125
    low4 = _lane_is_low((GQA_R * L, LANES))
    halves = []
    for g in range(GQA_KV_HEADS):
        qm = jnp.where(low4 if g == 0 else jnp.logical_not(low4), q_all, jnp.zeros_like(q_all))
        halves.append(_softmax_pv([_dot_nt(qm, gk_ref[0])], [gv_ref[0]], extra=sink_ref[g][:, 0:1]))
    out = jnp.where(low4, halves[0], halves[1])
    for r in range(GQA_R):
        od_ref[0, :, r * LANES:(r + 1) * LANES] = out[r * L:(r + 1) * L].astype(od_ref.dtype)


def _ctx_attn_call(pc, sink):
    B, L, _ = pc.shape
    W = NAT_HEADS * HEAD_DIM
    QW = GQA_Q_HEADS * HEAD_DIM
    kern = functools.partial(_ctx_attn_kernel, L=L)
    return pl.pallas_call(
        kern,
        grid=(B,),
        in_specs=[pl.BlockSpec((1, L, W), lambda b: (b, 0, NQ_COL)),
                  pl.BlockSpec((1, L, W), lambda b: (b, 0, NK_COL)),
                  pl.BlockSpec((1, L, W), lambda b: (b, 0, NV_COL)),
                  pl.BlockSpec((1, L, QW), lambda b: (b, 0, GQ_COL)),
                  pl.BlockSpec((1, L, LANES), lambda b: (b, 0, GK_COL)),
                  pl.BlockSpec((1, L, LANES), lambda b: (b, 0, GV_COL)),
                  pl.BlockSpec((GQA_KV_HEADS, GQA_R * L, LANES), lambda b: (0, 0, 0))],
        out_specs=[pl.BlockSpec((1, L, W), lambda b: (b, 0, 0)),
                   pl.BlockSpec((1, L, QW), lambda b: (b, 0, 0))],
        out_shape=[jax.ShapeDtypeStruct((B, L, W), bf16), jax.ShapeDtypeStruct((B, L, QW), bf16)],
        compiler_params=_cparams(("parallel",)),
        name="context_attention",
    )(pc, pc, pc, pc, pc, pc, _sink_cols(sink, L))


def _merge_kernel(x_ref, sh_ref, sc_ref, g1_ref, ya_ref, yb_ref, yc_ref, yd_ref, wg_ref, bg_ref,
                  wa_ref, wb_ref, wc_ref, wd_ref, wo_ref, lg_ref, lb_ref, o_ref):
    D = D_MODEL
    x = x_ref[0]
    xm = (_ln(x) * (1.0 + sc_ref[0]) + sh_ref[0]).astype(bf16)
    m = None
    for j, (y_ref, w_ref) in enumerate(((ya_ref, wa_ref), (yb_ref, wb_ref), (yc_ref, wc_ref), (yd_ref, wd_ref))):
        gate = jax.nn.sigmoid(_dot(xm, wg_ref[:, j * D:(j + 1) * D]) + bg_ref[:, j * D:(j + 1) * D])
        t = gate * _dot(y_ref[0].astype(bf16), w_ref[...])
        m = t if m is None else m + t
    mix = _dot(m.astype(bf16), wo_ref[...])
    z = DEEPNORM_ALPHA * x + g1_ref[0] * mix
    o_ref[0] = _ln(z) * lg_ref[...] + lb_ref[...]


MERGE_TM = 512


def _merge_call(x, sh, sc, g1, ya, yb, yc, yd, wg, bg, wa, wb, wc, wd, wo, lg, lb, tm):
    B, T, D = x.shape
    tok = lambda w: pl.BlockSpec((1, tm, w), lambda b, i: (b, i, 0))
    mod = pl.BlockSpec((1, 1, D), lambda b, i: (b, 0, 0))
    full = lambda a: pl.BlockSpec(a.shape, lambda b, i: (0,) * a.ndim, pipeline_mode=pl.Buffered(1))
    return pl.pallas_call(
        _merge_kernel,
        grid=(B, T // tm),
        in_specs=[tok(D), mod, mod, mod, tok(ya.shape[-1]), tok(yb.shape[-1]), tok(yc.shape[-1]), tok(yd.shape[-1]),
                  full(wg), full(bg), full(wa), full(wb), full(wc), full(wd), full(wo), full(lg), full(lb)],
        out_specs=tok(D),
        out_shape=jax.ShapeDtypeStruct((B, T, D), f32),
        compiler_params=_cparams(("parallel", "parallel")),
        name="merge_ln",
    )(x, sh, sc, g1, ya, yb, yc, yd, wg, bg, wa, wb, wc, wd, wo, lg, lb)


PEER_TM = 512
PEER_EBLK = 16
PEER_SEL_HEADS = 8
PEER_DENSE_J = 4
BIG_NEG = -3.0e38
SQRT_HALF = 0.7071067811865476


def _bitonic_merge(vs):
    n = len(vs)
    k = n // 2
    while k >= 1:
        for i in range(n):
            if i % (2 * k) < k:
                hi, lo = jnp.maximum(vs[i], vs[i + k]), jnp.minimum(vs[i], vs[i + k])
                vs[i], vs[i + k] = hi, lo
        yield
        k //= 2
    return vs


def _sort_desc(vs):
    if len(vs) == 1:
        return list(vs)
    half = len(vs) // 2
    top = yield from _sort_desc(vs[:half])
    bot = yield from _sort_desc(vs[half:])
    return (yield from _bitonic_merge(top + bot[::-1]))


def _sorted_top(s, count):
    m = s.shape[0] // SUBLANES
    vs = yield from _sort_desc([s[i * SUBLANES:(i + 1) * SUBLANES] for i in range(m)])
    shift = SUBLANES // 2
    if len(vs) < count:
        other = [pltpu.roll(v, shift, 0) for v in vs]
        vs = yield from _bitonic_merge(vs + other[::-1])
        shift //= 2
    while shift >= 1:
        n = len(vs)
        vs = yield from _bitonic_merge([jnp.maximum(vs[i], pltpu.roll(vs[n - 1 - i], shift, 0)) for i in range(n)])
        shift //= 2
    return vs[:count]


def _peer_select_chunk(h, s1t, s2t, ln, n1_ref, e1_ref, r2_ref, e2_ref):
    k = PEER_TOPK
    pairs = [(i, j) for i in range(k) for j in range(k) if (i + 1) * (j + 1) <= k]
    cand_rows = SUBLANES * pl.next_power_of_2(-(-len(pairs) // SUBLANES))
    top_ranks = [a for a in range(k) if (a + 1) * (PEER_DENSE_J + 1) <= k]
    a = [v[0:1, :] for v in (yield from _sorted_top(s1t, k))]
    b = [v[0:1, :] for v in (yield from _sorted_top(s2t, k))]
    rank2 = jnp.full(s2t.shape, float(k), f32)
    for r in reversed(range(k)):
        rank2 = jnp.where(s2t == b[r], float(r), rank2)
        if r % 4 == 0:
            yield
    r2_ref[h, :, ln] = rank2.astype(bf16)
    e2_ref[h, :, ln] = jnp.exp(s2t - b[0]).astype(bf16)
    yield
    cand = jnp.concatenate([a[i] + b[j] for i, j in pairs]
                           + [jnp.full((cand_rows - len(pairs), LANES), BIG_NEG, f32)], axis=0)
    tau = (yield from _sorted_top(cand, k))[k - 1][0:1, :]
    z = jnp.sum(jnp.where(cand >= tau, jnp.exp(cand - (a[0] + b[0])), 0.0), axis=0, keepdims=True)
    yield
    n1 = jnp.zeros_like(s1t)
    for j in range(PEER_DENSE_J):
        n1 = n1 + jnp.where(s1t + b[j] >= tau, 1.0, 0.0)
        yield
    for r in top_ranks:
        extra = jnp.zeros_like(tau)
        for j in range(PEER_DENSE_J, k // (r + 1)):
            extra = extra + jnp.where(a[r] + b[j] >= tau, 1.0, 0.0)
        n1 = n1 + jnp.where(s1t == a[r], extra, 0.0)
        yield
    n1_ref[h, :, ln] = n1
    e1_ref[h, :, ln] = jnp.exp(s1t - a[0]) * (0.5 / z)
    yield


def _peer_select_lanes(h, off, nlanes, ut, wq_ref, k1_ref, k2_ref, n1_ref, e1_ref, r2_ref, e2_ref):
    nk = PEER_N_KEYS
    r0 = pl.multiple_of(h * 2 * nk, 2 * nk)
    u_cols = ut[:, pl.ds(off, nlanes)]
    q1 = _dot(wq_ref[pl.ds(r0, nk), :], u_cols).astype(bf16)
    yield
    q2 = _dot(wq_ref[pl.ds(r0 + nk, nk), :], u_cols).astype(bf16)
    yield
    s1 = _dot(k1_ref[h], q1)
    s2 = _dot(k2_ref[h], q2)
    yield
    chunks = [_peer_select_chunk(h, s1[:, t * LANES:(t + 1) * LANES], s2[:, t * LANES:(t + 1) * LANES],
                                 pl.ds(off + t * LANES, LANES), n1_ref, e1_ref, r2_ref, e2_ref)
              for t in range(nlanes // LANES)]
    while chunks:
        for g in list(chunks):
            try:
                next(g)
                yield
            except StopIteration:
                chunks.remove(g)


def _peer_kernel(x_ref, sh_ref, sc_ref, g2_ref, wq_ref, k1_ref, k2_ref, eu_ref, ev_ref, lg_ref, lb_ref,
                 o_ref, ut_ref, n1_ref, e1_ref, r2_ref, e2_ref, acc_ref, *, tm, nsteps):
    step = pl.program_id(2)
    nk = PEER_N_KEYS

    @pl.when(step == 0)
    def _select():
        u = _ln(x_ref[0]) * (1.0 + sc_ref[0]) + sh_ref[0]
        ut_ref[...] = u.T.astype(bf16)
        acc_ref[...] = jnp.zeros_like(acc_ref)

        def heads(i, carry):
            _run_together([_peer_select_lanes(i * PEER_SEL_HEADS + d, 0, tm, ut_ref, wq_ref, k1_ref, k2_ref,
                                              n1_ref, e1_ref, r2_ref, e2_ref) for d in range(PEER_SEL_HEADS)])
            return carry

        lax.fori_loop(0, PEER_HEADS // PEER_SEL_HEADS, heads, 0)

    gs = []
    for b in range(PEER_EBLK):
        i1 = step * PEER_EBLK + b
        ht = _dot(eu_ref[b * nk:(b + 1) * nk, :], ut_ref[...])
        w = jnp.zeros((nk, tm), bf16)
        for h in range(PEER_HEADS):
            n1 = n1_ref[h, pl.ds(i1, 1), :].astype(bf16)
            e1 = e1_ref[h, pl.ds(i1, 1), :].astype(bf16)
            w = w + jnp.where(r2_ref[h] < n1, e2_ref[h] * e1, jnp.zeros((), bf16))
        act = ht * (1.0 + lax.erf(ht * SQRT_HALF))
        gs.append(w * act.astype(bf16))
    acc_ref[...] += _dot(ev_ref[...], jnp.concatenate(gs, axis=0))

    @pl.when(step == nsteps - 1)
    def _finish():
        z = DEEPNORM_ALPHA * x_ref[0] + g2_ref[0] * acc_ref[...].T
        o_ref[0] = _ln(z) * lg_ref[...] + lb_ref[...]


def _peer_call(x, sh, sc, g2, wq_t, k1, k2, eu, ev_t, lg, lb):
    B, T, D = x.shape
    tm = PEER_TM
    nk = PEER_N_KEYS
    ne = eu.shape[0]
    eb = PEER_EBLK * nk
    nsteps = ne // eb
    kern = functools.partial(_peer_kernel, tm=tm, nsteps=nsteps)
    mod = pl.BlockSpec((1, 1, D), lambda b, i, s: (b, 0, 0))
    full = lambda a, **kw: pl.BlockSpec(a.shape, lambda b, i, s: (0,) * a.ndim, **kw)
    sel = lambda dt: pltpu.VMEM((PEER_HEADS, nk, tm), dt)
    return pl.pallas_call(
        kern,
        grid=(B, T // tm, nsteps),
        in_specs=[pl.BlockSpec((1, tm, D), lambda b, i, s: (b, i, 0)), mod, mod, mod,
                  full(wq_t, pipeline_mode=pl.Buffered(1)), full(k1), full(k2),
                  pl.BlockSpec((eb, D), lambda b, i, s: (s, 0)),
                  pl.BlockSpec((D, eb), lambda b, i, s: (0, s)),
                  full(lg), full(lb)],
        out_specs=pl.BlockSpec((1, tm, D), lambda b, i, s: (b, i, 0)),
        out_shape=jax.ShapeDtypeStruct((B, T, D), f32),
        scratch_shapes=[pltpu.VMEM((D, tm), bf16), sel(f32), sel(f32), sel(bf16), sel(bf16),
                        pltpu.VMEM((D, tm), f32)],
        compiler_params=_cparams(("parallel", "parallel", "arbitrary")),
        name="peer_ffn",
    )(x, sh, sc, g2, wq_t, k1, k2, eu, ev_t, lg, lb)


def _pair_kv_groups(a, axis):
    shape = a.shape
    a = a.reshape(shape[:axis] + (GQA_KV_HEADS, GQA_R, HEAD_DIM) + shape[axis + 1:])
    return jnp.swapaxes(a, axis, axis + 1).reshape(shape)


def kernel(x, c, ctx, c_ctx, w_ada, b_ada, w_in, b_in, conv_w, conv_b, conv_ln_g, conv_ln_b, nat_rpb, gqa_sink,
           w_branch_a, w_branch_b, w_branch_c, w_branch_d, w_out, ln1_g, ln1_b, peer_wq, peer_k1, peer_k2,
           peer_u, peer_v, ln2_g, ln2_b):
    B, S, D = x.shape
    L = ctx.shape[1]
    gq0, gq1 = 1536, 2048

    cvec = jnp.concatenate([c, c_ctx[None, :], jnp.zeros((8 - B - 1, D), f32)], axis=0)
    mod = _ada_call(cvec, w_ada, b_ada)
    cos, sin = _rope_tables(S)
    assert S == FFT_N * FFT_N
    ct_mats = _fft_ct_mats()
    cs_ctx = _dft_time_mats(L)
    bd = _dft_chan_mats()
    row = lambda v: v.reshape(1, -1)

    xc = ctx.reshape(1, B * L, D)
    for i in range(DEPTH):
        lat = [mod[i, :B, k * D:(k + 1) * D].reshape(B, 1, D) for k in range(6)]
        con = [mod[i, B:B + 1, k * D:(k + 1) * D].reshape(1, 1, D) for k in range(6)]
        wi, bi = w_in[i], b_in[i]
        w_small = jnp.concatenate([wi[:, :gq0], _pair_kv_groups(wi[:, gq0:gq1], 1), wi[:, gq1:N_SMALL]],
                                  axis=1).astype(bf16)
        b_small = row(jnp.concatenate([bi[:gq0], _pair_kv_groups(bi[gq0:gq1], 0), bi[gq1:N_SMALL]]))
        w_gate = wi[:, N_SMALL:].astype(bf16)
        b_gate = row(bi[N_SMALL:])
        wa, wb, wc = w_branch_a[i].astype(bf16), w_branch_b[i].astype(bf16), w_branch_c[i].astype(bf16)
        wd = _pair_kv_groups(w_branch_d[i], 0).astype(bf16)
        wo = w_out[i].astype(bf16)
        merge_w = (w_gate, b_gate, wa, wb, wc, wd, wo, row(ln1_g[i]), row(ln1_b[i]))
        peer_w = (peer_wq[i].T.astype(bf16), peer_k1[i].astype(bf16), peer_k2[i].astype(bf16),
                  peer_u[i].astype(bf16), peer_v[i].T.astype(bf16), row(ln2_g[i]), row(ln2_b[i]))
        conv_p = (conv_w[i], conv_b[i], conv_ln_g[i], conv_ln_b[i])

        pc, pcf = _inproj_call(xc, con[0], con[1], w_small, b_small, INPROJ_TM)
        pc, pcf = pc.reshape(B, L, N_SMALL), pcf.reshape(B, L, FNET_W)
        p, pf = _inproj_call(x, lat[0], lat[1], w_small, b_small, INPROJ_TM)
        y_a = _conv_call(p, *conv_p)
        y_b = _nat_call(p, pc, _nat_bias_tables(nat_rpb[i]))
        y_c = _fft_ct_call(pf, ct_mats)
        y_d = _gqa_call(p, pc, cos, sin, gqa_sink[i])
        x = _merge_call(x, lat[0], lat[1], lat[2], y_a, y_b, y_c, y_d, *merge_w, MERGE_TM)
        x = _peer_call(x, lat[3], lat[4], lat[5], *peer_w)

        if i < DEPTH - 1:
            yc_a = _conv_call(pc, *conv_p)
            yc_b, yc_d = _ctx_attn_call(pc, gqa_sink[i])
            yc_c = _fft_call(pcf, cs_ctx, bd)
            flat = lambda a: a.reshape(1, B * L, a.shape[-1])
            xc = _merge_call(xc, con[0], con[1], con[2], flat(yc_a), flat(yc_b), flat(yc_c), flat(yc_d),
                             *merge_w, MERGE_TM)
            xc = _peer_call(xc, con[3], con[4], con[5], *peer_w)
    return x
```
